```python
import math
import jax
import jax.numpy as jnp
from jax import lax
import numpy as np

D_MODEL = 1024
BATCH = 8
SEQ = 2048
DEPTH = 2
DEC_BATCH = 128
DEC_SEQ = 4
PAST_LEN = 16384
PAGE_SIZE = 128

HEAD_DIM = 64
MIX_WIDTH = D_MODEL
GROUP_WIDTH = MIX_WIDTH // 4
H_A = GROUP_WIDTH // HEAD_DIM
DK_A = HEAD_DIM
DV_A = HEAD_DIM
H_B = GROUP_WIDTH // HEAD_DIM
P_B = HEAD_DIM
N_B = 64
G_B = 2
H_C = GROUP_WIDTH // HEAD_DIM
DK_C = HEAD_DIM
DV_C = HEAD_DIM
H_D = GROUP_WIDTH // HEAD_DIM
DK_D = HEAD_DIM
DV_D = HEAD_DIM

CONV_K = 4
CHUNK = 64
D_FF = 4 * D_MODEL
EPS = 1e-6
NEG = -1e30
DEEPNORM_ALPHA = (2 * DEPTH) ** 0.25
DEEPNORM_BETA = (8 * DEPTH) ** -0.25

QK_A = H_A * DK_A
W_A = H_A * DV_A
CONV_A_CH = 2 * QK_A + W_A
W_B = H_B * P_B
CONV_B_CH = W_B + 2 * G_B * N_B
QK_C = H_C * DK_C
W_C = H_C * DV_C
QK_D = H_D * DK_D
W_D = H_D * DV_D
MIX_OUT = W_A + W_B + W_C + W_D
IN_SIZES = (CONV_A_CH, W_A, H_A, H_A,
            W_B, CONV_B_CH, H_B,
            QK_C, QK_C, W_C, W_C,
            QK_D, QK_D, W_D, W_D, H_D, H_D)
IN_DIM = sum(IN_SIZES)

kernel_name = 'hybrid_deltanet_ssd_hgrn2_mlstm_decode_step'


def _split(u, sizes):
    offs = np.cumsum(sizes)[:-1].tolist()
    return jnp.split(u, offs, axis=-1)


def _layernorm(x, g, b):
    x32 = x.astype(jnp.float32)
    mu = jnp.mean(x32, axis=-1, keepdims=True)
    xc = x32 - mu
    var = jnp.mean(xc * xc, axis=-1, keepdims=True)
    return (xc * lax.rsqrt(var + EPS) * g + b).astype(x.dtype)


def _rms_heads(x, g):
    y = x * lax.rsqrt(jnp.mean(x * x, axis=-1, keepdims=True) + EPS)
    return y.reshape(x.shape[:-2] + (-1,)) * g


def _l2norm(x):
    return x * lax.rsqrt(jnp.sum(x * x, axis=-1, keepdims=True) + EPS)


def _masked_exp(diff, mask):
    return jnp.where(mask, jnp.exp(jnp.where(mask, diff, 0.0)), 0.0)


def _causal_conv(x, buf, w, b):
    t = x.shape[1]
    xx = jnp.concatenate([buf.astype(x.dtype), x], axis=1)
    y = b + xx[:, 0:t] * w[0]
    for j in range(1, CONV_K):
        y = y + xx[:, j:j + t] * w[j]
    return y, xx[:, t:]


def _chunks(x, L):
    b, t, h = x.shape[:3]
    x = x.reshape((b, t // L, L, h) + x.shape[3:])
    return jnp.moveaxis(x, (1, 3), (0, 2))


def _unchunk(y):
    y = jnp.moveaxis(y, (0, 2), (1, 3))
    return y.reshape((y.shape[0], -1) + y.shape[3:])


def _masks(L):
    idx = jnp.arange(L)
    return idx[:, None] >= idx[None, :], idx[:, None] > idx[None, :]


def _gated_delta(q, k, v, log_a, beta, s0):
    L = math.gcd(q.shape[1], CHUNK)
    incl, strict = _masks(L)
    eye = jnp.eye(L, dtype=jnp.float32)
    dv = v.shape[-1]

    def step(s, inp):
        qc, kc, vc, gc, bc = inp
        g = jnp.cumsum(gc, axis=-1)
        decay = _masked_exp(g[..., :, None] - g[..., None, :], incl)
        kk = jnp.einsum('bhrd,bhjd->bhrj', kc, kc)
        m = jnp.where(strict, bc[..., :, None] * kk * decay, 0.0) + eye
        rhs = jnp.concatenate([bc[..., None] * vc, (bc * jnp.exp(g))[..., None] * kc], axis=-1)
        sol = lax.linalg.triangular_solve(m, rhs, left_side=True, lower=True, unit_diagonal=True)
        u = sol[..., :dv] - jnp.einsum('bhlk,bhkv->bhlv', sol[..., dv:], s)
        qk = jnp.einsum('bhrd,bhjd->bhrj', qc, kc) * decay
        o = (jnp.exp(g)[..., None] * jnp.einsum('bhlk,bhkv->bhlv', qc, s)
             + jnp.einsum('bhrj,bhjv->bhrv', qk, u))
        g_last = g[..., -1:]
        s_new = (jnp.exp(g_last)[..., None] * s
                 + jnp.einsum('bhjk,bhjv->bhkv', kc * jnp.exp(g_last - g)[..., None], u))
        return s_new, o

    xs = (_chunks(q, L), _chunks(k, L), _chunks(v, L), _chunks(log_a, L), _chunks(beta, L))
    s_fin, o = lax.scan(step, s0, xs)
    return _unchunk(o), s_fin


def _scalar_decay_attn(q, k, v, log_a, s0):
    L = math.gcd(q.shape[1], CHUNK)
    incl, _ = _masks(L)

    def step(s, inp):
        qc, kc, vc, gc = inp
        g = jnp.cumsum(gc, axis=-1)
        decay = _masked_exp(g[..., :, None] - g[..., None, :], incl)
        att = jnp.einsum('bhrd,bhjd->bhrj', qc, kc) * decay
        o = (jnp.exp(g)[..., None] * jnp.einsum('bhlk,bhkv->bhlv', qc, s)
             + jnp.einsum('bhrj,bhjv->bhrv', att, vc))
        g_last = g[..., -1:]
        s_new = (jnp.exp(g_last)[..., None] * s
                 + jnp.einsum('bhjk,bhjv->bhkv', kc * jnp.exp(g_last - g)[..., None], vc))
        return s_new, o

    xs = (_chunks(q, L), _chunks(k, L), _chunks(v, L), _chunks(log_a, L))
    s_fin, o = lax.scan(step, s0, xs)
    return _unchunk(o), s_fin


def _vector_decay_attn(q, k, v, log_f, s0):
    L = math.gcd(q.shape[1], CHUNK)
    incl, _ = _masks(L)
    mask3 = incl[:, :, None]

    def step(s, inp):
        qc, kc, vc, fc = inp
        g = jnp.cumsum(fc, axis=2)
        w = _masked_exp(g[:, :, :, None, :] - g[:, :, None, :, :], mask3)
        att = jnp.einsum('bhrc,bhjc,bhrjc->bhrj', qc, kc, w)
        o = (jnp.einsum('bhlc,bhcv->bhlv', qc * jnp.exp(g), s)
             + jnp.einsum('bhrj,bhjv->bhrv', att, vc))
        g_last = g[:, :, -1:, :]
        s_new = (jnp.exp(g[:, :, -1, :])[..., None] * s
                 + jnp.einsum('bhjc,bhjv->bhcv', kc * jnp.exp(g_last - g), vc))
        return s_new, o

    xs = (_chunks(q, L), _chunks(k, L), _chunks(v, L), _chunks(log_f, L))
    s_fin, o = lax.scan(step, s0, xs)
    return _unchunk(o), s_fin


def _mlstm(q, k, v, i_pre, log_f, c0, n0, m0):
    L = math.gcd(q.shape[1], CHUNK)
    incl, _ = _masks(L)

    def step(carry, inp):
        c, n, m = carry
        qc, kc, vc, ic, fc = inp
        b = jnp.cumsum(fc, axis=-1)
        d = jnp.where(incl, b[..., :, None] - b[..., None, :] + ic[..., None, :], NEG)
        init = b + m[..., None]
        m_r = jnp.maximum(init, jnp.max(d, axis=-1))
        p = _masked_exp(d - m_r[..., None], incl) * jnp.einsum('bhrd,bhjd->bhrj', qc, kc)
        s_init = jnp.exp(init - m_r)
        num = (s_init[..., None] * jnp.einsum('bhlk,bhkv->bhlv', qc, c)
               + jnp.einsum('bhrj,bhjv->bhrv', p, vc))
        den = s_init * jnp.einsum('bhlk,bhk->bhl', qc, n) + jnp.sum(p, axis=-1)
        h = num / jnp.maximum(jnp.abs(den), jnp.exp(-m_r))[..., None]
        m_last = m_r[..., -1]
        b_last = b[..., -1]
        sc = jnp.exp(b_last + m - m_last)
        kw = kc * jnp.exp(b_last[..., None] - b + ic - m_last[..., None])[..., None]
        c_new = sc[..., None, None] * c + jnp.einsum('bhjk,bhjv->bhkv', kw, vc)
        n_new = sc[..., None] * n + jnp.sum(kw, axis=2)
        return (c_new, n_new, m_last), h

    xs = (_chunks(q, L), _chunks(k, L), _chunks(v, L), _chunks(i_pre, L), _chunks(log_f, L))
    (c_f, n_f, m_f), h = lax.scan(step, (c0, n0, m0), xs)
    return _unchunk(h), c_f, n_f, m_f


def _layer(x, st, lb, w_in, conv_a_w, conv_a_b, a_log_a, dt_bias_a, norm_a_g,
           conv_b_w, conv_b_b, a_log_b, dt_bias_b, d_skip_b, norm_b_g, norm_c_g,
           i_bias_d, f_bias_d, norm_d_g, w_out, ln1_g, ln1_b, w_up, w_down, ln2_g, ln2_b):
    f32 = jnp.float32
    a_conv, a_s, b_conv, b_s, c_s, d_c, d_n, d_m = (s.astype(f32) for s in st)
    bsz, t, _ = x.shape

    def heads(z, h):
        return z.reshape(bsz, t, h, -1)

    u = jnp.einsum('btd,de->bte', x, w_in).astype(f32)
    (a_qkv, a_z, a_b, a_a, b_z, b_xbc, b_dt, c_q, c_f, c_i, c_g,
     d_q, d_k, d_v, d_o, d_i, d_f) = _split(u, IN_SIZES)

    a_qkv, a_conv_new = _causal_conv(a_qkv, a_conv, conv_a_w, conv_a_b)
    aq, ak, av = _split(jax.nn.silu(a_qkv), (QK_A, QK_A, W_A))
    aq = _l2norm(heads(aq, H_A)) * DK_A ** -0.5
    ak = _l2norm(heads(ak, H_A))
    log_alpha = -jnp.exp(a_log_a) * jax.nn.softplus(a_a + dt_bias_a)
    o_a, a_s_new = _gated_delta(aq, ak, heads(av, H_A), log_alpha, jax.nn.sigmoid(a_b), a_s)
    y_a = _rms_heads(o_a, norm_a_g) * jax.nn.silu(a_z)

    b_xbc, b_conv_new = _causal_conv(b_xbc, b_conv, conv_b_w, conv_b_b)
    bx, bB, bC = _split(jax.nn.silu(b_xbc), (W_B, G_B * N_B, G_B * N_B))
    bx = heads(bx, H_B)
    bB = jnp.repeat(heads(bB, G_B), H_B // G_B, axis=2)
    bC = jnp.repeat(heads(bC, G_B), H_B // G_B, axis=2)
    dt = jax.nn.softplus(b_dt + dt_bias_b)
    o_b, b_s_new = _scalar_decay_attn(bC, bB, bx * dt[..., None], -jnp.exp(a_log_b) * dt, b_s)
    o_b = (o_b + d_skip_b[:, None] * bx).reshape(bsz, t, W_B) * jax.nn.silu(b_z)
    y_b = _rms_heads(o_b.reshape(bsz, t, G_B, -1), norm_b_g)

    lb = lb.reshape(H_C, DK_C)
    fx = heads(c_f, H_C)
    f_gate = lb + (1.0 - lb) * jax.nn.sigmoid(fx)
    log_f = jnp.log(f_gate)
    kc = 1.0 - f_gate
    o_c, c_s_new = _vector_decay_attn(heads(c_q, H_C), kc, heads(c_i, H_C), log_f, c_s)
    y_c = _rms_heads(o_c, norm_c_g) * jax.nn.sigmoid(c_g)

    h_d, d_c_new, d_n_new, d_m_new = _mlstm(
        heads(d_q, H_D), heads(d_k, H_D) * DK_D ** -0.5, heads(d_v, H_D),
        d_i + i_bias_d, jax.nn.log_sigmoid(d_f + f_bias_d), d_c, d_n, d_m)
    y_d = _rms_heads(h_d, norm_d_g) * jax.nn.sigmoid(d_o)

    mix = jnp.concatenate([y_a, y_b, y_c, y_d], axis=-1).astype(x.dtype) @ w_out
    h1 = _layernorm(DEEPNORM_ALPHA * x + mix, ln1_g, ln1_b)
    ff = jnp.square(jax.nn.relu(h1 @ w_up)) @ w_down
    out = _layernorm(DEEPNORM_ALPHA * h1 + ff, ln2_g, ln2_b)
    return out, (a_conv_new, a_s_new, b_conv_new, b_s_new, c_s_new, d_c_new, d_n_new, d_m_new)


def _decoder(x, states, emb_ln_g, emb_ln_b, lb_logits_c, layer_w):
    p = jax.nn.softmax(lb_logits_c.astype(jnp.float32), axis=0)
    lb_all = jnp.cumsum(p, axis=0) - p[0]
    h = _layernorm(x, emb_ln_g, emb_ln_b)
    new = []
    for l in range(DEPTH):
        h, st = _layer(h, tuple(s[l] for s in states), lb_all[l], *[w[l] for w in layer_w])
        new.append(st)
    return h, tuple(jnp.stack(z) for z in zip(*new))


def setup_inputs(seed: int = 0) -> dict:
    key = jax.random.key(seed)
    ks = iter(jax.random.split(key, 64))

    def nrm(shape, scale=1.0):
        return scale * jax.random.normal(next(ks), shape, jnp.float32)

    def gain(shape):
        return 1.0 + nrm(shape, 0.02)

    def unif(shape, lo, hi):
        return jax.random.uniform(next(ks), shape, jnp.float32, lo, hi)

    def dt_bias(shape):
        dt = jnp.exp(unif(shape, math.log(1e-3), math.log(1e-1)))
        return dt + jnp.log(-jnp.expm1(-dt))

    L = DEPTH
    return {
        'x_prompt': nrm((BATCH, SEQ, D_MODEL)),
        'x_sample': nrm((DEC_BATCH, DEC_SEQ, D_MODEL)),
        'state_a_conv': nrm((L, DEC_BATCH, CONV_K - 1, CONV_A_CH)),
        'state_a_ssm': nrm((L, DEC_BATCH, H_A, DK_A, DV_A), 0.3),
        'state_b_conv': nrm((L, DEC_BATCH, CONV_K - 1, CONV_B_CH)),
        'state_b_ssm': nrm((L, DEC_BATCH, H_B, N_B, P_B), 0.3),
        'state_c_ssm': nrm((L, DEC_BATCH, H_C, DK_C, DV_C), 0.5),
        'state_d_cmem': nrm((L, DEC_BATCH, H_D, DK_D, DV_D), 0.3),
        'state_d_nvec': nrm((L, DEC_BATCH, H_D, DK_D), 0.3),
        'state_d_mstab': nrm((L, DEC_BATCH, H_D)),
        'emb_ln_g': gain((D_MODEL,)),
        'emb_ln_b': nrm((D_MODEL,), 0.02),
        'lb_logits_c': nrm((L, QK_C), 0.5),
        'w_in': nrm((L, D_MODEL, IN_DIM), D_MODEL ** -0.5),
        'conv_a_w': nrm((L, CONV_K, CONV_A_CH), CONV_K ** -0.5),
        'conv_a_b': nrm((L, CONV_A_CH), 0.02),
        'a_log_a': jnp.log(unif((L, H_A), 1.0, 16.0)),
        'dt_bias_a': dt_bias((L, H_A)),
        'norm_a_g': gain((L, W_A)),
        'conv_b_w': nrm((L, CONV_K, CONV_B_CH), CONV_K ** -0.5),
        'conv_b_b': nrm((L, CONV_B_CH), 0.02),
        'a_log_b': jnp.log(unif((L, H_B), 1.0, 16.0)),
        'dt_bias_b': dt_bias((L, H_B)),
        'd_skip_b': gain((L, H_B)),
        'norm_b_g': gain((L, W_B)),
        'norm_c_g': gain((L, W_C)),
        'i_bias_d': nrm((L, H_D), 0.1),
        'f_bias_d': jnp.linspace(3.0, 6.0, H_D, dtype=jnp.float32) + nrm((L, H_D), 0.1),
        'norm_d_g': gain((L, W_D)),
        'w_out': nrm((L, MIX_OUT, D_MODEL), DEEPNORM_BETA * MIX_OUT ** -0.5),
        'ln1_g': gain((L, D_MODEL)),
        'ln1_b': nrm((L, D_MODEL), 0.02),
        'w_up': nrm((L, D_MODEL, D_FF), D_MODEL ** -0.5),
        'w_down': nrm((L, D_FF, D_MODEL), DEEPNORM_BETA * D_FF ** -0.5),
        'ln2_g': gain((L, D_MODEL)),
        'ln2_b': nrm((L, D_MODEL), 0.02),
    }


def reference(x_prompt, x_sample, state_a_conv, state_a_ssm, state_b_conv, state_b_ssm,
              state_c_ssm, state_d_cmem, state_d_nvec, state_d_mstab,
              emb_ln_g, emb_ln_b, lb_logits_c, w_in, conv_a_w, conv_a_b, a_log_a, dt_bias_a,
              norm_a_g, conv_b_w, conv_b_b, a_log_b, dt_bias_b, d_skip_b, norm_b_g, norm_c_g,
              i_bias_d, f_bias_d, norm_d_g, w_out, ln1_g, ln1_b, w_up, w_down, ln2_g, ln2_b):
    layer_w = (w_in, conv_a_w, conv_a_b, a_log_a, dt_bias_a, norm_a_g,
               conv_b_w, conv_b_b, a_log_b, dt_bias_b, d_skip_b, norm_b_g, norm_c_g,
               i_bias_d, f_bias_d, norm_d_g, w_out, ln1_g, ln1_b, w_up, w_down, ln2_g, ln2_b)
    sample_states = (state_a_conv, state_a_ssm, state_b_conv, state_b_ssm,
                     state_c_ssm, state_d_cmem, state_d_nvec, state_d_mstab)
    prompt_states = tuple(jnp.zeros((DEPTH, x_prompt.shape[0]) + s.shape[2:], jnp.float32)
                          for s in sample_states)
    y_prompt, (pa_conv, pa_ssm, pb_conv, pb_ssm, pc_ssm, pd_cmem, pd_nvec, pd_mstab) = _decoder(
        x_prompt, prompt_states, emb_ln_g, emb_ln_b, lb_logits_c, layer_w)
    y_sample, (sa_conv, sa_ssm, sb_conv, sb_ssm, sc_ssm, sd_cmem, sd_nvec, sd_mstab) = _decoder(
        x_sample, sample_states, emb_ln_g, emb_ln_b, lb_logits_c, layer_w)
    return (y_prompt, y_sample,
            pa_conv, pa_ssm, pb_conv, pb_ssm, pc_ssm, pd_cmem, pd_nvec, pd_mstab,
            sa_conv, sa_ssm, sb_conv, sb_ssm, sc_ssm, sd_cmem, sd_nvec, sd_mstab)
```

```python
import functools

import jax
import jax.numpy as jnp
from jax import lax
from jax.experimental import pallas as pl
from jax.experimental.pallas import tpu as pltpu

F32 = jnp.float32
BF16 = jnp.bfloat16

D_MODEL = 1024
DEPTH = 2
N_HEADS = 4
HEAD_DIM = 64
GROUP_WIDTH = N_HEADS * HEAD_DIM
G_B = 2
CONV_K = 4
CHUNK = 64
D_FF = 4 * D_MODEL
EPS = 1e-6
NEG = -1e30
ALPHA = (2 * DEPTH) ** 0.25

CONV_A_CH = 3 * GROUP_WIDTH
CONV_B_CH = GROUP_WIDTH + 2 * G_B * HEAD_DIM

OFF_A_QKV = 0
OFF_A_Z = 768
OFF_B_Z = 1024
OFF_B_XBC = 1280
OFF_C_Q = 1792
OFF_C_F = 2048
OFF_C_I = 2304
OFF_C_G = 2560
OFF_D_Q = 2816
OFF_D_K = 3072
OFF_D_V = 3328
OFF_D_O = 3584
OFF_GATES = 3840
U_COLS = 3968
LANES = 128
GL_BETA = 0
GL_DECAY_A = 4
GL_DT_B = 8
GL_I_D = 12
GL_F_D = 16

CONV_TAIL = CONV_K - 1
ROW0 = 8

VMEM_LIMIT = 56 * 1024 * 1024
TOKEN_TILE = 256


def _layernorm(x, g, b):
    mu = jnp.mean(x, axis=-1, keepdims=True)
    xc = x - mu
    var = jnp.mean(xc * xc, axis=-1, keepdims=True)
    return xc * lax.rsqrt(var + EPS) * g + b


def _silu(x):
    return x * (1.0 / (1.0 + jnp.exp(-x)))


def _sigmoid(x):
    return 1.0 / (1.0 + jnp.exp(-x))


def _mm(a, b):
    return jnp.dot(a.astype(BF16), b.astype(BF16), preferred_element_type=F32)


def _mm_hi(a, b):
    return jnp.dot(a, b, precision=lax.Precision.HIGHEST, preferred_element_type=F32)


def _nt(a, b):
    return lax.dot_general(a.astype(BF16), b.astype(BF16), (((1,), (1,)), ((), ())),
                           preferred_element_type=F32)


def _tn(a, b):
    return lax.dot_general(a.astype(BF16), b.astype(BF16), (((0,), (0,)), ((), ())),
                           preferred_element_type=F32)


def _split3(x):
    hi = x.astype(BF16)
    r1 = x - hi.astype(F32)
    mid = r1.astype(BF16)
    r2 = r1 - mid.astype(F32)
    return hi, mid, r2.astype(BF16)


def _dot01(m01, parts):
    acc = jnp.dot(m01, parts[0], preferred_element_type=F32)
    for p in parts[1:]:
        acc = acc + jnp.dot(m01, p, preferred_element_type=F32)
    return acc


def _rms(x):
    return x * lax.rsqrt(jnp.mean(x * x, axis=-1, keepdims=True) + EPS)


def _levels(L):
    out, b = [], 1
    while 2 * b <= L:
        out.append(b)
        b *= 2
    return out


def _inproj_kernel(x_ref, w_ref, g_ref, b_ref, *out_refs, apply_ln):
    x = x_ref[...]
    if apply_ln:
        x = _layernorm(x, g_ref[...], b_ref[...])
        out_refs[0][...] = x
    out_refs[-1][...] = jnp.dot(x.astype(BF16), w_ref[...], preferred_element_type=F32)


def _inproj(x, w_in_p, g, b, apply_ln):
    n = x.shape[0]
    tm = min(TOKEN_TILE, n)
    grid = (n // tm,)
    const = lambda i: (0, 0)
    out_shape = [jax.ShapeDtypeStruct((n, U_COLS), F32)]
    out_specs = [pl.BlockSpec((tm, U_COLS), lambda i: (i, 0))]
    if apply_ln:
        out_shape = [jax.ShapeDtypeStruct((n, D_MODEL), F32)] + out_shape
        out_specs = [pl.BlockSpec((tm, D_MODEL), lambda i: (i, 0))] + out_specs
    return pl.pallas_call(
        functools.partial(_inproj_kernel, apply_ln=apply_ln),
        grid=grid,
        in_specs=[pl.BlockSpec((tm, D_MODEL), lambda i: (i, 0)),
                  pl.BlockSpec((D_MODEL, U_COLS), const, pipeline_mode=pl.Buffered(1)),
                  pl.BlockSpec((1, D_MODEL), const),
                  pl.BlockSpec((1, D_MODEL), const)],
        out_specs=out_specs,
        out_shape=out_shape,
        compiler_params=pltpu.CompilerParams(dimension_semantics=("arbitrary",),
                                             vmem_limit_bytes=VMEM_LIMIT),
        name="inproj_ln" if apply_ln else "inproj",
    )(x, w_in_p, g, b)


def _outffn_kernel(mix_ref, x_ref, wo_ref, g1_ref, b1_ref, wu_ref, wd_ref, g2_ref, b2_ref, o_ref):
    m = jnp.dot(mix_ref[...].astype(BF16), wo_ref[...], preferred_element_type=F32)
    h1 = _layernorm(ALPHA * x_ref[...] + m, g1_ref[...], b1_ref[...])
    up = jnp.dot(h1.astype(BF16), wu_ref[...], preferred_element_type=F32)
    act = jnp.square(jnp.maximum(up, 0.0))
    ff = jnp.dot(act.astype(BF16), wd_ref[...], preferred_element_type=F32)
    o_ref[...] = _layernorm(ALPHA * h1 + ff, g2_ref[...], b2_ref[...])


def _outffn(mix, x, wo, g1, b1, wu, wd, g2, b2):
    n = x.shape[0]
    tm = min(TOKEN_TILE, n)
    const = lambda i: (0, 0)
    tile = pl.BlockSpec((tm, D_MODEL), lambda i: (i, 0))
    row = pl.BlockSpec((1, D_MODEL), const)
    return pl.pallas_call(
        _outffn_kernel,
        grid=(n // tm,),
        in_specs=[tile, tile,
                  pl.BlockSpec((D_MODEL, D_MODEL), const, pipeline_mode=pl.Buffered(1)),
                  row, row,
                  pl.BlockSpec((D_MODEL, D_FF), const, pipeline_mode=pl.Buffered(1)),
                  pl.BlockSpec((D_FF, D_MODEL), const, pipeline_mode=pl.Buffered(1)),
                  row, row],
        out_specs=tile,
        out_shape=jax.ShapeDtypeStruct((n, D_MODEL), F32),
        compiler_params=pltpu.CompilerParams(dimension_semantics=("arbitrary",),
                                             vmem_limit_bytes=VMEM_LIMIT),
        name="outffn",
    )(mix, x, wo, g1, b1, wu, wd, g2, b2)


def _mixer_kernel(u_ref, ca_in, sa_in, cb_in, sb_in, sc_in, dc_in, dn_in, dm_in,
                  caw_ref, cab_ref, cbw_ref, cbb_ref, gbias_ref, galog_ref,
                  nag_ref, nbg_ref, ncg_ref, ndg_ref, dskip_ref, lblog_ref,
                  y_ref, ca_out, sa_out, cb_out, sb_out, sc_out, dc_out, dn_out, dm_out,
                  ub, sa, sb, sc, dc, dn, dm, obuf,
                  *, L, T, NC, layer):
    c = pl.program_id(1)
    H, HD = N_HEADS, HEAD_DIM
    R0, R1 = ROW0, ROW0 + L
    padded = T < L

    @pl.when(c == 0)
    def _():
        ub[R0 - CONV_TAIL:R0, OFF_A_QKV:OFF_A_QKV + CONV_A_CH] = ca_in[0]
        ub[R0 - CONV_TAIL:R0, OFF_B_XBC:OFF_B_XBC + CONV_B_CH] = cb_in[0]
        sa[...] = sa_in[0]
        sb[...] = sb_in[0]
        sc[...] = sc_in[0]
        dc[...] = dc_in[0]
        dn[...] = dn_in[0]
        dm[...] = dm_in[0]

    ub[R0:R0 + T, :] = u_ref[0]
    if padded:
        ub[R0 + T:R1, :] = jnp.zeros((L - T, U_COLS), F32)

    def seg(off, width):
        return ub[R0:R1, off:off + width]

    def conv(off, width, w_ref, b_ref):
        acc = b_ref[...] + ub[R0:R1, off:off + width] * w_ref[CONV_K - 1:CONV_K, :]
        for j in range(CONV_K - 1):
            lo = R0 - CONV_TAIL + j
            acc = acc + ub[lo:lo + L, off:off + width] * w_ref[j:j + 1, :]
        return acc

    ri = lax.broadcasted_iota(jnp.int32, (L, L), 0)
    ci = lax.broadcasted_iota(jnp.int32, (L, L), 1)
    incl = ri >= ci
    strict = ri > ci
    tril01 = jnp.where(incl, 1.0, 0.0).astype(BF16)
    levels = _levels(L)

    def level_mask(b):
        sh = b.bit_length() - 1
        rb = ri >> sh
        cb = ci >> sh
        return (rb - cb == 1) & ((cb & 1) == 0)

    lmask = {b: level_mask(b) for b in levels}
    rowv = None
    if padded:
        rowv = lax.broadcasted_iota(jnp.int32, (L, 1), 0) < T

    lane = lax.broadcasted_iota(jnp.int32, (1, LANES), 1)
    pre = seg(OFF_GATES, LANES) + gbias_ref[...]
    e = jnp.exp(-jnp.abs(pre))
    l1p = jnp.log1p(e)
    softplus = jnp.maximum(pre, 0.0) + l1p
    logsig = jnp.minimum(pre, 0.0) - l1p
    sig = _sigmoid(pre)
    is_dt = (lane >= GL_DECAY_A) & (lane < GL_I_D)
    is_i = (lane >= GL_I_D) & (lane < GL_F_D)
    is_f = (lane >= GL_F_D) & (lane < GL_F_D + H)
    logdec = jnp.where(is_dt, -jnp.exp(galog_ref[...]) * softplus, jnp.where(is_f, logsig, 0.0))
    gval = jnp.where(lane < GL_DECAY_A, sig, jnp.where(is_i, pre, softplus))
    if padded:
        logdec = jnp.where(rowv, logdec, 0.0)
        gval = jnp.where(rowv, gval, jnp.where(is_i, NEG, 0.0))
    gcum = _dot01(tril01, _split3(logdec))
    gt = jnp.where(is_i, gval, gcum).T

    def col(x, j):
        return x[:, j:j + 1]

    def decay_mat(j):
        return jnp.where(incl, jnp.exp(jnp.minimum(col(gcum, j) - gt[j:j + 1, :], 0.0)), 0.0)

    def put_y(off, val):
        y_ref[0, :, off:off + val.shape[1]] = val[:T] if padded else val

    qkv = _silu(conv(OFF_A_QKV, CONV_A_CH, caw_ref, cab_ref))
    a_z = seg(OFF_A_Z, GROUP_WIDTH)
    eye = jnp.where(ri == ci, 1.0, 0.0)
    for h in range(H):
        q = qkv[:, h * HD:(h + 1) * HD]
        k = qkv[:, GROUP_WIDTH + h * HD:GROUP_WIDTH + (h + 1) * HD]
        v = qkv[:, 2 * GROUP_WIDTH + h * HD:2 * GROUP_WIDTH + (h + 1) * HD]
        q = q * lax.rsqrt(jnp.sum(q * q, axis=-1, keepdims=True) + EPS) * (HD ** -0.5)
        k = k * lax.rsqrt(jnp.sum(k * k, axis=-1, keepdims=True) + EPS)
        g = col(gcum, GL_DECAY_A + h)
        beta = col(gval, GL_BETA + h)
        dmat = decay_mat(GL_DECAY_A + h)
        eg = jnp.exp(g)
        nmat = jnp.where(strict, beta * _nt(k, k) * dmat, 0.0)
        tinv = eye
        for b in levels:
            off = jnp.where(lmask[b], nmat, 0.0)
            tinv = tinv - _mm_hi(tinv, _mm_hi(off, tinv))
        sol_v = _mm(tinv, beta * v)
        sol_k = _mm(tinv, (beta * eg) * k)
        s = sa[h]
        unew = sol_v - _mm(sol_k, s)
        o = eg * _mm(q, s) + _mm(_nt(q, k) * dmat, unew)
        g_last = g[L - 1:L, :]
        sa[h] = jnp.exp(g_last) * s + _tn(k * jnp.exp(g_last - g), unew)
        y = _rms(o) * nag_ref[:, h * HD:(h + 1) * HD] * _silu(a_z[:, h * HD:(h + 1) * HD])
        put_y(h * HD, y)

    xbc = _silu(conv(OFF_B_XBC, CONV_B_CH, cbw_ref, cbb_ref))
    for grp in range(G_B):
        bmat = xbc[:, GROUP_WIDTH + grp * HD:GROUP_WIDTH + (grp + 1) * HD]
        cmat = xbc[:, GROUP_WIDTH + G_B * HD + grp * HD:GROUP_WIDTH + G_B * HD + (grp + 1) * HD]
        cb = _nt(cmat, bmat)
        for h in range(grp * (H // G_B), (grp + 1) * (H // G_B)):
            g = col(gcum, GL_DT_B + h)
            dt = col(gval, GL_DT_B + h)
            xh = xbc[:, h * HD:(h + 1) * HD]
            v = xh * dt
            s = sb[h]
            o = jnp.exp(g) * _mm(cmat, s) + _mm(cb * decay_mat(GL_DT_B + h), v)
            g_last = g[L - 1:L, :]
            sb[h] = jnp.exp(g_last) * s + _tn(bmat * jnp.exp(g_last - g), v)
            obuf[:, h * HD:(h + 1) * HD] = o + dskip_ref[:, h * HD:(h + 1) * HD] * xh
    ob = obuf[...] * _silu(seg(OFF_B_Z, GROUP_WIDTH))
    gw = GROUP_WIDTH // G_B
    for grp in range(G_B):
        put_y(GROUP_WIDTH + grp * gw, _rms(ob[:, grp * gw:(grp + 1) * gw]) * nbg_ref[:, grp * gw:(grp + 1) * gw])

    pl_sm = lblog_ref[...]
    pl_sm = jnp.exp(pl_sm - jnp.max(pl_sm, axis=0, keepdims=True))
    pl_sm = pl_sm / jnp.sum(pl_sm, axis=0, keepdims=True)
    lb = pl_sm[0:1, :]
    for i in range(1, layer + 1):
        lb = lb + pl_sm[i:i + 1, :]
    lb = lb - pl_sm[0:1, :]
    fgate = lb + (1.0 - lb) * _sigmoid(seg(OFF_C_F, GROUP_WIDTH))
    if padded:
        fgate = jnp.where(rowv, fgate, 1.0)
    gc_parts = _split3(_dot01(tril01, _split3(jnp.log(fgate))))
    gc = gc_parts[0].astype(F32) + gc_parts[1].astype(F32) + gc_parts[2].astype(F32)
    kc = 1.0 - fgate
    c_q = seg(OFF_C_Q, GROUP_WIDTH)
    c_v = seg(OFF_C_I, GROUP_WIDTH)
    pref = {}
    for b in levels:
        sh = b.bit_length()
        sel = jnp.where(ci == (((ri >> sh) << sh) + (b - 1)), 1.0, 0.0).astype(BF16)
        pref[b] = _dot01(sel, gc_parts)
    g_last_t = jnp.broadcast_to(gc[L - 1:L, :], (8, GROUP_WIDTH)).T
    c_g = seg(OFF_C_G, GROUP_WIDTH)
    for h in range(H):
        sl = slice(h * HD, (h + 1) * HD)
        q, k, v, g = c_q[:, sl], kc[:, sl], c_v[:, sl], gc[:, sl]
        att = None
        for b in levels:
            p = pref[b][:, sl]
            part = jnp.where(lmask[b],
                             _nt(q * jnp.exp(jnp.minimum(g - p, 0.0)), k * jnp.exp(jnp.minimum(p - g, 0.0))),
                             0.0)
            att = part if att is None else att + part
        s = sc[h]
        o = _mm(q * jnp.exp(g), s) + _mm(att, v) + jnp.sum(q * k, axis=-1, keepdims=True) * v
        g_last = g[L - 1:L, :]
        sc[h] = jnp.exp(g_last_t[h * HD:(h + 1) * HD, 0:1]) * s + _tn(k * jnp.exp(g_last - g), v)
        put_y(2 * GROUP_WIDTH + h * HD, _rms(o) * ncg_ref[:, sl] * _sigmoid(c_g[:, sl]))

    d_q = seg(OFF_D_Q, GROUP_WIDTH)
    d_k = seg(OFF_D_K, GROUP_WIDTH)
    d_v = seg(OFF_D_V, GROUP_WIDTH)
    d_o = seg(OFF_D_O, GROUP_WIDTH)
    m_row = dm[...]
    m_new_row = m_row
    for h in range(H):
        sl = slice(h * HD, (h + 1) * HD)
        q, k, v = d_q[:, sl], d_k[:, sl] * (HD ** -0.5), d_v[:, sl]
        bcum = col(gcum, GL_F_D + h)
        m0 = m_row[:, h:h + 1]
        dmat = jnp.where(incl, bcum - gt[GL_F_D + h:GL_F_D + h + 1, :] + gt[GL_I_D + h:GL_I_D + h + 1, :], NEG)
        init = bcum + m0
        m_r = jnp.maximum(init, jnp.max(dmat, axis=-1, keepdims=True))
        p = jnp.where(incl, jnp.exp(dmat - m_r), 0.0) * _nt(q, k)
        s_init = jnp.exp(init - m_r)
        cmem = dc[h]
        nvec = dn[h:h + 1, :]
        num = s_init * _mm(q, cmem) + _mm(p, v)
        den = s_init * jnp.sum(q * nvec, axis=-1, keepdims=True) + jnp.sum(p, axis=-1, keepdims=True)
        hh = num / jnp.maximum(jnp.abs(den), jnp.exp(-m_r))
        m_last = m_r[L - 1:L, :]
        b_last = bcum[L - 1:L, :]
        scale = jnp.exp(b_last + m0 - m_last)
        kw = k * jnp.exp(b_last - bcum + col(gval, GL_I_D + h) - m_last)
        dc[h] = scale * cmem + _tn(kw, v)
        dn[h:h + 1, :] = scale * nvec + jnp.sum(kw, axis=0, keepdims=True)
        m_new_row = jnp.where(lane == h, m_last, m_new_row)
        put_y(3 * GROUP_WIDTH + h * HD, _rms(hh) * ndg_ref[:, sl] * _sigmoid(d_o[:, sl]))
    dm[...] = m_new_row

    tail_a = ub[R0 + T - CONV_TAIL:R0 + T, OFF_A_QKV:OFF_A_QKV + CONV_A_CH]
    tail_b = ub[R0 + T - CONV_TAIL:R0 + T, OFF_B_XBC:OFF_B_XBC + CONV_B_CH]
    ub[R0 - CONV_TAIL:R0, OFF_A_QKV:OFF_A_QKV + CONV_A_CH] = tail_a
    ub[R0 - CONV_TAIL:R0, OFF_B_XBC:OFF_B_XBC + CONV_B_CH] = tail_b

    @pl.when(c == NC - 1)
    def _():
        ca_out[0] = tail_a
        cb_out[0] = tail_b
        sa_out[0] = sa[...]
        sb_out[0] = sb[...]
        sc_out[0] = sc[...]
        dc_out[0] = dc[...]
        dn_out[0] = dn[...]
        dm_out[0] = dm[...]


def _mixers(u, states, params, *, layer, T, L):
    bsz, ttot, _ = u.shape
    nc = ttot // T
    H, HD = N_HEADS, HEAD_DIM
    ca, s_a, cb, s_b, s_c, d_c, d_n, d_m = states
    d_m = jnp.pad(d_m, ((0, 0), (0, LANES - H))).reshape(bsz, 1, LANES)

    def per_seq(shape):
        nd = len(shape)
        return pl.BlockSpec((1,) + tuple(shape), lambda b, c: (b,) + (0,) * nd)

    def whole(x):
        nd = x.ndim
        return pl.BlockSpec(x.shape, lambda b, c: (0,) * nd)

    state_specs = [per_seq((CONV_TAIL, CONV_A_CH)), per_seq((H, HD, HD)),
                   per_seq((CONV_TAIL, CONV_B_CH)), per_seq((H, HD, HD)), per_seq((H, HD, HD)),
                   per_seq((H, HD, HD)), per_seq((H, HD)), per_seq((1, LANES))]
    state_shapes = [jax.ShapeDtypeStruct((bsz, CONV_TAIL, CONV_A_CH), F32),
                    jax.ShapeDtypeStruct((bsz, H, HD, HD), F32),
                    jax.ShapeDtypeStruct((bsz, CONV_TAIL, CONV_B_CH), F32),
                    jax.ShapeDtypeStruct((bsz, H, HD, HD), F32),
                    jax.ShapeDtypeStruct((bsz, H, HD, HD), F32),
                    jax.ShapeDtypeStruct((bsz, H, HD, HD), F32),
                    jax.ShapeDtypeStruct((bsz, H, HD), F32),
                    jax.ShapeDtypeStruct((bsz, 1, LANES), F32)]
    outs = pl.pallas_call(
        functools.partial(_mixer_kernel, L=L, T=T, NC=nc, layer=layer),
        grid=(bsz, nc),
        in_specs=[pl.BlockSpec((1, T, U_COLS), lambda b, c: (b, c, 0))] + state_specs
                 + [whole(p) for p in params],
        out_specs=[pl.BlockSpec((1, T, D_MODEL), lambda b, c: (b, c, 0))] + state_specs,
        out_shape=[jax.ShapeDtypeStruct((bsz, ttot, D_MODEL), F32)] + state_shapes,
        scratch_shapes=[pltpu.VMEM((ROW0 + L, U_COLS), F32),
                        pltpu.VMEM((H, HD, HD), F32), pltpu.VMEM((H, HD, HD), F32),
                        pltpu.VMEM((H, HD, HD), F32), pltpu.VMEM((H, HD, HD), F32),
                        pltpu.VMEM((H, HD), F32), pltpu.VMEM((1, LANES), F32),
                        pltpu.VMEM((L, GROUP_WIDTH), F32)],
        compiler_params=pltpu.CompilerParams(dimension_semantics=("arbitrary", "arbitrary"),
                                             vmem_limit_bytes=VMEM_LIMIT),
        name="mixers_l%d_t%d" % (layer, T),
    )(u, ca, s_a, cb, s_b, s_c, d_c, d_n, d_m, *params)
    y, new = outs[0], list(outs[1:])
    new[7] = new[7].reshape(bsz, LANES)[:, :H]
    return y, tuple(new)


def _permute_w_in(w):
    pad = jnp.zeros((w.shape[0], U_COLS - OFF_GATES - 20), w.dtype)
    return jnp.concatenate([w[:, 0:1024], w[:, 1032:1800], w[:, 1804:3852],
                            w[:, 1024:1032], w[:, 1800:1804], w[:, 3852:3860], pad], axis=1).astype(BF16)


def _gate_row(parts):
    row = jnp.zeros((LANES,), F32)
    for off, val in parts:
        row = lax.dynamic_update_slice(row, val.astype(F32), (off,))
    return row.reshape(1, LANES)


def _decoder(x, states, seq_t, chunk, emb_g, emb_b, lb_logits, layer_w):
    bsz, t, _ = x.shape
    n = bsz * t
    h = x.reshape(n, D_MODEL)
    new_states = []
    for l in range(DEPTH):
        (w_in, conv_a_w, conv_a_b, a_log_a, dt_bias_a, norm_a_g, conv_b_w, conv_b_b, a_log_b, dt_bias_b,
         d_skip_b, norm_b_g, norm_c_g, i_bias_d, f_bias_d, norm_d_g, w_out, ln1_g, ln1_b, w_up, w_down,
         ln2_g, ln2_b) = [w[l] for w in layer_w]
        w_in_p = _permute_w_in(w_in)
        if l == 0:
            h, u = _inproj(h, w_in_p, emb_g.reshape(1, -1), emb_b.reshape(1, -1), True)
        else:
            (u,) = _inproj(h, w_in_p, emb_g.reshape(1, -1), emb_b.reshape(1, -1), False)
        params = (conv_a_w, conv_a_b.reshape(1, -1), conv_b_w, conv_b_b.reshape(1, -1),
                  _gate_row([(GL_DECAY_A, dt_bias_a), (GL_DT_B, dt_bias_b), (GL_I_D, i_bias_d), (GL_F_D, f_bias_d)]),
                  _gate_row([(GL_DECAY_A, a_log_a), (GL_DT_B, a_log_b)]),
                  norm_a_g.reshape(1, -1), norm_b_g.reshape(1, -1), norm_c_g.reshape(1, -1),
                  norm_d_g.reshape(1, -1), jnp.repeat(d_skip_b, HEAD_DIM).reshape(1, -1), lb_logits)
        y, st = _mixers(u.reshape(bsz, t, U_COLS), tuple(s[l] for s in states), params,
                        layer=l, T=seq_t, L=chunk)
        new_states.append(st)
        h = _outffn(y.reshape(n, D_MODEL), h, w_out.astype(BF16), ln1_g.reshape(1, -1), ln1_b.reshape(1, -1),
                    w_up.astype(BF16), w_down.astype(BF16), ln2_g.reshape(1, -1), ln2_b.reshape(1, -1))
    return h.reshape(bsz, t, D_MODEL), tuple(jnp.stack(z) for z in zip(*new_states))


def kernel(x_prompt, x_sample, state_a_conv, state_a_ssm, state_b_conv, state_b_ssm, state_c_ssm, state_d_cmem, state_d_nvec, state_d_mstab, emb_ln_g, emb_ln_b, lb_logits_c, w_in, conv_a_w, conv_a_b, a_log_a, dt_bias_a, norm_a_g, conv_b_w, conv_b_b, a_log_b, dt_bias_b, d_skip_b, norm_b_g, norm_c_g, i_bias_d, f_bias_d, norm_d_g, w_out, ln1_g, ln1_b, w_up, w_down, ln2_g, ln2_b):
    layer_w = (w_in, conv_a_w, conv_a_b, a_log_a, dt_bias_a, norm_a_g, conv_b_w, conv_b_b, a_log_b, dt_bias_b,
               d_skip_b, norm_b_g, norm_c_g, i_bias_d, f_bias_d, norm_d_g, w_out, ln1_g, ln1_b, w_up, w_down,
               ln2_g, ln2_b)
    sample_states = (state_a_conv, state_a_ssm, state_b_conv, state_b_ssm,
                     state_c_ssm, state_d_cmem, state_d_nvec, state_d_mstab)
    prompt_states = tuple(jnp.zeros((DEPTH, x_prompt.shape[0]) + s.shape[2:], F32) for s in sample_states)
    y_p, ps = _decoder(x_prompt, prompt_states, CHUNK, CHUNK, emb_ln_g, emb_ln_b, lb_logits_c, layer_w)
    t_s = x_sample.shape[1]
    y_s, ss = _decoder(x_sample, sample_states, t_s, 8, emb_ln_g, emb_ln_b, lb_logits_c, layer_w)
    return (y_p, y_s) + ps + ss
```

```python
import functools

import jax
import jax.numpy as jnp
from jax import lax
from jax.experimental import pallas as pl
from jax.experimental.pallas import tpu as pltpu

F32 = jnp.float32
BF16 = jnp.bfloat16

D_MODEL = 1024
DEPTH = 2
N_HEADS = 4
HEAD_DIM = 64
GROUP_WIDTH = N_HEADS * HEAD_DIM
G_B = 2
CONV_K = 4
CHUNK = 64
D_FF = 4 * D_MODEL
EPS = 1e-6
NEG = -1e30
ALPHA = (2 * DEPTH) ** 0.25

CONV_A_CH = 3 * GROUP_WIDTH
CONV_B_CH = GROUP_WIDTH + 2 * G_B * HEAD_DIM

OFF_A_QKV = 0
OFF_A_Z = 768
OFF_B_Z = 1024
OFF_B_XBC = 1280
OFF_C_Q = 1792
OFF_C_F = 2048
OFF_C_I = 2304
OFF_C_G = 2560
OFF_D_Q = 2816
OFF_D_K = 3072
OFF_D_V = 3328
OFF_D_O = 3584
OFF_GATES = 3840
U_COLS = 3968
LANES = 128
GL_BETA = 0
GL_DECAY_A = 4
GL_DT_B = 8
GL_I_D = 12
GL_F_D = 16

CONV_TAIL = CONV_K - 1
ROW0 = 8

VMEM_LIMIT = 56 * 1024 * 1024
TOKEN_TILE = 256
SEQS_PER_STEP_PROMPT = 2
SEQS_PER_STEP_SAMPLE = 4
SAMPLE_ROWS = 8


def _layernorm(x, g, b):
    mu = jnp.mean(x, axis=-1, keepdims=True)
    xc = x - mu
    var = jnp.mean(xc * xc, axis=-1, keepdims=True)
    return xc * lax.rsqrt(var + EPS) * g + b


def _sigmoid(x):
    return 1.0 / (1.0 + jnp.exp(-x))


def _silu(x):
    return x * _sigmoid(x)


def _dot(a, b):
    return jnp.dot(a, b, preferred_element_type=F32)


def _mm(a, b):
    return _dot(a.astype(BF16), b.astype(BF16))


def _nt(a, b):
    return lax.dot_general(a.astype(BF16), b.astype(BF16), (((1,), (1,)), ((), ())),
                           preferred_element_type=F32)


def _tn(a, b):
    return lax.dot_general(a.astype(BF16), b.astype(BF16), (((0,), (0,)), ((), ())),
                           preferred_element_type=F32)


def _split2(x):
    hi = x.astype(BF16)
    return hi, (x - hi.astype(F32)).astype(BF16)


def _split3(x):
    hi = x.astype(BF16)
    r1 = x - hi.astype(F32)
    mid = r1.astype(BF16)
    r2 = r1 - mid.astype(F32)
    return hi, mid, r2.astype(BF16)


def _dot01(m01, parts):
    acc = _dot(m01, parts[0])
    for p in parts[1:]:
        acc = acc + _dot(m01, p)
    return acc


def _rms(x):
    return x * lax.rsqrt(jnp.mean(x * x, axis=-1, keepdims=True) + EPS)


def _levels(L):
    out, b = [], 1
    while 2 * b <= L:
        out.append(b)
        b *= 2
    return out


def _run_interleaved(chains):
    chains = list(chains)
    while chains:
        alive = []
        for ch in chains:
            try:
                next(ch)
                alive.append(ch)
            except StopIteration:
                pass
        chains = alive


def _inproj_kernel(x_ref, w_ref, g_ref, b_ref, *out_refs, apply_ln):
    x = x_ref[...]
    if apply_ln:
        x = _layernorm(x, g_ref[...], b_ref[...])
        out_refs[0][...] = x
    out_refs[-1][...] = _dot(x.astype(BF16), w_ref[...])


def _inproj(x, w_in_p, g, b, apply_ln):
    n = x.shape[0]
    tm = min(TOKEN_TILE, n)
    grid = (n // tm,)
    const = lambda i: (0, 0)
    out_shape = [jax.ShapeDtypeStruct((n, U_COLS), F32)]
    out_specs = [pl.BlockSpec((tm, U_COLS), lambda i: (i, 0))]
    if apply_ln:
        out_shape = [jax.ShapeDtypeStruct((n, D_MODEL), F32)] + out_shape
        out_specs = [pl.BlockSpec((tm, D_MODEL), lambda i: (i, 0))] + out_specs
    return pl.pallas_call(
        functools.partial(_inproj_kernel, apply_ln=apply_ln),
        grid=grid,
        in_specs=[pl.BlockSpec((tm, D_MODEL), lambda i: (i, 0)),
                  pl.BlockSpec((D_MODEL, U_COLS), const, pipeline_mode=pl.Buffered(1)),
                  pl.BlockSpec((1, D_MODEL), const),
                  pl.BlockSpec((1, D_MODEL), const)],
        out_specs=out_specs,
        out_shape=out_shape,
        compiler_params=pltpu.CompilerParams(dimension_semantics=("arbitrary",),
                                             vmem_limit_bytes=VMEM_LIMIT),
        name="inproj_ln" if apply_ln else "inproj",
    )(x, w_in_p, g, b)


def _outffn_kernel(mix_ref, x_ref, wo_ref, g1_ref, b1_ref, wu_ref, wd_ref, g2_ref, b2_ref, o_ref):
    m = _dot(mix_ref[...].astype(BF16), wo_ref[...])
    h1 = _layernorm(ALPHA * x_ref[...] + m, g1_ref[...], b1_ref[...])
    up = _dot(h1.astype(BF16), wu_ref[...])
    act = jnp.square(jnp.maximum(up, 0.0))
    ff = _dot(act.astype(BF16), wd_ref[...])
    o_ref[...] = _layernorm(ALPHA * h1 + ff, g2_ref[...], b2_ref[...])


def _outffn(mix, x, wo, g1, b1, wu, wd, g2, b2):
    n = x.shape[0]
    tm = min(TOKEN_TILE, n)
    const = lambda i: (0, 0)
    tile = pl.BlockSpec((tm, D_MODEL), lambda i: (i, 0))
    row = pl.BlockSpec((1, D_MODEL), const)
    return pl.pallas_call(
        _outffn_kernel,
        grid=(n // tm,),
        in_specs=[tile, tile,
                  pl.BlockSpec((D_MODEL, D_MODEL), const, pipeline_mode=pl.Buffered(1)),
                  row, row,
                  pl.BlockSpec((D_MODEL, D_FF), const, pipeline_mode=pl.Buffered(1)),
                  pl.BlockSpec((D_FF, D_MODEL), const, pipeline_mode=pl.Buffered(1)),
                  row, row],
        out_specs=tile,
        out_shape=jax.ShapeDtypeStruct((n, D_MODEL), F32),
        compiler_params=pltpu.CompilerParams(dimension_semantics=("arbitrary",),
                                             vmem_limit_bytes=VMEM_LIMIT),
        name="outffn",
    )(mix, x, wo, g1, b1, wu, wd, g2, b2)


def _mixer_kernel(u_ref, ca_in, sa_in, cb_in, sb_in, sc_in, dc_in, dn_in, dm_in,
                  caw_ref, cab_ref, cbw_ref, cbb_ref, gbias_ref, galog_ref,
                  nag_ref, nbg_ref, ncg_ref, ndg_ref, dskip_ref, lblog_ref,
                  y_ref, ca_out, sa_out, cb_out, sb_out, sc_out, dc_out, dn_out, dm_out,
                  ub, sa, sb, sc, dc, dn, dm,
                  *, L, T, NC, P, layer):
    c = pl.program_id(1)
    H, HD = N_HEADS, HEAD_DIM
    R0, R1 = ROW0, ROW0 + L
    padded = T < L
    a_cols = slice(OFF_A_QKV, OFF_A_QKV + CONV_A_CH)
    b_cols = slice(OFF_B_XBC, OFF_B_XBC + CONV_B_CH)

    @pl.when(c == 0)
    def _():
        ub[:, R0 - CONV_TAIL:R0, a_cols] = ca_in[...]
        ub[:, R0 - CONV_TAIL:R0, b_cols] = cb_in[...]
        sa[...] = sa_in[...]
        sb[...] = sb_in[...]
        sc[...] = sc_in[...]
        dc[...] = dc_in[...]
        dn[...] = dn_in[...]
        dm[...] = dm_in[...]

    ub[:, R0:R0 + T, :] = u_ref[...]
    if padded:
        ub[:, R0 + T:R1, :] = jnp.zeros((P, L - T, U_COLS), F32)

    ri = lax.broadcasted_iota(jnp.int32, (L, L), 0)
    ci = lax.broadcasted_iota(jnp.int32, (L, L), 1)
    incl = ri >= ci
    strict = ri > ci
    tril01 = jnp.where(incl, 1.0, 0.0).astype(BF16)
    eye = jnp.where(ri == ci, 1.0, 0.0)
    levels = _levels(L)

    def level_mask(b):
        sh = b.bit_length() - 1
        rb = ri >> sh
        cb = ci >> sh
        return (rb - cb == 1) & ((cb & 1) == 0)

    lmask = {b: level_mask(b) for b in levels}
    sel01 = {}
    for b in levels:
        sh = b.bit_length()
        sel01[b] = jnp.where(ci == (((ri >> sh) << sh) + (b - 1)), 1.0, 0.0).astype(BF16)
    rowv = lax.broadcasted_iota(jnp.int32, (L, 1), 0) < T
    lane = lax.broadcasted_iota(jnp.int32, (1, LANES), 1)
    is_dt = (lane >= GL_DECAY_A) & (lane < GL_I_D)
    is_i = (lane >= GL_I_D) & (lane < GL_F_D)
    is_f = (lane >= GL_F_D) & (lane < GL_F_D + H)

    pl_sm = lblog_ref[...]
    pl_sm = jnp.exp(pl_sm - jnp.max(pl_sm, axis=0, keepdims=True))
    pl_sm = pl_sm / jnp.sum(pl_sm, axis=0, keepdims=True)
    lb = pl_sm[0:1, :]
    for i in range(1, layer + 1):
        lb = lb + pl_sm[i:i + 1, :]
    lb = lb - pl_sm[0:1, :]

    m_last_out = {}

    def seq_chains(s):
        def seg(off, width):
            return ub[s, R0:R1, off:off + width]

        def conv(off, width, w_ref, b_ref):
            acc = b_ref[...] + ub[s, R0:R1, off:off + width] * w_ref[CONV_K - 1:CONV_K, :]
            for j in range(CONV_K - 1):
                lo = R0 - CONV_TAIL + j
                acc = acc + ub[s, lo:lo + L, off:off + width] * w_ref[j:j + 1, :]
            return acc

        def put_y(off, val):
            y_ref[s, :, off:off + val.shape[1]] = val[:T] if padded else val

        pre = seg(OFF_GATES, LANES) + gbias_ref[...]
        e = jnp.exp(-jnp.abs(pre))
        l1p = jnp.log1p(e)
        softplus = jnp.maximum(pre, 0.0) + l1p
        logsig = jnp.minimum(pre, 0.0) - l1p
        logdec = jnp.where(is_dt, -jnp.exp(galog_ref[...]) * softplus, jnp.where(is_f, logsig, 0.0))
        gval = jnp.where(lane < GL_DECAY_A, _sigmoid(pre), jnp.where(is_i, pre, softplus))
        if padded:
            logdec = jnp.where(rowv, logdec, 0.0)
            gval = jnp.where(rowv, gval, jnp.where(is_i, NEG, 0.0))
        gcum = _dot01(tril01, _split3(logdec))
        gt = jnp.where(is_i, gval, gcum).T

        def col(x, j):
            return x[:, j:j + 1]

        def decay_mat(j):
            return jnp.where(incl, jnp.exp(jnp.minimum(col(gcum, j) - gt[j:j + 1, :], 0.0)), 0.0)

        qkv = _silu(conv(OFF_A_QKV, CONV_A_CH, caw_ref, cab_ref))
        a_z = seg(OFF_A_Z, GROUP_WIDTH)

        def chain_a(h):
            sl = slice(h * HD, (h + 1) * HD)
            q = qkv[:, sl]
            k = qkv[:, GROUP_WIDTH + h * HD:GROUP_WIDTH + (h + 1) * HD]
            v = qkv[:, 2 * GROUP_WIDTH + h * HD:2 * GROUP_WIDTH + (h + 1) * HD]
            q = q * lax.rsqrt(jnp.sum(q * q, axis=-1, keepdims=True) + EPS) * (HD ** -0.5)
            k = k * lax.rsqrt(jnp.sum(k * k, axis=-1, keepdims=True) + EPS)
            g = col(gcum, GL_DECAY_A + h)
            beta = col(gval, GL_BETA + h)
            dmat = decay_mat(GL_DECAY_A + h)
            eg = jnp.exp(g)
            kb = k.astype(BF16)
            kk = _nt(kb, kb)
            qk = _nt(q, kb)
            yield
            nmat = jnp.where(strict, beta * kk * dmat, 0.0)
            tinv = eye - jnp.where(lmask[1], nmat, 0.0)
            for b in levels[1:]:
                t_hi = tinv.astype(BF16)
                o_hi, o_lo = _split2(jnp.where(lmask[b], nmat, 0.0))
                x = _dot(o_hi, t_hi) + _dot(o_lo, t_hi)
                yield
                x_hi, x_lo = _split2(x)
                tinv = tinv - (_dot(t_hi, x_hi) + _dot(t_hi, x_lo))
                yield
            r_hi, r_lo = _split2(jnp.concatenate([beta * v, (beta * eg) * k], axis=-1))
            t_hi = tinv.astype(BF16)
            sol = _dot(t_hi, r_hi) + _dot(t_hi, r_lo)
            s0 = sa[s, h]
            qs = _mm(q, s0)
            yield
            unew = sol[:, :HD] - _mm(sol[:, HD:], s0)
            yield
            o = eg * qs + _mm(qk * dmat, unew)
            g_last = g[L - 1:L, :]
            sa[s, h] = jnp.exp(g_last) * s0 + _tn(k * jnp.exp(g_last - g), unew)
            yield
            put_y(h * HD, _rms(o) * nag_ref[:, sl] * _silu(a_z[:, sl]))

        xbc = _silu(conv(OFF_B_XBC, CONV_B_CH, cbw_ref, cbb_ref))
        b_z = seg(OFF_B_Z, GROUP_WIDTH)

        def chain_b(grp):
            bmat = xbc[:, GROUP_WIDTH + grp * HD:GROUP_WIDTH + (grp + 1) * HD]
            cmat = xbc[:, GROUP_WIDTH + G_B * HD + grp * HD:GROUP_WIDTH + G_B * HD + (grp + 1) * HD]
            cb = _nt(cmat, bmat)
            yield
            outs = []
            for h in range(grp * (H // G_B), (grp + 1) * (H // G_B)):
                sl = slice(h * HD, (h + 1) * HD)
                g = col(gcum, GL_DT_B + h)
                dt = col(gval, GL_DT_B + h)
                xh = xbc[:, sl]
                v = xh * dt
                s0 = sb[s, h]
                o = jnp.exp(g) * _mm(cmat, s0) + _mm(cb * decay_mat(GL_DT_B + h), v)
                g_last = g[L - 1:L, :]
                sb[s, h] = jnp.exp(g_last) * s0 + _tn(bmat * jnp.exp(g_last - g), v)
                yield
                outs.append(o + dskip_ref[:, sl] * xh)
            gw = GROUP_WIDTH // G_B
            gsl = slice(grp * gw, (grp + 1) * gw)
            ob = jnp.concatenate(outs, axis=-1) * _silu(b_z[:, gsl])
            put_y(GROUP_WIDTH + grp * gw, _rms(ob) * nbg_ref[:, gsl])

        fgate = lb + (1.0 - lb) * _sigmoid(seg(OFF_C_F, GROUP_WIDTH))
        if padded:
            fgate = jnp.where(rowv, fgate, 1.0)
        gc_parts = _split3(_dot01(tril01, _split3(jnp.log(fgate))))
        gc = gc_parts[0].astype(F32) + gc_parts[1].astype(F32) + gc_parts[2].astype(F32)
        kc = 1.0 - fgate
        c_q = seg(OFF_C_Q, GROUP_WIDTH)
        c_v = seg(OFF_C_I, GROUP_WIDTH)
        c_g = seg(OFF_C_G, GROUP_WIDTH)
        pref = {b: _dot01(sel01[b], gc_parts) for b in levels}
        g_last_t = jnp.broadcast_to(gc[L - 1:L, :], (8, GROUP_WIDTH)).T

        def chain_c(h):
            sl = slice(h * HD, (h + 1) * HD)
            q, k, v, g = c_q[:, sl], kc[:, sl], c_v[:, sl], gc[:, sl]
            att = None
            for b in levels:
                p = pref[b][:, sl]
                part = jnp.where(lmask[b],
                                 _nt(q * jnp.exp(jnp.minimum(g - p, 0.0)), k * jnp.exp(jnp.minimum(p - g, 0.0))),
                                 0.0)
                att = part if att is None else att + part
                yield
            s0 = sc[s, h]
            o = _mm(q * jnp.exp(g), s0) + _mm(att, v) + jnp.sum(q * k, axis=-1, keepdims=True) * v
            g_last = g[L - 1:L, :]
            sc[s, h] = jnp.exp(g_last_t[sl, 0:1]) * s0 + _tn(k * jnp.exp(g_last - g), v)
            yield
            put_y(2 * GROUP_WIDTH + h * HD, _rms(o) * ncg_ref[:, sl] * _sigmoid(c_g[:, sl]))

        d_q = seg(OFF_D_Q, GROUP_WIDTH)
        d_k = seg(OFF_D_K, GROUP_WIDTH)
        d_v = seg(OFF_D_V, GROUP_WIDTH)
        d_o = seg(OFF_D_O, GROUP_WIDTH)
        m_row = dm[s]

        def chain_d(h):
            sl = slice(h * HD, (h + 1) * HD)
            q, k, v = d_q[:, sl], d_k[:, sl] * (HD ** -0.5), d_v[:, sl]
            bcum = col(gcum, GL_F_D + h)
            m0 = m_row[:, h:h + 1]
            dmat = jnp.where(incl, bcum - gt[GL_F_D + h:GL_F_D + h + 1, :] + gt[GL_I_D + h:GL_I_D + h + 1, :], NEG)
            init = bcum + m0
            m_r = jnp.maximum(init, jnp.max(dmat, axis=-1, keepdims=True))
            qk = _nt(q, k)
            cmem = dc[s, h]
            qc = _mm(q, cmem)
            yield
            p = jnp.where(incl, jnp.exp(dmat - m_r), 0.0) * qk
            s_init = jnp.exp(init - m_r)
            nvec = dn[s, h:h + 1, :]
            num = s_init * qc + _mm(p, v)
            den = s_init * jnp.sum(q * nvec, axis=-1, keepdims=True) + jnp.sum(p, axis=-1, keepdims=True)
            m_last = m_r[L - 1:L, :]
            b_last = bcum[L - 1:L, :]
            scale = jnp.exp(b_last + m0 - m_last)
            kw = k * jnp.exp(b_last - bcum + col(gval, GL_I_D + h) - m_last)
            dc[s, h] = scale * cmem + _tn(kw, v)
            dn[s, h:h + 1, :] = scale * nvec + jnp.sum(kw, axis=0, keepdims=True)
            m_last_out[(s, h)] = m_last
            yield
            hh = num / jnp.maximum(jnp.abs(den), jnp.exp(-m_r))
            put_y(3 * GROUP_WIDTH + h * HD, _rms(hh) * ndg_ref[:, sl] * _sigmoid(d_o[:, sl]))

        return ([chain_a(h) for h in range(H)] + [chain_b(g) for g in range(G_B)]
                + [chain_c(h) for h in range(H)] + [chain_d(h) for h in range(H)])

    chains = []
    for s in range(P):
        chains += seq_chains(s)
    _run_interleaved(chains)

    for s in range(P):
        m_new = dm[s]
        for h in range(H):
            m_new = jnp.where(lane == h, m_last_out[(s, h)], m_new)
        dm[s] = m_new

    tail_a = ub[:, R0 + T - CONV_TAIL:R0 + T, a_cols]
    tail_b = ub[:, R0 + T - CONV_TAIL:R0 + T, b_cols]
    ub[:, R0 - CONV_TAIL:R0, a_cols] = tail_a
    ub[:, R0 - CONV_TAIL:R0, b_cols] = tail_b

    @pl.when(c == NC - 1)
    def _():
        ca_out[...] = tail_a
        cb_out[...] = tail_b
        sa_out[...] = sa[...]
        sb_out[...] = sb[...]
        sc_out[...] = sc[...]
        dc_out[...] = dc[...]
        dn_out[...] = dn[...]
        dm_out[...] = dm[...]


def _mixers(u, states, params, *, layer, T, L, P):
    bsz, ttot, _ = u.shape
    nc = ttot // T
    H, HD = N_HEADS, HEAD_DIM
    ca, s_a, cb, s_b, s_c, d_c, d_n, d_m = states
    d_m = jnp.pad(d_m, ((0, 0), (0, LANES - H))).reshape(bsz, 1, LANES)

    def per_seq(shape):
        nd = len(shape)
        return pl.BlockSpec((P,) + tuple(shape), lambda b, c: (b,) + (0,) * nd)

    def whole(x):
        nd = x.ndim
        return pl.BlockSpec(x.shape, lambda b, c: (0,) * nd)

    state_dims = [(CONV_TAIL, CONV_A_CH), (H, HD, HD), (CONV_TAIL, CONV_B_CH), (H, HD, HD), (H, HD, HD),
                  (H, HD, HD), (H, HD), (1, LANES)]
    state_specs = [per_seq(d) for d in state_dims]
    state_shapes = [jax.ShapeDtypeStruct((bsz,) + d, F32) for d in state_dims]
    outs = pl.pallas_call(
        functools.partial(_mixer_kernel, L=L, T=T, NC=nc, P=P, layer=layer),
        grid=(bsz // P, nc),
        in_specs=[pl.BlockSpec((P, T, U_COLS), lambda b, c: (b, c, 0))] + state_specs
                 + [whole(p) for p in params],
        out_specs=[pl.BlockSpec((P, T, D_MODEL), lambda b, c: (b, c, 0))] + state_specs,
        out_shape=[jax.ShapeDtypeStruct((bsz, ttot, D_MODEL), F32)] + state_shapes,
        scratch_shapes=[pltpu.VMEM((P, ROW0 + L, U_COLS), F32)]
                       + [pltpu.VMEM((P,) + d, F32) for d in state_dims[1:2] + state_dims[3:]],
        compiler_params=pltpu.CompilerParams(dimension_semantics=("arbitrary", "arbitrary"),
                                             vmem_limit_bytes=VMEM_LIMIT),
        name="mixers_l%d_t%d" % (layer, T),
    )(u, ca, s_a, cb, s_b, s_c, d_c, d_n, d_m, *params)
    y, new = outs[0], list(outs[1:])
    new[7] = new[7].reshape(bsz, LANES)[:, :H]
    return y, tuple(new)


def _permute_w_in(w):
    pad = jnp.zeros((w.shape[0], U_COLS - OFF_GATES - 20), w.dtype)
    return jnp.concatenate([w[:, 0:1024], w[:, 1032:1800], w[:, 1804:3852],
                            w[:, 1024:1032], w[:, 1800:1804], w[:, 3852:3860], pad], axis=1).astype(BF16)


def _gate_row(parts):
    row = jnp.zeros((LANES,), F32)
    for off, val in parts:
        row = lax.dynamic_update_slice(row, val.astype(F32), (off,))
    return row.reshape(1, LANES)


def _decoder(x, states, seq_t, chunk, seqs_per_step, emb_g, emb_b, lb_logits, layer_w):
    bsz, t, _ = x.shape
    n = bsz * t
    h = x.reshape(n, D_MODEL)
    new_states = []
    for l in range(DEPTH):
        (w_in, conv_a_w, conv_a_b, a_log_a, dt_bias_a, norm_a_g, conv_b_w, conv_b_b, a_log_b, dt_bias_b,
         d_skip_b, norm_b_g, norm_c_g, i_bias_d, f_bias_d, norm_d_g, w_out, ln1_g, ln1_b, w_up, w_down,
         ln2_g, ln2_b) = [w[l] for w in layer_w]
        w_in_p = _permute_w_in(w_in)
        if l == 0:
            h, u = _inproj(h, w_in_p, emb_g.reshape(1, -1), emb_b.reshape(1, -1), True)
        else:
            (u,) = _inproj(h, w_in_p, emb_g.reshape(1, -1), emb_b.reshape(1, -1), False)
        params = (conv_a_w, conv_a_b.reshape(1, -1), conv_b_w, conv_b_b.reshape(1, -1),
                  _gate_row([(GL_DECAY_A, dt_bias_a), (GL_DT_B, dt_bias_b), (GL_I_D, i_bias_d), (GL_F_D, f_bias_d)]),
                  _gate_row([(GL_DECAY_A, a_log_a), (GL_DT_B, a_log_b)]),
                  norm_a_g.reshape(1, -1), norm_b_g.reshape(1, -1), norm_c_g.reshape(1, -1),
                  norm_d_g.reshape(1, -1), jnp.repeat(d_skip_b, HEAD_DIM).reshape(1, -1), lb_logits)
        y, st = _mixers(u.reshape(bsz, t, U_COLS), tuple(s[l] for s in states), params,
                        layer=l, T=seq_t, L=chunk, P=seqs_per_step)
        new_states.append(st)
        h = _outffn(y.reshape(n, D_MODEL), h, w_out.astype(BF16), ln1_g.reshape(1, -1), ln1_b.reshape(1, -1),
                    w_up.astype(BF16), w_down.astype(BF16), ln2_g.reshape(1, -1), ln2_b.reshape(1, -1))
    return h.reshape(bsz, t, D_MODEL), tuple(jnp.stack(z) for z in zip(*new_states))


def kernel(x_prompt, x_sample, state_a_conv, state_a_ssm, state_b_conv, state_b_ssm, state_c_ssm, state_d_cmem, state_d_nvec, state_d_mstab, emb_ln_g, emb_ln_b, lb_logits_c, w_in, conv_a_w, conv_a_b, a_log_a, dt_bias_a, norm_a_g, conv_b_w, conv_b_b, a_log_b, dt_bias_b, d_skip_b, norm_b_g, norm_c_g, i_bias_d, f_bias_d, norm_d_g, w_out, ln1_g, ln1_b, w_up, w_down, ln2_g, ln2_b):
    layer_w = (w_in, conv_a_w, conv_a_b, a_log_a, dt_bias_a, norm_a_g, conv_b_w, conv_b_b, a_log_b, dt_bias_b,
               d_skip_b, norm_b_g, norm_c_g, i_bias_d, f_bias_d, norm_d_g, w_out, ln1_g, ln1_b, w_up, w_down,
               ln2_g, ln2_b)
    sample_states = (state_a_conv, state_a_ssm, state_b_conv, state_b_ssm,
                     state_c_ssm, state_d_cmem, state_d_nvec, state_d_mstab)
    prompt_states = tuple(jnp.zeros((DEPTH, x_prompt.shape[0]) + s.shape[2:], F32) for s in sample_states)
    y_p, ps = _decoder(x_prompt, prompt_states, CHUNK, CHUNK, SEQS_PER_STEP_PROMPT,
                       emb_ln_g, emb_ln_b, lb_logits_c, layer_w)
    y_s, ss = _decoder(x_sample, sample_states, x_sample.shape[1], SAMPLE_ROWS, SEQS_PER_STEP_SAMPLE,
                       emb_ln_g, emb_ln_b, lb_logits_c, layer_w)
    return (y_p, y_s) + ps + ss
```

```python
import functools

import jax
import jax.numpy as jnp
from jax import lax
from jax.experimental import pallas as pl
from jax.experimental.pallas import tpu as pltpu

F32 = jnp.float32
BF16 = jnp.bfloat16

D_MODEL = 1024
DEPTH = 2
N_HEADS = 4
HEAD_DIM = 64
GROUP_WIDTH = N_HEADS * HEAD_DIM
G_B = 2
CONV_K = 4
CHUNK = 64
D_FF = 4 * D_MODEL
EPS = 1e-6
NEG = -1e30
ALPHA = (2 * DEPTH) ** 0.25

CONV_A_CH = 3 * GROUP_WIDTH
CONV_B_CH = GROUP_WIDTH + 2 * G_B * HEAD_DIM

OFF_A_QKV = 0
OFF_A_Z = 768
OFF_B_Z = 1024
OFF_B_XBC = 1280
OFF_C_Q = 1792
OFF_C_F = 2048
OFF_C_I = 2304
OFF_C_G = 2560
OFF_D_Q = 2816
OFF_D_K = 3072
OFF_D_V = 3328
OFF_D_O = 3584
OFF_GATES = 3840
U_COLS = 3968
LANES = 128
GL_BETA = 0
GL_DECAY_A = 4
GL_DT_B = 8
GL_I_D = 12
GL_F_D = 16

CONV_TAIL = CONV_K - 1
ROW0 = 8

VMEM_LIMIT = 56 * 1024 * 1024
TOKEN_TILE = 256
SEQS_PER_STEP_PROMPT = 2
SEQS_PER_STEP_SAMPLE = 4
SAMPLE_ROWS = 8


def _layernorm(x, g, b):
    mu = jnp.mean(x, axis=-1, keepdims=True)
    xc = x - mu
    var = jnp.mean(xc * xc, axis=-1, keepdims=True)
    return xc * lax.rsqrt(var + EPS) * g + b


def _sigmoid(x):
    return 1.0 / (1.0 + jnp.exp(-x))


def _silu(x):
    return x * _sigmoid(x)


def _dot(a, b):
    return jnp.dot(a, b, preferred_element_type=F32)


def _mm(a, b):
    return _dot(a.astype(BF16), b.astype(BF16))


def _nt(a, b):
    return lax.dot_general(a.astype(BF16), b.astype(BF16), (((1,), (1,)), ((), ())),
                           preferred_element_type=F32)


def _tn(a, b):
    return lax.dot_general(a.astype(BF16), b.astype(BF16), (((0,), (0,)), ((), ())),
                           preferred_element_type=F32)


def _split2(x):
    hi = x.astype(BF16)
    return hi, (x - hi.astype(F32)).astype(BF16)


def _split3(x):
    hi = x.astype(BF16)
    r1 = x - hi.astype(F32)
    mid = r1.astype(BF16)
    r2 = r1 - mid.astype(F32)
    return hi, mid, r2.astype(BF16)


def _dot01(m01, parts):
    acc = _dot(m01, parts[0])
    for p in parts[1:]:
        acc = acc + _dot(m01, p)
    return acc


def _rms(x):
    return x * lax.rsqrt(jnp.mean(x * x, axis=-1, keepdims=True) + EPS)


def _levels(L):
    out, b = [], 1
    while 2 * b <= L:
        out.append(b)
        b *= 2
    return out


def _run_interleaved(chains):
    chains = list(chains)
    while chains:
        alive = []
        for ch in chains:
            try:
                next(ch)
                alive.append(ch)
            except StopIteration:
                pass
        chains = alive


def _inproj_kernel(x_ref, w_ref, g_ref, b_ref, *out_refs, apply_ln):
    x = x_ref[...]
    if apply_ln:
        x = _layernorm(x, g_ref[...], b_ref[...])
        out_refs[0][...] = x
    out_refs[-1][...] = _dot(x.astype(BF16), w_ref[...])


def _inproj(x, w_in_p, layer, g, b, apply_ln):
    n = x.shape[0]
    tm = min(TOKEN_TILE, n)
    grid = (n // tm,)
    const = lambda i: (0, 0)
    out_shape = [jax.ShapeDtypeStruct((n, U_COLS), F32)]
    out_specs = [pl.BlockSpec((tm, U_COLS), lambda i: (i, 0))]
    if apply_ln:
        out_shape = [jax.ShapeDtypeStruct((n, D_MODEL), F32)] + out_shape
        out_specs = [pl.BlockSpec((tm, D_MODEL), lambda i: (i, 0))] + out_specs
    return pl.pallas_call(
        functools.partial(_inproj_kernel, apply_ln=apply_ln),
        grid=grid,
        in_specs=[pl.BlockSpec((tm, D_MODEL), lambda i: (i, 0)),
                  pl.BlockSpec((None, D_MODEL, U_COLS), lambda i: (layer, 0, 0), pipeline_mode=pl.Buffered(1)),
                  pl.BlockSpec((1, D_MODEL), const),
                  pl.BlockSpec((1, D_MODEL), const)],
        out_specs=out_specs,
        out_shape=out_shape,
        compiler_params=pltpu.CompilerParams(dimension_semantics=("arbitrary",),
                                             vmem_limit_bytes=VMEM_LIMIT),
        name="inproj_ln" if apply_ln else "inproj",
    )(x, w_in_p, g, b)


def _outffn_kernel(mix_ref, x_ref, wo_ref, g1_ref, b1_ref, wu_ref, wd_ref, g2_ref, b2_ref, o_ref):
    m = _dot(mix_ref[...].astype(BF16), wo_ref[...])
    h1 = _layernorm(ALPHA * x_ref[...] + m, g1_ref[...], b1_ref[...])
    up = _dot(h1.astype(BF16), wu_ref[...])
    act = jnp.square(jnp.maximum(up, 0.0))
    ff = _dot(act.astype(BF16), wd_ref[...])
    o_ref[...] = _layernorm(ALPHA * h1 + ff, g2_ref[...], b2_ref[...])


def _outffn(mix, x, wo, g1, b1, wu, wd, g2, b2):
    n = x.shape[0]
    tm = min(TOKEN_TILE, n)
    const = lambda i: (0, 0)
    tile = pl.BlockSpec((tm, D_MODEL), lambda i: (i, 0))
    row = pl.BlockSpec((1, D_MODEL), const)
    return pl.pallas_call(
        _outffn_kernel,
        grid=(n // tm,),
        in_specs=[tile, tile,
                  pl.BlockSpec((D_MODEL, D_MODEL), const, pipeline_mode=pl.Buffered(1)),
                  row, row,
                  pl.BlockSpec((D_MODEL, D_FF), const, pipeline_mode=pl.Buffered(1)),
                  pl.BlockSpec((D_FF, D_MODEL), const, pipeline_mode=pl.Buffered(1)),
                  row, row],
        out_specs=tile,
        out_shape=jax.ShapeDtypeStruct((n, D_MODEL), F32),
        compiler_params=pltpu.CompilerParams(dimension_semantics=("arbitrary",),
                                             vmem_limit_bytes=VMEM_LIMIT),
        name="outffn",
    )(mix, x, wo, g1, b1, wu, wd, g2, b2)


def _mixer_kernel(u_ref, ca_in, sa_in, cb_in, sb_in, sc_in, dc_in, dn_in, dm_in,
                  caw_ref, cab_ref, cbw_ref, cbb_ref, gbias_ref, galog_ref,
                  nag_ref, nbg_ref, ncg_ref, ndg_ref, dskip_ref, lblog_ref,
                  y_ref, ca_out, sa_out, cb_out, sb_out, sc_out, dc_out, dn_out, dm_out,
                  ub, sa, sb, sc, dc, dn, dm,
                  *, L, T, NC, P, layer):
    c = pl.program_id(1)
    H, HD = N_HEADS, HEAD_DIM
    R0, R1 = ROW0, ROW0 + L
    padded = T < L
    a_cols = slice(OFF_A_QKV, OFF_A_QKV + CONV_A_CH)
    b_cols = slice(OFF_B_XBC, OFF_B_XBC + CONV_B_CH)

    @pl.when(c == 0)
    def _():
        ub[:, R0 - CONV_TAIL:R0, a_cols] = ca_in[...]
        ub[:, R0 - CONV_TAIL:R0, b_cols] = cb_in[...]
        sa[...] = sa_in[...]
        sb[...] = sb_in[...]
        sc[...] = sc_in[...]
        dc[...] = dc_in[...]
        dn[...] = dn_in[...]
        dm[...] = dm_in[...]

    ub[:, R0:R0 + T, :] = u_ref[...]
    if padded:
        ub[:, R0 + T:R1, :] = jnp.zeros((P, L - T, U_COLS), F32)

    ri = lax.broadcasted_iota(jnp.int32, (L, L), 0)
    ci = lax.broadcasted_iota(jnp.int32, (L, L), 1)
    incl = ri >= ci
    strict = ri > ci
    tril01 = jnp.where(incl, 1.0, 0.0).astype(BF16)
    eye = jnp.where(ri == ci, 1.0, 0.0)
    levels = _levels(L)

    def level_mask(b):
        sh = b.bit_length() - 1
        rb = ri >> sh
        cb = ci >> sh
        return (rb - cb == 1) & ((cb & 1) == 0)

    lmask = {b: level_mask(b) for b in levels}
    sel01 = {}
    for b in levels:
        sh = b.bit_length()
        sel01[b] = jnp.where(ci == (((ri >> sh) << sh) + (b - 1)), 1.0, 0.0).astype(BF16)
    rowv = lax.broadcasted_iota(jnp.int32, (L, 1), 0) < T
    lane = lax.broadcasted_iota(jnp.int32, (1, LANES), 1)
    is_dt = (lane >= GL_DECAY_A) & (lane < GL_I_D)
    is_i = (lane >= GL_I_D) & (lane < GL_F_D)
    is_f = (lane >= GL_F_D) & (lane < GL_F_D + H)

    pl_sm = lblog_ref[...]
    pl_sm = jnp.exp(pl_sm - jnp.max(pl_sm, axis=0, keepdims=True))
    pl_sm = pl_sm / jnp.sum(pl_sm, axis=0, keepdims=True)
    lb = pl_sm[0:1, :]
    for i in range(1, layer + 1):
        lb = lb + pl_sm[i:i + 1, :]
    lb = lb - pl_sm[0:1, :]

    m_last_out = {}

    def seq_chains(s):
        def seg(off, width):
            return ub[s, R0:R1, off:off + width]

        def conv(off, width, w_ref, b_ref):
            acc = b_ref[...] + ub[s, R0:R1, off:off + width] * w_ref[CONV_K - 1:CONV_K, :]
            for j in range(CONV_K - 1):
                lo = R0 - CONV_TAIL + j
                acc = acc + ub[s, lo:lo + L, off:off + width] * w_ref[j:j + 1, :]
            return acc

        def put_y(off, val):
            y_ref[s, :, off:off + val.shape[1]] = val[:T] if padded else val

        pre = seg(OFF_GATES, LANES) + gbias_ref[...]
        e = jnp.exp(-jnp.abs(pre))
        l1p = jnp.log1p(e)
        softplus = jnp.maximum(pre, 0.0) + l1p
        logsig = jnp.minimum(pre, 0.0) - l1p
        logdec = jnp.where(is_dt, -jnp.exp(galog_ref[...]) * softplus, jnp.where(is_f, logsig, 0.0))
        gval = jnp.where(lane < GL_DECAY_A, _sigmoid(pre), jnp.where(is_i, pre, softplus))
        if padded:
            logdec = jnp.where(rowv, logdec, 0.0)
            gval = jnp.where(rowv, gval, jnp.where(is_i, NEG, 0.0))
        gcum = _dot01(tril01, _split3(logdec))
        gt = jnp.where(is_i, gval, gcum).T

        def col(x, j):
            return x[:, j:j + 1]

        def decay_mat(j):
            return jnp.where(incl, jnp.exp(jnp.minimum(col(gcum, j) - gt[j:j + 1, :], 0.0)), 0.0)

        qkv = _silu(conv(OFF_A_QKV, CONV_A_CH, caw_ref, cab_ref))
        a_z = seg(OFF_A_Z, GROUP_WIDTH)

        def chain_a(h):
            sl = slice(h * HD, (h + 1) * HD)
            q = qkv[:, sl]
            k = qkv[:, GROUP_WIDTH + h * HD:GROUP_WIDTH + (h + 1) * HD]
            v = qkv[:, 2 * GROUP_WIDTH + h * HD:2 * GROUP_WIDTH + (h + 1) * HD]
            q = q * lax.rsqrt(jnp.sum(q * q, axis=-1, keepdims=True) + EPS) * (HD ** -0.5)
            k = k * lax.rsqrt(jnp.sum(k * k, axis=-1, keepdims=True) + EPS)
            g = col(gcum, GL_DECAY_A + h)
            beta = col(gval, GL_BETA + h)
            dmat = decay_mat(GL_DECAY_A + h)
            eg = jnp.exp(g)
            kb = k.astype(BF16)
            kk = _nt(kb, kb)
            qk = _nt(q, kb)
            yield
            nmat = jnp.where(strict, beta * kk * dmat, 0.0)
            tinv = eye - jnp.where(lmask[1], nmat, 0.0)
            for b in levels[1:]:
                t_hi = tinv.astype(BF16)
                o_hi, o_lo = _split2(jnp.where(lmask[b], nmat, 0.0))
                x = _dot(o_hi, t_hi) + _dot(o_lo, t_hi)
                yield
                x_hi, x_lo = _split2(x)
                tinv = tinv - (_dot(t_hi, x_hi) + _dot(t_hi, x_lo))
                yield
            r_hi, r_lo = _split2(jnp.concatenate([beta * v, (beta * eg) * k], axis=-1))
            t_hi = tinv.astype(BF16)
            sol = _dot(t_hi, r_hi) + _dot(t_hi, r_lo)
            s0 = sa[s, h]
            qs = _mm(q, s0)
            yield
            unew = sol[:, :HD] - _mm(sol[:, HD:], s0)
            yield
            o = eg * qs + _mm(qk * dmat, unew)
            g_last = g[L - 1:L, :]
            sa[s, h] = jnp.exp(g_last) * s0 + _tn(k * jnp.exp(g_last - g), unew)
            yield
            put_y(h * HD, _rms(o) * nag_ref[:, sl] * _silu(a_z[:, sl]))

        xbc = _silu(conv(OFF_B_XBC, CONV_B_CH, cbw_ref, cbb_ref))
        b_z = seg(OFF_B_Z, GROUP_WIDTH)

        def chain_b(grp):
            bmat = xbc[:, GROUP_WIDTH + grp * HD:GROUP_WIDTH + (grp + 1) * HD]
            cmat = xbc[:, GROUP_WIDTH + G_B * HD + grp * HD:GROUP_WIDTH + G_B * HD + (grp + 1) * HD]
            cb = _nt(cmat, bmat)
            yield
            outs = []
            for h in range(grp * (H // G_B), (grp + 1) * (H // G_B)):
                sl = slice(h * HD, (h + 1) * HD)
                g = col(gcum, GL_DT_B + h)
                dt = col(gval, GL_DT_B + h)
                xh = xbc[:, sl]
                v = xh * dt
                s0 = sb[s, h]
                o = jnp.exp(g) * _mm(cmat, s0) + _mm(cb * decay_mat(GL_DT_B + h), v)
                g_last = g[L - 1:L, :]
                sb[s, h] = jnp.exp(g_last) * s0 + _tn(bmat * jnp.exp(g_last - g), v)
                yield
                outs.append(o + dskip_ref[:, sl] * xh)
            gw = GROUP_WIDTH // G_B
            gsl = slice(grp * gw, (grp + 1) * gw)
            ob = jnp.concatenate(outs, axis=-1) * _silu(b_z[:, gsl])
            put_y(GROUP_WIDTH + grp * gw, _rms(ob) * nbg_ref[:, gsl])

        fgate = lb + (1.0 - lb) * _sigmoid(seg(OFF_C_F, GROUP_WIDTH))
        if padded:
            fgate = jnp.where(rowv, fgate, 1.0)
        gc_parts = _split3(_dot01(tril01, _split3(jnp.log(fgate))))
        gc = gc_parts[0].astype(F32) + gc_parts[1].astype(F32) + gc_parts[2].astype(F32)
        kc = 1.0 - fgate
        c_q = seg(OFF_C_Q, GROUP_WIDTH)
        c_v = seg(OFF_C_I, GROUP_WIDTH)
        c_g = seg(OFF_C_G, GROUP_WIDTH)
        pref = {b: _dot01(sel01[b], gc_parts) for b in levels}
        g_last_t = jnp.broadcast_to(gc[L - 1:L, :], (8, GROUP_WIDTH)).T

        def chain_c(h):
            sl = slice(h * HD, (h + 1) * HD)
            q, k, v, g = c_q[:, sl], kc[:, sl], c_v[:, sl], gc[:, sl]
            att = None
            for b in levels:
                p = pref[b][:, sl]
                part = jnp.where(lmask[b],
                                 _nt(q * jnp.exp(jnp.minimum(g - p, 0.0)), k * jnp.exp(jnp.minimum(p - g, 0.0))),
                                 0.0)
                att = part if att is None else att + part
                yield
            s0 = sc[s, h]
            o = _mm(q * jnp.exp(g), s0) + _mm(att, v) + jnp.sum(q * k, axis=-1, keepdims=True) * v
            g_last = g[L - 1:L, :]
            sc[s, h] = jnp.exp(g_last_t[sl, 0:1]) * s0 + _tn(k * jnp.exp(g_last - g), v)
            yield
            put_y(2 * GROUP_WIDTH + h * HD, _rms(o) * ncg_ref[:, sl] * _sigmoid(c_g[:, sl]))

        d_q = seg(OFF_D_Q, GROUP_WIDTH)
        d_k = seg(OFF_D_K, GROUP_WIDTH)
        d_v = seg(OFF_D_V, GROUP_WIDTH)
        d_o = seg(OFF_D_O, GROUP_WIDTH)
        m_row = dm[s]

        def chain_d(h):
            sl = slice(h * HD, (h + 1) * HD)
            q, k, v = d_q[:, sl], d_k[:, sl] * (HD ** -0.5), d_v[:, sl]
            bcum = col(gcum, GL_F_D + h)
            m0 = m_row[:, h:h + 1]
            dmat = jnp.where(incl, bcum - gt[GL_F_D + h:GL_F_D + h + 1, :] + gt[GL_I_D + h:GL_I_D + h + 1, :], NEG)
            init = bcum + m0
            m_r = jnp.maximum(init, jnp.max(dmat, axis=-1, keepdims=True))
            qk = _nt(q, k)
            cmem = dc[s, h]
            qc = _mm(q, cmem)
            yield
            p = jnp.where(incl, jnp.exp(dmat - m_r), 0.0) * qk
            s_init = jnp.exp(init - m_r)
            nvec = dn[s, h:h + 1, :]
            num = s_init * qc + _mm(p, v)
            den = s_init * jnp.sum(q * nvec, axis=-1, keepdims=True) + jnp.sum(p, axis=-1, keepdims=True)
            m_last = m_r[L - 1:L, :]
            b_last = bcum[L - 1:L, :]
            scale = jnp.exp(b_last + m0 - m_last)
            kw = k * jnp.exp(b_last - bcum + col(gval, GL_I_D + h) - m_last)
            dc[s, h] = scale * cmem + _tn(kw, v)
            dn[s, h:h + 1, :] = scale * nvec + jnp.sum(kw, axis=0, keepdims=True)
            m_last_out[(s, h)] = m_last
            yield
            hh = num / jnp.maximum(jnp.abs(den), jnp.exp(-m_r))
            put_y(3 * GROUP_WIDTH + h * HD, _rms(hh) * ndg_ref[:, sl] * _sigmoid(d_o[:, sl]))

        return ([chain_a(h) for h in range(H)] + [chain_b(g) for g in range(G_B)]
                + [chain_c(h) for h in range(H)] + [chain_d(h) for h in range(H)])

    chains = []
    for s in range(P):
        chains += seq_chains(s)
    _run_interleaved(chains)

    for s in range(P):
        m_new = dm[s]
        for h in range(H):
            m_new = jnp.where(lane == h, m_last_out[(s, h)], m_new)
        dm[s] = m_new

    tail_a = ub[:, R0 + T - CONV_TAIL:R0 + T, a_cols]
    tail_b = ub[:, R0 + T - CONV_TAIL:R0 + T, b_cols]
    ub[:, R0 - CONV_TAIL:R0, a_cols] = tail_a
    ub[:, R0 - CONV_TAIL:R0, b_cols] = tail_b

    @pl.when(c == NC - 1)
    def _():
        ca_out[...] = tail_a
        cb_out[...] = tail_b
        sa_out[...] = sa[...]
        sb_out[...] = sb[...]
        sc_out[...] = sc[...]
        dc_out[...] = dc[...]
        dn_out[...] = dn[...]
        dm_out[...] = dm[...]


def _mixers(u, states, params, *, layer, T, L, P):
    bsz, ttot, _ = u.shape
    nc = ttot // T
    H, HD = N_HEADS, HEAD_DIM
    ca, s_a, cb, s_b, s_c, d_c, d_n, d_m = states
    d_m = jnp.pad(d_m, ((0, 0), (0, 0), (0, LANES - H))).reshape(d_m.shape[0], bsz, 1, LANES)

    def per_seq(shape):
        nd = len(shape)
        return pl.BlockSpec((P,) + tuple(shape), lambda b, c: (b,) + (0,) * nd)

    def per_seq_in(x, shape):
        nd = len(shape)
        lidx = layer if x.shape[0] > 1 else 0
        return pl.BlockSpec((None, P) + tuple(shape), lambda b, c: (lidx, b) + (0,) * nd)

    def whole(x):
        nd = x.ndim
        return pl.BlockSpec(x.shape, lambda b, c: (0,) * nd)

    state_dims = [(CONV_TAIL, CONV_A_CH), (H, HD, HD), (CONV_TAIL, CONV_B_CH), (H, HD, HD), (H, HD, HD),
                  (H, HD, HD), (H, HD), (1, LANES)]
    state_specs = [per_seq(d) for d in state_dims]
    state_shapes = [jax.ShapeDtypeStruct((bsz,) + d, F32) for d in state_dims]
    outs = pl.pallas_call(
        functools.partial(_mixer_kernel, L=L, T=T, NC=nc, P=P, layer=layer),
        grid=(bsz // P, nc),
        in_specs=[pl.BlockSpec((P, T, U_COLS), lambda b, c: (b, c, 0))]
                 + [per_seq_in(x, d) for x, d in zip((ca, s_a, cb, s_b, s_c, d_c, d_n, d_m), state_dims)]
                 + [whole(p) for p in params],
        out_specs=[pl.BlockSpec((P, T, D_MODEL), lambda b, c: (b, c, 0))] + state_specs,
        out_shape=[jax.ShapeDtypeStruct((bsz, ttot, D_MODEL), F32)] + state_shapes,
        scratch_shapes=[pltpu.VMEM((P, ROW0 + L, U_COLS), F32)]
                       + [pltpu.VMEM((P,) + d, F32) for d in state_dims[1:2] + state_dims[3:]],
        compiler_params=pltpu.CompilerParams(dimension_semantics=("arbitrary", "arbitrary"),
                                             vmem_limit_bytes=VMEM_LIMIT),
        name="mixers_l%d_t%d" % (layer, T),
    )(u, ca, s_a, cb, s_b, s_c, d_c, d_n, d_m, *params)
    y, new = outs[0], list(outs[1:])
    new[7] = new[7].reshape(bsz, LANES)[:, :H]
    return y, tuple(new)


def _prep_w_in_kernel(w_ref, o_ref):
    w = w_ref[...]
    o_ref[:, 0:1024] = w[:, 0:1024].astype(BF16)
    o_ref[:, 1024:1792] = w[:, 1032:1800].astype(BF16)
    o_ref[:, 1792:OFF_GATES] = w[:, 1804:3852].astype(BF16)
    gates = jnp.concatenate([w[:, 1024:1032], w[:, 1800:1804], w[:, 3852:3860],
                             jnp.zeros((w.shape[0], U_COLS - OFF_GATES - 20), F32)], axis=1)
    o_ref[:, OFF_GATES:U_COLS] = gates.astype(BF16)


def _prep_w_in(w_in):
    depth, d, cols = w_in.shape
    rows = 256
    return pl.pallas_call(
        _prep_w_in_kernel,
        grid=(depth, d // rows),
        in_specs=[pl.BlockSpec((None, rows, cols), lambda l, i: (l, i, 0))],
        out_specs=pl.BlockSpec((None, rows, U_COLS), lambda l, i: (l, i, 0)),
        out_shape=jax.ShapeDtypeStruct((depth, d, U_COLS), BF16),
        compiler_params=pltpu.CompilerParams(dimension_semantics=("arbitrary", "arbitrary"),
                                             vmem_limit_bytes=VMEM_LIMIT),
        name="prep_w_in",
    )(w_in)


def _gate_row(parts):
    row = jnp.zeros((LANES,), F32)
    for off, val in parts:
        row = lax.dynamic_update_slice(row, val.astype(F32), (off,))
    return row.reshape(1, LANES)


def _decoder(x, states, seq_t, chunk, seqs_per_step, emb_g, emb_b, lb_logits, w_in_p, layer_w):
    bsz, t, _ = x.shape
    n = bsz * t
    h = x.reshape(n, D_MODEL)
    new_states = []
    for l in range(DEPTH):
        (conv_a_w, conv_a_b, a_log_a, dt_bias_a, norm_a_g, conv_b_w, conv_b_b, a_log_b, dt_bias_b,
         d_skip_b, norm_b_g, norm_c_g, i_bias_d, f_bias_d, norm_d_g, w_out, ln1_g, ln1_b, w_up, w_down,
         ln2_g, ln2_b) = [w[l] for w in layer_w]
        if l == 0:
            h, u = _inproj(h, w_in_p, l, emb_g.reshape(1, -1), emb_b.reshape(1, -1), True)
        else:
            (u,) = _inproj(h, w_in_p, l, emb_g.reshape(1, -1), emb_b.reshape(1, -1), False)
        params = (conv_a_w, conv_a_b.reshape(1, -1), conv_b_w, conv_b_b.reshape(1, -1),
                  _gate_row([(GL_DECAY_A, dt_bias_a), (GL_DT_B, dt_bias_b), (GL_I_D, i_bias_d), (GL_F_D, f_bias_d)]),
                  _gate_row([(GL_DECAY_A, a_log_a), (GL_DT_B, a_log_b)]),
                  norm_a_g.reshape(1, -1), norm_b_g.reshape(1, -1), norm_c_g.reshape(1, -1),
                  norm_d_g.reshape(1, -1), jnp.repeat(d_skip_b, HEAD_DIM).reshape(1, -1), lb_logits)
        y, st = _mixers(u.reshape(bsz, t, U_COLS), states, params,
                        layer=l, T=seq_t, L=chunk, P=seqs_per_step)
        new_states.append(st)
        h = _outffn(y.reshape(n, D_MODEL), h, w_out.astype(BF16), ln1_g.reshape(1, -1), ln1_b.reshape(1, -1),
                    w_up.astype(BF16), w_down.astype(BF16), ln2_g.reshape(1, -1), ln2_b.reshape(1, -1))
    return h.reshape(bsz, t, D_MODEL), tuple(jnp.stack(z) for z in zip(*new_states))


def kernel(x_prompt, x_sample, state_a_conv, state_a_ssm, state_b_conv, state_b_ssm, state_c_ssm, state_d_cmem, state_d_nvec, state_d_mstab, emb_ln_g, emb_ln_b, lb_logits_c, w_in, conv_a_w, conv_a_b, a_log_a, dt_bias_a, norm_a_g, conv_b_w, conv_b_b, a_log_b, dt_bias_b, d_skip_b, norm_b_g, norm_c_g, i_bias_d, f_bias_d, norm_d_g, w_out, ln1_g, ln1_b, w_up, w_down, ln2_g, ln2_b):
    layer_w = (conv_a_w, conv_a_b, a_log_a, dt_bias_a, norm_a_g, conv_b_w, conv_b_b, a_log_b, dt_bias_b,
               d_skip_b, norm_b_g, norm_c_g, i_bias_d, f_bias_d, norm_d_g, w_out, ln1_g, ln1_b, w_up, w_down,
               ln2_g, ln2_b)
    sample_states = (state_a_conv, state_a_ssm, state_b_conv, state_b_ssm,
                     state_c_ssm, state_d_cmem, state_d_nvec, state_d_mstab)
    prompt_states = tuple(jnp.zeros((1, x_prompt.shape[0]) + s.shape[2:], F32) for s in sample_states)
    w_in_p = _prep_w_in(w_in)
    y_p, ps = _decoder(x_prompt, prompt_states, CHUNK, CHUNK, SEQS_PER_STEP_PROMPT,
                       emb_ln_g, emb_ln_b, lb_logits_c, w_in_p, layer_w)
    y_s, ss = _decoder(x_sample, sample_states, x_sample.shape[1], SAMPLE_ROWS, SEQS_PER_STEP_SAMPLE,
                       emb_ln_g, emb_ln_b, lb_logits_c, w_in_p, layer_w)
    return (y_p, y_s) + ps + ss
```

```python
import functools

import jax
import jax.numpy as jnp
from jax import lax
from jax.experimental import pallas as pl
from jax.experimental.pallas import tpu as pltpu

F32 = jnp.float32
BF16 = jnp.bfloat16

D_MODEL = 1024
DEPTH = 2
N_HEADS = 4
HEAD_DIM = 64
GROUP_WIDTH = N_HEADS * HEAD_DIM
G_B = 2
CONV_K = 4
CHUNK = 64
D_FF = 4 * D_MODEL
EPS = 1e-6
NEG = -1e30
ALPHA = (2 * DEPTH) ** 0.25

CONV_A_CH = 3 * GROUP_WIDTH
CONV_B_CH = GROUP_WIDTH + 2 * G_B * HEAD_DIM

OFF_A_QKV = 0
OFF_A_Z = 768
OFF_B_Z = 1024
OFF_B_XBC = 1280
OFF_C_Q = 1792
OFF_C_F = 2048
OFF_C_I = 2304
OFF_C_G = 2560
OFF_D_Q = 2816
OFF_D_K = 3072
OFF_D_V = 3328
OFF_D_O = 3584
OFF_GATES = 3840
U_COLS = 3968
LANES = 128
GL_BETA = 0
GL_DECAY_A = 4
GL_DT_B = 8
GL_I_D = 12
GL_F_D = 16
GL_DT_B2 = 20
N_GATE_COLS = 24

CONV_TAIL = CONV_K - 1
ROW0 = 8

VMEM_LIMIT = 56 * 1024 * 1024
TOKEN_TILE = 256
SEQS_PER_STEP_PROMPT = 4
SEQS_PER_STEP_SAMPLE = 4
SAMPLE_ROWS = 8


def _layernorm(x, g, b):
    mu = jnp.mean(x, axis=-1, keepdims=True)
    xc = x - mu
    var = jnp.mean(xc * xc, axis=-1, keepdims=True)
    return xc * lax.rsqrt(var + EPS) * g + b


def _sigmoid(x):
    return 1.0 / (1.0 + jnp.exp(-x))


def _silu(x):
    return x * _sigmoid(x)


def _dot(a, b):
    return jnp.dot(a, b, preferred_element_type=F32)


def _mm(a, b):
    return _dot(a.astype(BF16), b.astype(BF16))


def _nt(a, b):
    return lax.dot_general(a.astype(BF16), b.astype(BF16), (((1,), (1,)), ((), ())),
                           preferred_element_type=F32)


def _tn(a, b):
    n = a.shape[1]
    eye = (lax.broadcasted_iota(jnp.int32, (n, n), 0) == lax.broadcasted_iota(jnp.int32, (n, n), 1))
    a_t = _nt(jnp.where(eye, 1.0, 0.0), a)
    return _dot(a_t.astype(BF16), b.astype(BF16))


def _split2(x):
    hi = x.astype(BF16)
    return hi, (x - hi.astype(F32)).astype(BF16)


def _split3(x):
    hi = x.astype(BF16)
    r1 = x - hi.astype(F32)
    mid = r1.astype(BF16)
    r2 = r1 - mid.astype(F32)
    return hi, mid, r2.astype(BF16)


def _dot01(m01, parts):
    acc = _dot(m01, parts[0])
    for p in parts[1:]:
        acc = acc + _dot(m01, p)
    return acc


def _rms(x):
    return x * lax.rsqrt(jnp.mean(x * x, axis=-1, keepdims=True) + EPS)


def _levels(L):
    out, b = [], 1
    while 2 * b <= L:
        out.append(b)
        b *= 2
    return out


def _run_interleaved(chains):
    chains = list(chains)
    while chains:
        alive = []
        for ch in chains:
            try:
                next(ch)
                alive.append(ch)
            except StopIteration:
                pass
        chains = alive


def _inproj_kernel(x_ref, w_ref, g_ref, b_ref, *out_refs, apply_ln):
    x = x_ref[...]
    if apply_ln:
        x = _layernorm(x, g_ref[...], b_ref[...])
        out_refs[0][...] = x
    out_refs[-1][...] = _dot(x.astype(BF16), w_ref[...])


def _inproj(x, w_in_p, layer, g, b, apply_ln):
    n = x.shape[0]
    tm = min(TOKEN_TILE, n)
    grid = (n // tm,)
    const = lambda i: (0, 0)
    out_shape = [jax.ShapeDtypeStruct((n, U_COLS), F32)]
    out_specs = [pl.BlockSpec((tm, U_COLS), lambda i: (i, 0))]
    if apply_ln:
        out_shape = [jax.ShapeDtypeStruct((n, D_MODEL), F32)] + out_shape
        out_specs = [pl.BlockSpec((tm, D_MODEL), lambda i: (i, 0))] + out_specs
    return pl.pallas_call(
        functools.partial(_inproj_kernel, apply_ln=apply_ln),
        grid=grid,
        in_specs=[pl.BlockSpec((tm, D_MODEL), lambda i: (i, 0)),
                  pl.BlockSpec((None, D_MODEL, U_COLS), lambda i: (layer, 0, 0), pipeline_mode=pl.Buffered(1)),
                  pl.BlockSpec((1, D_MODEL), const),
                  pl.BlockSpec((1, D_MODEL), const)],
        out_specs=out_specs,
        out_shape=out_shape,
        compiler_params=pltpu.CompilerParams(dimension_semantics=("arbitrary",),
                                             vmem_limit_bytes=VMEM_LIMIT),
        name="inproj_ln" if apply_ln else "inproj",
    )(x, w_in_p, g, b)


def _outffn_kernel(mix_ref, x_ref, wo_ref, g1_ref, b1_ref, wu_ref, wd_ref, g2_ref, b2_ref, o_ref):
    m = _dot(mix_ref[...].astype(BF16), wo_ref[...])
    h1 = _layernorm(ALPHA * x_ref[...] + m, g1_ref[...], b1_ref[...])
    up = _dot(h1.astype(BF16), wu_ref[...])
    act = jnp.square(jnp.maximum(up, 0.0))
    ff = _dot(act.astype(BF16), wd_ref[...])
    o_ref[...] = _layernorm(ALPHA * h1 + ff, g2_ref[...], b2_ref[...])


def _outffn(mix, x, wo, g1, b1, wu, wd, g2, b2):
    n = x.shape[0]
    tm = min(TOKEN_TILE, n)
    const = lambda i: (0, 0)
    tile = pl.BlockSpec((tm, D_MODEL), lambda i: (i, 0))
    row = pl.BlockSpec((1, D_MODEL), const)
    return pl.pallas_call(
        _outffn_kernel,
        grid=(n // tm,),
        in_specs=[tile, tile,
                  pl.BlockSpec((D_MODEL, D_MODEL), const, pipeline_mode=pl.Buffered(1)),
                  row, row,
                  pl.BlockSpec((D_MODEL, D_FF), const, pipeline_mode=pl.Buffered(1)),
                  pl.BlockSpec((D_FF, D_MODEL), const, pipeline_mode=pl.Buffered(1)),
                  row, row],
        out_specs=tile,
        out_shape=jax.ShapeDtypeStruct((n, D_MODEL), F32),
        compiler_params=pltpu.CompilerParams(dimension_semantics=("arbitrary",),
                                             vmem_limit_bytes=VMEM_LIMIT),
        name="outffn",
    )(mix, x, wo, g1, b1, wu, wd, g2, b2)


def _mixer_kernel(u_ref, ca_in, sa_in, cb_in, sb_in, sc_in, dc_in, dn_in, dm_in,
                  caw_ref, cab_ref, cbw_ref, cbb_ref, gbias_ref, galog_ref,
                  nag_ref, nbg_ref, ncg_ref, ndg_ref, dskip_ref, lblog_ref,
                  y_ref, ca_out, sa_out, cb_out, sb_out, sc_out, dc_out, dn_out, dm_out,
                  ub, sa, sb, sc, dc, dn, dm,
                  *, L, T, NC, P, layer):
    c = pl.program_id(1)
    H, HD = N_HEADS, HEAD_DIM
    R0, R1 = ROW0, ROW0 + L
    padded = T < L
    a_cols = slice(OFF_A_QKV, OFF_A_QKV + CONV_A_CH)
    b_cols = slice(OFF_B_XBC, OFF_B_XBC + CONV_B_CH)

    @pl.when(c == 0)
    def _():
        ub[:, R0 - CONV_TAIL:R0, a_cols] = ca_in[...]
        ub[:, R0 - CONV_TAIL:R0, b_cols] = cb_in[...]
        sa[...] = sa_in[...]
        sb[...] = sb_in[...]
        sc[...] = sc_in[...]
        dc[...] = dc_in[...]
        dn[...] = dn_in[...]
        dm[...] = dm_in[...]

    ub[:, R0:R0 + T, :] = u_ref[...]
    if padded:
        ub[:, R0 + T:R1, :] = jnp.zeros((P, L - T, U_COLS), F32)

    ri = lax.broadcasted_iota(jnp.int32, (L, L), 0)
    ci = lax.broadcasted_iota(jnp.int32, (L, L), 1)
    incl = ri >= ci
    strict = ri > ci
    tril01 = jnp.where(incl, 1.0, 0.0).astype(BF16)
    eye = jnp.where(ri == ci, 1.0, 0.0)
    levels = _levels(L)

    def level_mask(b):
        sh = b.bit_length() - 1
        rb = ri >> sh
        cb = ci >> sh
        return (rb - cb == 1) & ((cb & 1) == 0)

    lmask = {b: level_mask(b) for b in levels}
    sel01 = {}
    for b in levels:
        sh = b.bit_length()
        sel01[b] = jnp.where(ci == (((ri >> sh) << sh) + (b - 1)), 1.0, 0.0).astype(BF16)
    rowv = lax.broadcasted_iota(jnp.int32, (L, 1), 0) < T
    lane = lax.broadcasted_iota(jnp.int32, (1, LANES), 1)
    is_dt = (lane >= GL_DECAY_A) & (lane < GL_I_D)
    is_i = (lane >= GL_I_D) & (lane < GL_F_D)
    is_f = (lane >= GL_F_D) & (lane < GL_F_D + H)

    pl_sm = lblog_ref[...]
    pl_sm = jnp.exp(pl_sm - jnp.max(pl_sm, axis=0, keepdims=True))
    pl_sm = pl_sm / jnp.sum(pl_sm, axis=0, keepdims=True)
    lb = pl_sm[0:1, :]
    for i in range(1, layer + 1):
        lb = lb + pl_sm[i:i + 1, :]
    lb = lb - pl_sm[0:1, :]

    m_last_out = {}

    def seq_chains(s):
        def seg(off, width):
            return ub[s, R0:R1, off:off + width]

        def conv(off, width, w_ref, b_ref):
            acc = b_ref[...] + ub[s, R0:R1, off:off + width] * w_ref[CONV_K - 1:CONV_K, :]
            for j in range(CONV_K - 1):
                lo = R0 - CONV_TAIL + j
                acc = acc + ub[s, lo:lo + L, off:off + width] * w_ref[j:j + 1, :]
            return acc

        def put_y(off, val):
            y_ref[s, :, off:off + val.shape[1]] = val[:T] if padded else val

        pre = seg(OFF_GATES, LANES) + gbias_ref[...]
        e = jnp.exp(-jnp.abs(pre))
        l1p = jnp.log1p(e)
        softplus = jnp.maximum(pre, 0.0) + l1p
        logsig = jnp.minimum(pre, 0.0) - l1p
        logdec = jnp.where(is_dt, -jnp.exp(galog_ref[...]) * softplus, jnp.where(is_f, logsig, 0.0))
        gval = jnp.where(lane < GL_DECAY_A, _sigmoid(pre), jnp.where(is_i, pre, softplus))
        if padded:
            logdec = jnp.where(rowv, logdec, 0.0)
            gval = jnp.where(rowv, gval, jnp.where(is_i, NEG, 0.0))
        gcum = _dot01(tril01, _split3(logdec))
        gt = jnp.where(is_i, gval, gcum).T

        def col(x, j):
            return x[:, j:j + 1]

        def decay_mat(j):
            return jnp.where(incl, jnp.exp(jnp.minimum(col(gcum, j) - gt[j:j + 1, :], 0.0)), 0.0)

        qkv = _silu(conv(OFF_A_QKV, CONV_A_CH, caw_ref, cab_ref))
        a_z = seg(OFF_A_Z, GROUP_WIDTH)

        def chain_a(h):
            sl = slice(h * HD, (h + 1) * HD)
            q = qkv[:, sl]
            k = qkv[:, GROUP_WIDTH + h * HD:GROUP_WIDTH + (h + 1) * HD]
            v = qkv[:, 2 * GROUP_WIDTH + h * HD:2 * GROUP_WIDTH + (h + 1) * HD]
            q = q * lax.rsqrt(jnp.sum(q * q, axis=-1, keepdims=True) + EPS) * (HD ** -0.5)
            k = k * lax.rsqrt(jnp.sum(k * k, axis=-1, keepdims=True) + EPS)
            g = col(gcum, GL_DECAY_A + h)
            beta = col(gval, GL_BETA + h)
            dmat = decay_mat(GL_DECAY_A + h)
            eg = jnp.exp(g)
            kb = k.astype(BF16)
            kk = _nt(kb, kb)
            qk = _nt(q, kb)
            yield
            nmat = jnp.where(strict, beta * kk * dmat, 0.0)
            tinv = eye - jnp.where(lmask[1], nmat, 0.0)
            for b in levels[1:]:
                t_hi = tinv.astype(BF16)
                o_hi, o_lo = _split2(jnp.where(lmask[b], nmat, 0.0))
                x = _dot(o_hi, t_hi) + _dot(o_lo, t_hi)
                yield
                x_hi, x_lo = _split2(x)
                tinv = tinv - (_dot(t_hi, x_hi) + _dot(t_hi, x_lo))
                yield
            r_hi, r_lo = _split2(jnp.concatenate([beta * v, (beta * eg) * k], axis=-1))
            t_hi = tinv.astype(BF16)
            sol = _dot(t_hi, r_hi) + _dot(t_hi, r_lo)
            s0 = sa[s, h]
            qs = _mm(q, s0)
            yield
            unew = sol[:, :HD] - _mm(sol[:, HD:], s0)
            yield
            o = eg * qs + _mm(qk * dmat, unew)
            g_last = g[L - 1:L, :]
            sa[s, h] = jnp.exp(g_last) * s0 + _tn(k * jnp.exp(g_last - g), unew)
            yield
            put_y(h * HD, _rms(o) * nag_ref[:, sl] * _silu(a_z[:, sl]))

        xbc = _silu(conv(OFF_B_XBC, CONV_B_CH, cbw_ref, cbb_ref))
        b_z = seg(OFF_B_Z, GROUP_WIDTH)

        def chain_b(grp):
            bmat = xbc[:, GROUP_WIDTH + grp * HD:GROUP_WIDTH + (grp + 1) * HD]
            cmat = xbc[:, GROUP_WIDTH + G_B * HD + grp * HD:GROUP_WIDTH + G_B * HD + (grp + 1) * HD]
            cb = _nt(cmat, bmat)
            yield
            outs = []
            for h in range(grp * (H // G_B), (grp + 1) * (H // G_B)):
                sl = slice(h * HD, (h + 1) * HD)
                g = col(gcum, GL_DT_B + h)
                dt = col(gval, GL_DT_B + h)
                xh = xbc[:, sl]
                v = xh * dt
                s0 = sb[s, h]
                o = jnp.exp(g) * _mm(cmat, s0) + _mm(cb * decay_mat(GL_DT_B + h), v)
                g_last = g[L - 1:L, :]
                sb[s, h] = jnp.exp(g_last) * s0 + _tn(bmat * jnp.exp(g_last - g), v)
                yield
                outs.append(o + dskip_ref[:, sl] * xh)
            gw = GROUP_WIDTH // G_B
            gsl = slice(grp * gw, (grp + 1) * gw)
            ob = jnp.concatenate(outs, axis=-1) * _silu(b_z[:, gsl])
            put_y(GROUP_WIDTH + grp * gw, _rms(ob) * nbg_ref[:, gsl])

        fgate = lb + (1.0 - lb) * _sigmoid(seg(OFF_C_F, GROUP_WIDTH))
        if padded:
            fgate = jnp.where(rowv, fgate, 1.0)
        gc_parts = _split3(_dot01(tril01, _split3(jnp.log(fgate))))
        gc = gc_parts[0].astype(F32) + gc_parts[1].astype(F32) + gc_parts[2].astype(F32)
        kc = 1.0 - fgate
        c_q = seg(OFF_C_Q, GROUP_WIDTH)
        c_v = seg(OFF_C_I, GROUP_WIDTH)
        c_g = seg(OFF_C_G, GROUP_WIDTH)
        pref = {b: _dot01(sel01[b], gc_parts) for b in levels}
        g_last_t = jnp.broadcast_to(gc[L - 1:L, :], (8, GROUP_WIDTH)).T

        def chain_c(h):
            sl = slice(h * HD, (h + 1) * HD)
            q, k, v, g = c_q[:, sl], kc[:, sl], c_v[:, sl], gc[:, sl]
            att = None
            for b in levels:
                p = pref[b][:, sl]
                part = jnp.where(lmask[b],
                                 _nt(q * jnp.exp(jnp.minimum(g - p, 0.0)), k * jnp.exp(jnp.minimum(p - g, 0.0))),
                                 0.0)
                att = part if att is None else att + part
                yield
            s0 = sc[s, h]
            o = _mm(q * jnp.exp(g), s0) + _mm(att, v) + jnp.sum(q * k, axis=-1, keepdims=True) * v
            g_last = g[L - 1:L, :]
            sc[s, h] = jnp.exp(g_last_t[sl, 0:1]) * s0 + _tn(k * jnp.exp(g_last - g), v)
            yield
            put_y(2 * GROUP_WIDTH + h * HD, _rms(o) * ncg_ref[:, sl] * _sigmoid(c_g[:, sl]))

        d_q = seg(OFF_D_Q, GROUP_WIDTH)
        d_k = seg(OFF_D_K, GROUP_WIDTH)
        d_v = seg(OFF_D_V, GROUP_WIDTH)
        d_o = seg(OFF_D_O, GROUP_WIDTH)
        m_row = dm[s]

        def chain_d(h):
            sl = slice(h * HD, (h + 1) * HD)
            q, k, v = d_q[:, sl], d_k[:, sl] * (HD ** -0.5), d_v[:, sl]
            bcum = col(gcum, GL_F_D + h)
            m0 = m_row[:, h:h + 1]
            dmat = jnp.where(incl, bcum - gt[GL_F_D + h:GL_F_D + h + 1, :] + gt[GL_I_D + h:GL_I_D + h + 1, :], NEG)
            init = bcum + m0
            m_r = jnp.maximum(init, jnp.max(dmat, axis=-1, keepdims=True))
            qk = _nt(q, k)
            cmem = dc[s, h]
            qc = _mm(q, cmem)
            yield
            p = jnp.where(incl, jnp.exp(dmat - m_r), 0.0) * qk
            s_init = jnp.exp(init - m_r)
            nvec = dn[s, h:h + 1, :]
            num = s_init * qc + _mm(p, v)
            den = s_init * jnp.sum(q * nvec, axis=-1, keepdims=True) + jnp.sum(p, axis=-1, keepdims=True)
            m_last = m_r[L - 1:L, :]
            b_last = bcum[L - 1:L, :]
            scale = jnp.exp(b_last + m0 - m_last)
            kw = k * jnp.exp(b_last - bcum + col(gval, GL_I_D + h) - m_last)
            dc[s, h] = scale * cmem + _tn(kw, v)
            dn[s, h:h + 1, :] = scale * nvec + jnp.sum(kw, axis=0, keepdims=True)
            m_last_out[(s, h)] = m_last
            yield
            hh = num / jnp.maximum(jnp.abs(den), jnp.exp(-m_r))
            put_y(3 * GROUP_WIDTH + h * HD, _rms(hh) * ndg_ref[:, sl] * _sigmoid(d_o[:, sl]))

        return ([chain_a(h) for h in range(H)] + [chain_b(g) for g in range(G_B)]
                + [chain_c(h) for h in range(H)] + [chain_d(h) for h in range(H)])

    chains = []
    for s in range(P):
        chains += seq_chains(s)
    _run_interleaved(chains)

    for s in range(P):
        m_new = dm[s]
        for h in range(H):
            m_new = jnp.where(lane == h, m_last_out[(s, h)], m_new)
        dm[s] = m_new

    tail_a = ub[:, R0 + T - CONV_TAIL:R0 + T, a_cols]
    tail_b = ub[:, R0 + T - CONV_TAIL:R0 + T, b_cols]
    ub[:, R0 - CONV_TAIL:R0, a_cols] = tail_a
    ub[:, R0 - CONV_TAIL:R0, b_cols] = tail_b

    @pl.when(c == NC - 1)
    def _():
        ca_out[...] = tail_a
        cb_out[...] = tail_b
        sa_out[...] = sa[...]
        sb_out[...] = sb[...]
        sc_out[...] = sc[...]
        dc_out[...] = dc[...]
        dn_out[...] = dn[...]
        dm_out[...] = dm[...]


def _mixer_pair_kernel(u_ref, ca_in, sa_in, cb_in, sb_in, sc_in, dc_in, dn_in, dm_in,
                       caw_ref, cab_ref, cbw_ref, cbb_ref, gbias_ref, galog_ref,
                       nag_ref, nbg_ref, ncg_ref, ndg_ref, dskip_ref, lblog_ref,
                       y_ref, ca_out, sa_out, cb_out, sb_out, sc_out, dc_out, dn_out, dm_out,
                       cbuf, sa, sb, sc, dc, dn, dm,
                       *, NC, P, layer):
    c = pl.program_id(1)
    H, HD, L, W = N_HEADS, HEAD_DIM, CHUNK, LANES
    NP = H // 2
    R0, R1 = ROW0, ROW0 + L
    CB_A, CB_B = 0, CONV_A_CH
    f32 = F32

    def iota(shape, d):
        return lax.broadcasted_iota(jnp.int32, shape, d)

    def one_bf16(mask):
        return jnp.where(mask, 1.0, 0.0).astype(BF16)

    r_ll, c_ll = iota((L, L), 0), iota((L, L), 1)
    tril01 = one_bf16(r_ll >= c_ll)
    ones_ll = jnp.ones((L, L), BF16)
    r_lw, c_lw = iota((L, W), 0), iota((L, W), 1)
    j_lw = c_lw & (HD - 1)
    eyepair = jnp.where(r_lw == j_lw, 1.0, 0.0)
    incl_p = r_lw >= j_lw
    strict_p = r_lw > j_lw
    levels = _levels(L)
    lmask = {}
    sel01 = {}
    for b in levels:
        sh = b.bit_length() - 1
        rb, cb = r_lw >> sh, j_lw >> sh
        lmask[b] = (rb - cb == 1) & ((cb & 1) == 0)
        sel01[b] = one_bf16(c_ll == (((r_ll >> (sh + 1)) << (sh + 1)) + (b - 1)))
    r_ww, c_ww = iota((W, W), 0), iota((W, W), 1)
    bdiag = (r_ww >> 6) == (c_ww >> 6)
    ones_bd = one_bf16(bdiag)
    ones_ww = jnp.ones((W, W), BF16)
    eye_ww = one_bf16(r_ww == c_ww)
    lane = iota((1, W), 1)
    lo_half = lane < HD
    is_dt = (lane >= GL_DECAY_A) & (lane < GL_I_D)
    is_i = (lane >= GL_I_D) & (lane < GL_F_D)
    is_f = (lane >= GL_F_D) & (lane < GL_F_D + H)
    is_cum = is_dt | is_f
    exp_bases = (GL_DECAY_A, GL_BETA, GL_DT_B, GL_DT_B2, GL_F_D, GL_I_D)
    exp01 = jnp.concatenate([one_bf16(r_ww == (base + 2 * p + (c_ww >> 6)))
                             for base in exp_bases for p in range(NP)], axis=1)

    def bd(x):
        return jnp.where(bdiag, jnp.concatenate([x, x], axis=0), 0.0)

    def dot2(x, m01):
        hi, lo = _split2(x)
        return _dot(hi, m01) + _dot(lo, m01)

    def half_sums(x):
        return dot2(x, ones_bd)

    def row_form(x):
        return _dot01(ones_ll, _split3(x * eyepair))

    def transpose_bf16(x):
        return _nt(eye_ww, x)

    pl_sm = lblog_ref[...]
    pl_sm = jnp.exp(pl_sm - jnp.max(pl_sm, axis=0, keepdims=True))
    pl_sm = pl_sm / jnp.sum(pl_sm, axis=0, keepdims=True)
    lb = pl_sm[0:1, :]
    for i in range(1, layer + 1):
        lb = lb + pl_sm[i:i + 1, :]
    lb = lb - pl_sm[0:1, :]

    @pl.when(c == 0)
    def _():
        cbuf[:, R0 - CONV_TAIL:R0, CB_A:CB_A + CONV_A_CH] = ca_in[...]
        cbuf[:, R0 - CONV_TAIL:R0, CB_B:CB_B + CONV_B_CH] = cb_in[...]
        zero = jnp.zeros((W, W), f32)
        for s in range(P):
            m_row = dm_in[s]
            for p in range(NP):
                h0, h1 = 2 * p, 2 * p + 1
                for ref, src in ((sa, sa_in), (sc, sc_in), (dc, dc_in)):
                    ref[s, p] = zero
                    ref[s, p, 0:HD, 0:HD] = src[s, h0]
                    ref[s, p, HD:W, HD:W] = src[s, h1]
                sb[s, p] = zero
                sb[s, p, p * HD:(p + 1) * HD, 0:HD] = sb_in[s, h0]
                sb[s, p, p * HD:(p + 1) * HD, HD:W] = sb_in[s, h1]
                dn[s, p] = jnp.concatenate([dn_in[s, h0:h0 + 1, :], dn_in[s, h1:h1 + 1, :]], axis=1)
                dm[s, p] = jnp.where(lo_half, m_row[:, h0:h0 + 1], m_row[:, h1:h1 + 1])

    cbuf[:, R0:R1, CB_A:CB_A + CONV_A_CH] = u_ref[:, :, OFF_A_QKV:OFF_A_QKV + CONV_A_CH]
    cbuf[:, R0:R1, CB_B:CB_B + CONV_B_CH] = u_ref[:, :, OFF_B_XBC:OFF_B_XBC + CONV_B_CH]

    def seq_chains(s):
        def seg(off, width):
            return u_ref[s, :, off:off + width]

        def conv(off, width, w_ref, b_ref):
            acc = b_ref[...] + cbuf[s, R0:R1, off:off + width] * w_ref[CONV_K - 1:CONV_K, :]
            for j in range(CONV_K - 1):
                lo = R0 - CONV_TAIL + j
                acc = acc + cbuf[s, lo:lo + L, off:off + width] * w_ref[j:j + 1, :]
            return acc

        pre = seg(OFF_GATES, W) + gbias_ref[...]
        e = jnp.exp(-jnp.abs(pre))
        l1p = jnp.log1p(e)
        softplus = jnp.maximum(pre, 0.0) + l1p
        logsig = jnp.minimum(pre, 0.0) - l1p
        logdec = jnp.where(is_dt, -jnp.exp(galog_ref[...]) * softplus, jnp.where(is_f, logsig, 0.0))
        gval = jnp.where(lane < GL_DECAY_A, _sigmoid(pre), jnp.where(is_i, pre, softplus))
        gcum = _dot01(tril01, _split3(logdec))
        src = _split3(jnp.where(is_cum, gcum, gval))
        expd = _dot(src[0], exp01) + _dot(src[1], exp01) + _dot(src[2], exp01)

        def expanded(k, p):
            o = (k * NP + p) * W
            return expd[:, o:o + W]

        qkv = _silu(conv(CB_A, CONV_A_CH, caw_ref, cab_ref))

        def chain_a(p):
            ps = slice(p * W, (p + 1) * W)
            q = qkv[:, p * W:(p + 1) * W]
            k = qkv[:, GROUP_WIDTH + p * W:GROUP_WIDTH + (p + 1) * W]
            v = qkv[:, 2 * GROUP_WIDTH + p * W:2 * GROUP_WIDTH + (p + 1) * W]
            ssq = half_sums(q * q)
            ssk = half_sums(k * k)
            g = expanded(0, p)
            beta = expanded(1, p)
            grow = row_form(g)
            yield
            q = q * lax.rsqrt(ssq + EPS) * (HD ** -0.5)
            k = k * lax.rsqrt(ssk + EPS)
            eg = jnp.exp(g)
            g_last = g[L - 1:L, :]
            kbd = bd(k).astype(BF16)
            kk = _nt(k, kbd)
            qk = _nt(q, kbd)
            kw_t = transpose_bf16(k * jnp.exp(g_last - g))
            dmat = jnp.where(incl_p, jnp.exp(jnp.minimum(g - grow, 0.0)), 0.0)
            yield
            nmat = jnp.where(strict_p, beta * kk * dmat, 0.0)
            tinv = eyepair - jnp.where(lmask[1], nmat, 0.0)
            for b in levels[1:]:
                t_bd = bd(tinv).astype(BF16)
                x = dot2(jnp.where(lmask[b], nmat, 0.0), t_bd)
                yield
                x_hi, x_lo = _split2(bd(x))
                t_hi = tinv.astype(BF16)
                tinv = tinv - (_dot(t_hi, x_hi) + _dot(t_hi, x_lo))
                yield
            r_hi, r_lo = _split2(jnp.concatenate([bd(beta * v), bd((beta * eg) * k)], axis=1))
            t_hi = tinv.astype(BF16)
            sol = _dot(t_hi, r_hi) + _dot(t_hi, r_lo)
            s0 = sa[s, p]
            qs = _mm(q, s0)
            yield
            unew = sol[:, :W] - _mm(sol[:, W:], s0)
            yield
            o = eg * qs + _mm(qk * dmat, bd(unew))
            sa[s, p] = jnp.exp(g_last) * s0 + jnp.where(bdiag, _mm(kw_t, unew), 0.0)
            ms = half_sums(o * o) * (1.0 / HD)
            yield
            a_z = seg(OFF_A_Z + p * W, W)
            y_ref[s, :, ps] = o * lax.rsqrt(ms + EPS) * nag_ref[:, ps] * _silu(a_z)

        xbc = _silu(conv(CB_B, CONV_B_CH, cbw_ref, cbb_ref))
        b_all = xbc[:, GROUP_WIDTH:GROUP_WIDTH + W]
        c_all = xbc[:, GROUP_WIDTH + W:GROUP_WIDTH + 2 * W]
        b_t = transpose_bf16(b_all)

        def chain_b(p):
            ps = slice(p * W, (p + 1) * W)
            in_grp = (lane >> 6) == p
            b_grp = jnp.where(in_grp, b_all, 0.0)
            cb = _nt(c_all, jnp.concatenate([b_grp, b_grp], axis=0))
            g = expanded(2, p)
            dt = expanded(3, p)
            grow = row_form(g)
            yield
            g_last = g[L - 1:L, :]
            xh = xbc[:, ps]
            v = xh * dt
            s0 = sb[s, p]
            dmat = jnp.where(incl_p, jnp.exp(jnp.minimum(g - grow, 0.0)), 0.0)
            o = jnp.exp(g) * _mm(c_all, s0) + _mm(cb * dmat, bd(v))
            upd = _dot(b_t, (v * jnp.exp(g_last - g)).astype(BF16))
            sb[s, p] = jnp.exp(g_last) * s0 + jnp.where((r_ww >> 6) == p, upd, 0.0)
            yield
            ob = (o + dskip_ref[:, ps] * xh) * _silu(seg(OFF_B_Z + p * W, W))
            ms = dot2(ob * ob, ones_ww) * (1.0 / W)
            yield
            y_ref[s, :, GROUP_WIDTH + p * W:GROUP_WIDTH + (p + 1) * W] = ob * lax.rsqrt(ms + EPS) * nbg_ref[:, ps]

        fgate = lb + (1.0 - lb) * _sigmoid(seg(OFF_C_F, GROUP_WIDTH))
        gc_parts = _split3(_dot01(tril01, _split3(jnp.log(fgate))))
        gc = gc_parts[0].astype(f32) + gc_parts[1].astype(f32) + gc_parts[2].astype(f32)
        kc = 1.0 - fgate
        pref = {b: _dot01(sel01[b], gc_parts) for b in levels}

        def chain_c(p):
            ps = slice(p * W, (p + 1) * W)
            q = seg(OFF_C_Q + p * W, W)
            v = seg(OFF_C_I + p * W, W)
            k, g = kc[:, ps], gc[:, ps]
            g_last = g[L - 1:L, :]
            kw_t = transpose_bf16(k * jnp.exp(g_last - g))
            e_parts = _split3(jnp.where(r_ww == c_ww, jnp.broadcast_to(g_last, (W, W)), 0.0))
            e_col = jnp.exp(_dot(e_parts[0], ones_ww) + _dot(e_parts[1], ones_ww) + _dot(e_parts[2], ones_ww))
            qk_diag = half_sums(q * k)
            yield
            att = None
            for b in levels:
                pb = pref[b][:, ps]
                kd = bd(k * jnp.exp(jnp.minimum(pb - g, 0.0))).astype(BF16)
                part = jnp.where(lmask[b], _nt(q * jnp.exp(jnp.minimum(g - pb, 0.0)), kd), 0.0)
                att = part if att is None else att + part
                yield
            s0 = sc[s, p]
            o = _mm(q * jnp.exp(g), s0) + _mm(att, bd(v)) + qk_diag * v
            sc[s, p] = e_col * s0 + jnp.where(bdiag, _mm(kw_t, v), 0.0)
            ms = half_sums(o * o) * (1.0 / HD)
            yield
            c_g = seg(OFF_C_G + p * W, W)
            y_ref[s, :, 2 * GROUP_WIDTH + p * W:2 * GROUP_WIDTH + (p + 1) * W] = (
                o * lax.rsqrt(ms + EPS) * ncg_ref[:, ps] * _sigmoid(c_g))

        def chain_d(p):
            ps = slice(p * W, (p + 1) * W)
            q = seg(OFF_D_Q + p * W, W)
            k = seg(OFF_D_K + p * W, W) * (HD ** -0.5)
            v = seg(OFF_D_V + p * W, W)
            bcum = expanded(4, p)
            ipre = expanded(5, p)
            m0 = dm[s, p]
            a = ipre - bcum
            arow = row_form(a)
            qk = _nt(q, bd(k).astype(BF16))
            cmem = dc[s, p]
            qc = _mm(q, cmem)
            nvec = dn[s, p]
            qn = half_sums(q * nvec)
            cm = a
            sh = 1
            while sh < L:
                cm = jnp.maximum(cm, jnp.where(r_lw >= sh, pltpu.roll(cm, sh, axis=0), NEG))
                sh *= 2
            yield
            m_r = bcum + jnp.maximum(m0, cm)
            pmat = jnp.where(incl_p, jnp.exp(bcum + arow - m_r), 0.0) * qk
            s_init = jnp.exp(bcum + m0 - m_r)
            num = s_init * qc + _mm(pmat, bd(v))
            den = s_init * qn + half_sums(pmat)
            m_last = m_r[L - 1:L, :]
            b_last = bcum[L - 1:L, :]
            scale = jnp.exp(b_last + m0 - m_last)
            kw = k * jnp.exp(b_last - bcum + ipre - m_last)
            kw_t = transpose_bf16(kw)
            yield
            dc[s, p] = scale * cmem + jnp.where(bdiag, _mm(kw_t, v), 0.0)
            dn[s, p] = scale * nvec + jnp.sum(kw, axis=0, keepdims=True)
            dm[s, p] = m_last
            hh = num / jnp.maximum(jnp.abs(den), jnp.exp(-m_r))
            ms = half_sums(hh * hh) * (1.0 / HD)
            yield
            d_o = seg(OFF_D_O + p * W, W)
            y_ref[s, :, 3 * GROUP_WIDTH + p * W:3 * GROUP_WIDTH + (p + 1) * W] = (
                hh * lax.rsqrt(ms + EPS) * ndg_ref[:, ps] * _sigmoid(d_o))

        return ([chain_a(p) for p in range(NP)] + [chain_c(p) for p in range(NP)]
                + [chain_b(p) for p in range(NP)] + [chain_d(p) for p in range(NP)])

    chains = []
    for s in range(P):
        chains += seq_chains(s)
    _run_interleaved(chains)

    tail_a = cbuf[:, R1 - CONV_TAIL:R1, CB_A:CB_A + CONV_A_CH]
    tail_b = cbuf[:, R1 - CONV_TAIL:R1, CB_B:CB_B + CONV_B_CH]
    cbuf[:, R0 - CONV_TAIL:R0, CB_A:CB_A + CONV_A_CH] = tail_a
    cbuf[:, R0 - CONV_TAIL:R0, CB_B:CB_B + CONV_B_CH] = tail_b

    @pl.when(c == NC - 1)
    def _():
        ca_out[...] = tail_a
        cb_out[...] = tail_b
        for s in range(P):
            m_row = jnp.zeros((1, W), f32)
            for p in range(NP):
                h0, h1 = 2 * p, 2 * p + 1
                for ref, dst in ((sa, sa_out), (sc, sc_out), (dc, dc_out)):
                    dst[s, h0] = ref[s, p, 0:HD, 0:HD]
                    dst[s, h1] = ref[s, p, HD:W, HD:W]
                sb_out[s, h0] = sb[s, p, p * HD:(p + 1) * HD, 0:HD]
                sb_out[s, h1] = sb[s, p, p * HD:(p + 1) * HD, HD:W]
                nrow = dn[s, p]
                dn_out[s, h0:h0 + 1, :] = nrow[:, 0:HD]
                dn_out[s, h1:h1 + 1, :] = nrow[:, HD:W]
                mp = dm[s, p]
                m_row = jnp.where(lane == h0, mp[:, 0:1], jnp.where(lane == h1, mp[:, HD:HD + 1], m_row))
            dm_out[s] = m_row


def _mixers(u, states, params, *, layer, T, L, P):
    bsz, ttot, _ = u.shape
    nc = ttot // T
    H, HD = N_HEADS, HEAD_DIM
    ca, s_a, cb, s_b, s_c, d_c, d_n, d_m = states
    d_m = jnp.pad(d_m, ((0, 0), (0, 0), (0, LANES - H))).reshape(d_m.shape[0], bsz, 1, LANES)

    def per_seq(shape):
        nd = len(shape)
        return pl.BlockSpec((P,) + tuple(shape), lambda b, c: (b,) + (0,) * nd)

    def per_seq_in(x, shape):
        nd = len(shape)
        lidx = layer if x.shape[0] > 1 else 0
        return pl.BlockSpec((None, P) + tuple(shape), lambda b, c: (lidx, b) + (0,) * nd)

    def whole(x):
        nd = x.ndim
        return pl.BlockSpec(x.shape, lambda b, c: (0,) * nd)

    state_dims = [(CONV_TAIL, CONV_A_CH), (H, HD, HD), (CONV_TAIL, CONV_B_CH), (H, HD, HD), (H, HD, HD),
                  (H, HD, HD), (H, HD), (1, LANES)]
    state_specs = [per_seq(d) for d in state_dims]
    state_shapes = [jax.ShapeDtypeStruct((bsz,) + d, F32) for d in state_dims]
    if T == CHUNK and L == CHUNK:
        body = functools.partial(_mixer_pair_kernel, NC=nc, P=P, layer=layer)
        pair_state = (P, H // 2, LANES, LANES)
        scratch = ([pltpu.VMEM((P, ROW0 + L, CONV_A_CH + CONV_B_CH), F32)]
                   + [pltpu.VMEM(pair_state, F32) for _ in range(4)]
                   + [pltpu.VMEM((P, H // 2, 1, LANES), F32) for _ in range(2)])
    else:
        body = functools.partial(_mixer_kernel, L=L, T=T, NC=nc, P=P, layer=layer)
        scratch = ([pltpu.VMEM((P, ROW0 + L, U_COLS), F32)]
                   + [pltpu.VMEM((P,) + d, F32) for d in state_dims[1:2] + state_dims[3:]])
    outs = pl.pallas_call(
        body,
        grid=(bsz // P, nc),
        in_specs=[pl.BlockSpec((P, T, U_COLS), lambda b, c: (b, c, 0))]
                 + [per_seq_in(x, d) for x, d in zip((ca, s_a, cb, s_b, s_c, d_c, d_n, d_m), state_dims)]
                 + [whole(p) for p in params],
        out_specs=[pl.BlockSpec((P, T, D_MODEL), lambda b, c: (b, c, 0))] + state_specs,
        out_shape=[jax.ShapeDtypeStruct((bsz, ttot, D_MODEL), F32)] + state_shapes,
        scratch_shapes=scratch,
        compiler_params=pltpu.CompilerParams(dimension_semantics=("arbitrary", "arbitrary"),
                                             vmem_limit_bytes=VMEM_LIMIT),
        name="mixers_l%d_t%d" % (layer, T),
    )(u, ca, s_a, cb, s_b, s_c, d_c, d_n, d_m, *params)
    y, new = outs[0], list(outs[1:])
    new[7] = new[7].reshape(bsz, LANES)[:, :H]
    return y, tuple(new)


def _prep_w_in_kernel(w_ref, o_ref):
    w = w_ref[...]
    o_ref[:, 0:1024] = w[:, 0:1024].astype(BF16)
    o_ref[:, 1024:1792] = w[:, 1032:1800].astype(BF16)
    o_ref[:, 1792:OFF_GATES] = w[:, 1804:3852].astype(BF16)
    gates = jnp.concatenate([w[:, 1024:1032], w[:, 1800:1804], w[:, 3852:3860], w[:, 1800:1804],
                             jnp.zeros((w.shape[0], U_COLS - OFF_GATES - N_GATE_COLS), F32)], axis=1)
    o_ref[:, OFF_GATES:U_COLS] = gates.astype(BF16)


def _prep_w_in(w_in):
    depth, d, cols = w_in.shape
    rows = 256
    return pl.pallas_call(
        _prep_w_in_kernel,
        grid=(depth, d // rows),
        in_specs=[pl.BlockSpec((None, rows, cols), lambda l, i: (l, i, 0))],
        out_specs=pl.BlockSpec((None, rows, U_COLS), lambda l, i: (l, i, 0)),
        out_shape=jax.ShapeDtypeStruct((depth, d, U_COLS), BF16),
        compiler_params=pltpu.CompilerParams(dimension_semantics=("arbitrary", "arbitrary"),
                                             vmem_limit_bytes=VMEM_LIMIT),
        name="prep_w_in",
    )(w_in)


def _gate_row(parts):
    row = jnp.zeros((LANES,), F32)
    for off, val in parts:
        row = lax.dynamic_update_slice(row, val.astype(F32), (off,))
    return row.reshape(1, LANES)


def _decoder(x, states, seq_t, chunk, seqs_per_step, emb_g, emb_b, lb_logits, w_in_p, layer_w):
    bsz, t, _ = x.shape
    n = bsz * t
    h = x.reshape(n, D_MODEL)
    new_states = []
    for l in range(DEPTH):
        (conv_a_w, conv_a_b, a_log_a, dt_bias_a, norm_a_g, conv_b_w, conv_b_b, a_log_b, dt_bias_b,
         d_skip_b, norm_b_g, norm_c_g, i_bias_d, f_bias_d, norm_d_g, w_out, ln1_g, ln1_b, w_up, w_down,
         ln2_g, ln2_b) = [w[l] for w in layer_w]
        if l == 0:
            h, u = _inproj(h, w_in_p, l, emb_g.reshape(1, -1), emb_b.reshape(1, -1), True)
        else:
            (u,) = _inproj(h, w_in_p, l, emb_g.reshape(1, -1), emb_b.reshape(1, -1), False)
        params = (conv_a_w, conv_a_b.reshape(1, -1), conv_b_w, conv_b_b.reshape(1, -1),
                  _gate_row([(GL_DECAY_A, dt_bias_a), (GL_DT_B, dt_bias_b), (GL_I_D, i_bias_d), (GL_F_D, f_bias_d),
                             (GL_DT_B2, dt_bias_b)]),
                  _gate_row([(GL_DECAY_A, a_log_a), (GL_DT_B, a_log_b)]),
                  norm_a_g.reshape(1, -1), norm_b_g.reshape(1, -1), norm_c_g.reshape(1, -1),
                  norm_d_g.reshape(1, -1), jnp.repeat(d_skip_b, HEAD_DIM).reshape(1, -1), lb_logits)
        y, st = _mixers(u.reshape(bsz, t, U_COLS), states, params,
                        layer=l, T=seq_t, L=chunk, P=seqs_per_step)
        new_states.append(st)
        h = _outffn(y.reshape(n, D_MODEL), h, w_out.astype(BF16), ln1_g.reshape(1, -1), ln1_b.reshape(1, -1),
                    w_up.astype(BF16), w_down.astype(BF16), ln2_g.reshape(1, -1), ln2_b.reshape(1, -1))
    return h.reshape(bsz, t, D_MODEL), tuple(jnp.stack(z) for z in zip(*new_states))


def kernel(x_prompt, x_sample, state_a_conv, state_a_ssm, state_b_conv, state_b_ssm, state_c_ssm, state_d_cmem, state_d_nvec, state_d_mstab, emb_ln_g, emb_ln_b, lb_logits_c, w_in, conv_a_w, conv_a_b, a_log_a, dt_bias_a, norm_a_g, conv_b_w, conv_b_b, a_log_b, dt_bias_b, d_skip_b, norm_b_g, norm_c_g, i_bias_d, f_bias_d, norm_d_g, w_out, ln1_g, ln1_b, w_up, w_down, ln2_g, ln2_b):
    layer_w = (conv_a_w, conv_a_b, a_log_a, dt_bias_a, norm_a_g, conv_b_w, conv_b_b, a_log_b, dt_bias_b,
               d_skip_b, norm_b_g, norm_c_g, i_bias_d, f_bias_d, norm_d_g, w_out, ln1_g, ln1_b, w_up, w_down,
               ln2_g, ln2_b)
    sample_states = (state_a_conv, state_a_ssm, state_b_conv, state_b_ssm,
                     state_c_ssm, state_d_cmem, state_d_nvec, state_d_mstab)
    prompt_states = tuple(jnp.zeros((1, x_prompt.shape[0]) + s.shape[2:], F32) for s in sample_states)
    w_in_p = _prep_w_in(w_in)
    y_p, ps = _decoder(x_prompt, prompt_states, CHUNK, CHUNK, SEQS_PER_STEP_PROMPT,
                       emb_ln_g, emb_ln_b, lb_logits_c, w_in_p, layer_w)
    y_s, ss = _decoder(x_sample, sample_states, x_sample.shape[1], SAMPLE_ROWS, SEQS_PER_STEP_SAMPLE,
                       emb_ln_g, emb_ln_b, lb_logits_c, w_in_p, layer_w)
    return (y_p, y_s) + ps + ss
```

```python
import functools

import jax
import jax.numpy as jnp
from jax import lax
from jax.experimental import pallas as pl
from jax.experimental.pallas import tpu as pltpu

F32 = jnp.float32
BF16 = jnp.bfloat16

D_MODEL = 1024
DEPTH = 2
N_HEADS = 4
HEAD_DIM = 64
GROUP_WIDTH = N_HEADS * HEAD_DIM
G_B = 2
CONV_K = 4
CHUNK = 64
D_FF = 4 * D_MODEL
EPS = 1e-6
NEG = -1e30
ALPHA = (2 * DEPTH) ** 0.25

CONV_A_CH = 3 * GROUP_WIDTH
CONV_B_CH = GROUP_WIDTH + 2 * G_B * HEAD_DIM

OFF_A_QKV = 0
OFF_A_Z = 768
OFF_B_Z = 1024
OFF_B_XBC = 1280
OFF_C_Q = 1792
OFF_C_F = 2048
OFF_C_I = 2304
OFF_C_G = 2560
OFF_D_Q = 2816
OFF_D_K = 3072
OFF_D_V = 3328
OFF_D_O = 3584
OFF_GATES = 3840
U_COLS = 3968
LANES = 128
GL_BETA = 0
GL_DECAY_A = 4
GL_DT_B = 8
GL_I_D = 12
GL_F_D = 16
GL_DT_B2 = 20
N_GATE_COLS = 24

CONV_TAIL = CONV_K - 1
ROW0 = 8

VMEM_LIMIT = 56 * 1024 * 1024
TOKEN_TILE = 256
SEQS_PER_STEP_PROMPT = 4
SEQS_PER_STEP_SAMPLE = 4
SAMPLE_ROWS = 8


def _layernorm(x, g, b):
    mu = jnp.mean(x, axis=-1, keepdims=True)
    xc = x - mu
    var = jnp.mean(xc * xc, axis=-1, keepdims=True)
    return xc * lax.rsqrt(var + EPS) * g + b


def _sigmoid(x):
    return 1.0 / (1.0 + jnp.exp(-x))


def _silu(x):
    return x * _sigmoid(x)


def _dot(a, b):
    return jnp.dot(a, b, preferred_element_type=F32)


def _mm(a, b):
    return _dot(a.astype(BF16), b.astype(BF16))


def _nt(a, b):
    return lax.dot_general(a.astype(BF16), b.astype(BF16), (((1,), (1,)), ((), ())),
                           preferred_element_type=F32)


def _tn(a, b):
    n = a.shape[1]
    eye = (lax.broadcasted_iota(jnp.int32, (n, n), 0) == lax.broadcasted_iota(jnp.int32, (n, n), 1))
    a_t = _nt(jnp.where(eye, 1.0, 0.0), a)
    return _dot(a_t.astype(BF16), b.astype(BF16))


def _split2(x):
    hi = x.astype(BF16)
    return hi, (x - hi.astype(F32)).astype(BF16)


def _split3(x):
    hi = x.astype(BF16)
    r1 = x - hi.astype(F32)
    mid = r1.astype(BF16)
    r2 = r1 - mid.astype(F32)
    return hi, mid, r2.astype(BF16)


def _dot01(m01, parts):
    acc = _dot(m01, parts[0])
    for p in parts[1:]:
        acc = acc + _dot(m01, p)
    return acc


def _rms(x):
    return x * lax.rsqrt(jnp.mean(x * x, axis=-1, keepdims=True) + EPS)


def _levels(L):
    out, b = [], 1
    while 2 * b <= L:
        out.append(b)
        b *= 2
    return out


def _run_interleaved(chains):
    chains = list(chains)
    while chains:
        alive = []
        for ch in chains:
            try:
                next(ch)
                alive.append(ch)
            except StopIteration:
                pass
        chains = alive


def _inproj_kernel(x_ref, w_ref, g_ref, b_ref, *out_refs, apply_ln):
    x = x_ref[...]
    if apply_ln:
        x = _layernorm(x, g_ref[...], b_ref[...])
        out_refs[0][...] = x
    out_refs[-1][...] = _dot(x.astype(BF16), w_ref[...])


def _inproj(x, w_in_p, layer, g, b, apply_ln):
    n = x.shape[0]
    tm = min(TOKEN_TILE, n)
    grid = (n // tm,)
    const = lambda i: (0, 0)
    out_shape = [jax.ShapeDtypeStruct((n, U_COLS), F32)]
    out_specs = [pl.BlockSpec((tm, U_COLS), lambda i: (i, 0))]
    if apply_ln:
        out_shape = [jax.ShapeDtypeStruct((n, D_MODEL), F32)] + out_shape
        out_specs = [pl.BlockSpec((tm, D_MODEL), lambda i: (i, 0))] + out_specs
    return pl.pallas_call(
        functools.partial(_inproj_kernel, apply_ln=apply_ln),
        grid=grid,
        in_specs=[pl.BlockSpec((tm, D_MODEL), lambda i: (i, 0)),
                  pl.BlockSpec((None, D_MODEL, U_COLS), lambda i: (layer, 0, 0), pipeline_mode=pl.Buffered(1)),
                  pl.BlockSpec((1, D_MODEL), const),
                  pl.BlockSpec((1, D_MODEL), const)],
        out_specs=out_specs,
        out_shape=out_shape,
        compiler_params=pltpu.CompilerParams(dimension_semantics=("arbitrary",),
                                             vmem_limit_bytes=VMEM_LIMIT),
        name="inproj_ln" if apply_ln else "inproj",
    )(x, w_in_p, g, b)


def _outffn_kernel(mix_ref, x_ref, wo_ref, g1_ref, b1_ref, wu_ref, wd_ref, g2_ref, b2_ref, o_ref):
    m = _dot(mix_ref[...].astype(BF16), wo_ref[...])
    h1 = _layernorm(ALPHA * x_ref[...] + m, g1_ref[...], b1_ref[...])
    up = _dot(h1.astype(BF16), wu_ref[...])
    act = jnp.square(jnp.maximum(up, 0.0))
    ff = _dot(act.astype(BF16), wd_ref[...])
    o_ref[...] = _layernorm(ALPHA * h1 + ff, g2_ref[...], b2_ref[...])


def _outffn(mix, x, wo, g1, b1, wu, wd, g2, b2):
    n = x.shape[0]
    tm = min(TOKEN_TILE, n)
    const = lambda i: (0, 0)
    tile = pl.BlockSpec((tm, D_MODEL), lambda i: (i, 0))
    row = pl.BlockSpec((1, D_MODEL), const)
    return pl.pallas_call(
        _outffn_kernel,
        grid=(n // tm,),
        in_specs=[tile, tile,
                  pl.BlockSpec((D_MODEL, D_MODEL), const, pipeline_mode=pl.Buffered(1)),
                  row, row,
                  pl.BlockSpec((D_MODEL, D_FF), const, pipeline_mode=pl.Buffered(1)),
                  pl.BlockSpec((D_FF, D_MODEL), const, pipeline_mode=pl.Buffered(1)),
                  row, row],
        out_specs=tile,
        out_shape=jax.ShapeDtypeStruct((n, D_MODEL), F32),
        compiler_params=pltpu.CompilerParams(dimension_semantics=("arbitrary",),
                                             vmem_limit_bytes=VMEM_LIMIT),
        name="outffn",
    )(mix, x, wo, g1, b1, wu, wd, g2, b2)


def _mixer_kernel(u_ref, ca_in, sa_in, cb_in, sb_in, sc_in, dc_in, dn_in, dm_in,
                  caw_ref, cab_ref, cbw_ref, cbb_ref, gbias_ref, galog_ref,
                  nag_ref, nbg_ref, ncg_ref, ndg_ref, dskip_ref, lblog_ref,
                  y_ref, ca_out, sa_out, cb_out, sb_out, sc_out, dc_out, dn_out, dm_out,
                  ub, sa, sb, sc, dc, dn, dm,
                  *, L, T, NC, P, layer):
    c = pl.program_id(1)
    H, HD = N_HEADS, HEAD_DIM
    R0, R1 = ROW0, ROW0 + L
    padded = T < L
    a_cols = slice(OFF_A_QKV, OFF_A_QKV + CONV_A_CH)
    b_cols = slice(OFF_B_XBC, OFF_B_XBC + CONV_B_CH)

    @pl.when(c == 0)
    def _():
        ub[:, R0 - CONV_TAIL:R0, a_cols] = ca_in[...]
        ub[:, R0 - CONV_TAIL:R0, b_cols] = cb_in[...]
        sa[...] = sa_in[...]
        sb[...] = sb_in[...]
        sc[...] = sc_in[...]
        dc[...] = dc_in[...]
        dn[...] = dn_in[...]
        dm[...] = dm_in[...]

    ub[:, R0:R0 + T, :] = u_ref[...]
    if padded:
        ub[:, R0 + T:R1, :] = jnp.zeros((P, L - T, U_COLS), F32)

    ri = lax.broadcasted_iota(jnp.int32, (L, L), 0)
    ci = lax.broadcasted_iota(jnp.int32, (L, L), 1)
    incl = ri >= ci
    strict = ri > ci
    tril01 = jnp.where(incl, 1.0, 0.0).astype(BF16)
    eye = jnp.where(ri == ci, 1.0, 0.0)
    levels = _levels(L)

    def level_mask(b):
        sh = b.bit_length() - 1
        rb = ri >> sh
        cb = ci >> sh
        return (rb - cb == 1) & ((cb & 1) == 0)

    lmask = {b: level_mask(b) for b in levels}
    sel01 = {}
    for b in levels:
        sh = b.bit_length()
        sel01[b] = jnp.where(ci == (((ri >> sh) << sh) + (b - 1)), 1.0, 0.0).astype(BF16)
    rowv = lax.broadcasted_iota(jnp.int32, (L, 1), 0) < T
    lane = lax.broadcasted_iota(jnp.int32, (1, LANES), 1)
    is_dt = (lane >= GL_DECAY_A) & (lane < GL_I_D)
    is_i = (lane >= GL_I_D) & (lane < GL_F_D)
    is_f = (lane >= GL_F_D) & (lane < GL_F_D + H)

    pl_sm = lblog_ref[...]
    pl_sm = jnp.exp(pl_sm - jnp.max(pl_sm, axis=0, keepdims=True))
    pl_sm = pl_sm / jnp.sum(pl_sm, axis=0, keepdims=True)
    lb = pl_sm[0:1, :]
    for i in range(1, layer + 1):
        lb = lb + pl_sm[i:i + 1, :]
    lb = lb - pl_sm[0:1, :]

    m_last_out = {}

    def seq_chains(s):
        def seg(off, width):
            return ub[s, R0:R1, off:off + width]

        def conv(off, width, w_ref, b_ref):
            acc = b_ref[...] + ub[s, R0:R1, off:off + width] * w_ref[CONV_K - 1:CONV_K, :]
            for j in range(CONV_K - 1):
                lo = R0 - CONV_TAIL + j
                acc = acc + ub[s, lo:lo + L, off:off + width] * w_ref[j:j + 1, :]
            return acc

        def put_y(off, val):
            y_ref[s, :, off:off + val.shape[1]] = val[:T] if padded else val

        pre = seg(OFF_GATES, LANES) + gbias_ref[...]
        e = jnp.exp(-jnp.abs(pre))
        l1p = jnp.log1p(e)
        softplus = jnp.maximum(pre, 0.0) + l1p
        logsig = jnp.minimum(pre, 0.0) - l1p
        logdec = jnp.where(is_dt, -jnp.exp(galog_ref[...]) * softplus, jnp.where(is_f, logsig, 0.0))
        gval = jnp.where(lane < GL_DECAY_A, _sigmoid(pre), jnp.where(is_i, pre, softplus))
        if padded:
            logdec = jnp.where(rowv, logdec, 0.0)
            gval = jnp.where(rowv, gval, jnp.where(is_i, NEG, 0.0))
        gcum = _dot01(tril01, _split3(logdec))
        gt = jnp.where(is_i, gval, gcum).T

        def col(x, j):
            return x[:, j:j + 1]

        def decay_mat(j):
            return jnp.where(incl, jnp.exp(jnp.minimum(col(gcum, j) - gt[j:j + 1, :], 0.0)), 0.0)

        qkv = _silu(conv(OFF_A_QKV, CONV_A_CH, caw_ref, cab_ref))
        a_z = seg(OFF_A_Z, GROUP_WIDTH)

        def chain_a(h):
            sl = slice(h * HD, (h + 1) * HD)
            q = qkv[:, sl]
            k = qkv[:, GROUP_WIDTH + h * HD:GROUP_WIDTH + (h + 1) * HD]
            v = qkv[:, 2 * GROUP_WIDTH + h * HD:2 * GROUP_WIDTH + (h + 1) * HD]
            q = q * lax.rsqrt(jnp.sum(q * q, axis=-1, keepdims=True) + EPS) * (HD ** -0.5)
            k = k * lax.rsqrt(jnp.sum(k * k, axis=-1, keepdims=True) + EPS)
            g = col(gcum, GL_DECAY_A + h)
            beta = col(gval, GL_BETA + h)
            dmat = decay_mat(GL_DECAY_A + h)
            eg = jnp.exp(g)
            kb = k.astype(BF16)
            kk = _nt(kb, kb)
            qk = _nt(q, kb)
            yield
            nmat = jnp.where(strict, beta * kk * dmat, 0.0)
            tinv = eye - jnp.where(lmask[1], nmat, 0.0)
            for b in levels[1:]:
                t_hi = tinv.astype(BF16)
                o_hi, o_lo = _split2(jnp.where(lmask[b], nmat, 0.0))
                x = _dot(o_hi, t_hi) + _dot(o_lo, t_hi)
                yield
                x_hi, x_lo = _split2(x)
                tinv = tinv - (_dot(t_hi, x_hi) + _dot(t_hi, x_lo))
                yield
            r_hi, r_lo = _split2(jnp.concatenate([beta * v, (beta * eg) * k], axis=-1))
            t_hi = tinv.astype(BF16)
            sol = _dot(t_hi, r_hi) + _dot(t_hi, r_lo)
            s0 = sa[s, h]
            qs = _mm(q, s0)
            yield
            unew = sol[:, :HD] - _mm(sol[:, HD:], s0)
            yield
            o = eg * qs + _mm(qk * dmat, unew)
            g_last = g[L - 1:L, :]
            sa[s, h] = jnp.exp(g_last) * s0 + _tn(k * jnp.exp(g_last - g), unew)
            yield
            put_y(h * HD, _rms(o) * nag_ref[:, sl] * _silu(a_z[:, sl]))

        xbc = _silu(conv(OFF_B_XBC, CONV_B_CH, cbw_ref, cbb_ref))
        b_z = seg(OFF_B_Z, GROUP_WIDTH)

        def chain_b(grp):
            bmat = xbc[:, GROUP_WIDTH + grp * HD:GROUP_WIDTH + (grp + 1) * HD]
            cmat = xbc[:, GROUP_WIDTH + G_B * HD + grp * HD:GROUP_WIDTH + G_B * HD + (grp + 1) * HD]
            cb = _nt(cmat, bmat)
            yield
            outs = []
            for h in range(grp * (H // G_B), (grp + 1) * (H // G_B)):
                sl = slice(h * HD, (h + 1) * HD)
                g = col(gcum, GL_DT_B + h)
                dt = col(gval, GL_DT_B + h)
                xh = xbc[:, sl]
                v = xh * dt
                s0 = sb[s, h]
                o = jnp.exp(g) * _mm(cmat, s0) + _mm(cb * decay_mat(GL_DT_B + h), v)
                g_last = g[L - 1:L, :]
                sb[s, h] = jnp.exp(g_last) * s0 + _tn(bmat * jnp.exp(g_last - g), v)
                yield
                outs.append(o + dskip_ref[:, sl] * xh)
            gw = GROUP_WIDTH // G_B
            gsl = slice(grp * gw, (grp + 1) * gw)
            ob = jnp.concatenate(outs, axis=-1) * _silu(b_z[:, gsl])
            put_y(GROUP_WIDTH + grp * gw, _rms(ob) * nbg_ref[:, gsl])

        fgate = lb + (1.0 - lb) * _sigmoid(seg(OFF_C_F, GROUP_WIDTH))
        if padded:
            fgate = jnp.where(rowv, fgate, 1.0)
        gc_parts = _split3(_dot01(tril01, _split3(jnp.log(fgate))))
        gc = gc_parts[0].astype(F32) + gc_parts[1].astype(F32) + gc_parts[2].astype(F32)
        kc = 1.0 - fgate
        c_q = seg(OFF_C_Q, GROUP_WIDTH)
        c_v = seg(OFF_C_I, GROUP_WIDTH)
        c_g = seg(OFF_C_G, GROUP_WIDTH)
        pref = {b: _dot01(sel01[b], gc_parts) for b in levels}
        g_last_t = jnp.broadcast_to(gc[L - 1:L, :], (8, GROUP_WIDTH)).T

        def chain_c(h):
            sl = slice(h * HD, (h + 1) * HD)
            q, k, v, g = c_q[:, sl], kc[:, sl], c_v[:, sl], gc[:, sl]
            att = None
            for b in levels:
                p = pref[b][:, sl]
                part = jnp.where(lmask[b],
                                 _nt(q * jnp.exp(jnp.minimum(g - p, 0.0)), k * jnp.exp(jnp.minimum(p - g, 0.0))),
                                 0.0)
                att = part if att is None else att + part
                yield
            s0 = sc[s, h]
            o = _mm(q * jnp.exp(g), s0) + _mm(att, v) + jnp.sum(q * k, axis=-1, keepdims=True) * v
            g_last = g[L - 1:L, :]
            sc[s, h] = jnp.exp(g_last_t[sl, 0:1]) * s0 + _tn(k * jnp.exp(g_last - g), v)
            yield
            put_y(2 * GROUP_WIDTH + h * HD, _rms(o) * ncg_ref[:, sl] * _sigmoid(c_g[:, sl]))

        d_q = seg(OFF_D_Q, GROUP_WIDTH)
        d_k = seg(OFF_D_K, GROUP_WIDTH)
        d_v = seg(OFF_D_V, GROUP_WIDTH)
        d_o = seg(OFF_D_O, GROUP_WIDTH)
        m_row = dm[s]

        def chain_d(h):
            sl = slice(h * HD, (h + 1) * HD)
            q, k, v = d_q[:, sl], d_k[:, sl] * (HD ** -0.5), d_v[:, sl]
            bcum = col(gcum, GL_F_D + h)
            m0 = m_row[:, h:h + 1]
            dmat = jnp.where(incl, bcum - gt[GL_F_D + h:GL_F_D + h + 1, :] + gt[GL_I_D + h:GL_I_D + h + 1, :], NEG)
            init = bcum + m0
            m_r = jnp.maximum(init, jnp.max(dmat, axis=-1, keepdims=True))
            qk = _nt(q, k)
            cmem = dc[s, h]
            qc = _mm(q, cmem)
            yield
            p = jnp.where(incl, jnp.exp(dmat - m_r), 0.0) * qk
            s_init = jnp.exp(init - m_r)
            nvec = dn[s, h:h + 1, :]
            num = s_init * qc + _mm(p, v)
            den = s_init * jnp.sum(q * nvec, axis=-1, keepdims=True) + jnp.sum(p, axis=-1, keepdims=True)
            m_last = m_r[L - 1:L, :]
            b_last = bcum[L - 1:L, :]
            scale = jnp.exp(b_last + m0 - m_last)
            kw = k * jnp.exp(b_last - bcum + col(gval, GL_I_D + h) - m_last)
            dc[s, h] = scale * cmem + _tn(kw, v)
            dn[s, h:h + 1, :] = scale * nvec + jnp.sum(kw, axis=0, keepdims=True)
            m_last_out[(s, h)] = m_last
            yield
            hh = num / jnp.maximum(jnp.abs(den), jnp.exp(-m_r))
            put_y(3 * GROUP_WIDTH + h * HD, _rms(hh) * ndg_ref[:, sl] * _sigmoid(d_o[:, sl]))

        return ([chain_a(h) for h in range(H)] + [chain_b(g) for g in range(G_B)]
                + [chain_c(h) for h in range(H)] + [chain_d(h) for h in range(H)])

    chains = []
    for s in range(P):
        chains += seq_chains(s)
    _run_interleaved(chains)

    for s in range(P):
        m_new = dm[s]
        for h in range(H):
            m_new = jnp.where(lane == h, m_last_out[(s, h)], m_new)
        dm[s] = m_new

    tail_a = ub[:, R0 + T - CONV_TAIL:R0 + T, a_cols]
    tail_b = ub[:, R0 + T - CONV_TAIL:R0 + T, b_cols]
    ub[:, R0 - CONV_TAIL:R0, a_cols] = tail_a
    ub[:, R0 - CONV_TAIL:R0, b_cols] = tail_b

    @pl.when(c == NC - 1)
    def _():
        ca_out[...] = tail_a
        cb_out[...] = tail_b
        sa_out[...] = sa[...]
        sb_out[...] = sb[...]
        sc_out[...] = sc[...]
        dc_out[...] = dc[...]
        dn_out[...] = dn[...]
        dm_out[...] = dm[...]


def _mixer_pair_kernel(u_ref, ca_in, sa_in, cb_in, sb_in, sc_in, dc_in, dn_in, dm_in,
                       caw_ref, cab_ref, cbw_ref, cbb_ref, gbias_ref, galog_ref,
                       nag_ref, nbg_ref, ncg_ref, ndg_ref, dskip_ref, lblog_ref,
                       y_ref, ca_out, sa_out, cb_out, sb_out, sc_out, dc_out, dn_out, dm_out,
                       cbuf, sa, sb, sc, dc, dn, dm,
                       *, NC, P, layer):
    c = pl.program_id(1)
    H, HD, L, W = N_HEADS, HEAD_DIM, CHUNK, LANES
    NP = H // 2
    R0, R1 = ROW0, ROW0 + L
    CB_A, CB_B = 0, CONV_A_CH
    f32 = F32

    def iota(shape, d):
        return lax.broadcasted_iota(jnp.int32, shape, d)

    def one_bf16(mask):
        return jnp.where(mask, 1.0, 0.0).astype(BF16)

    r_ll, c_ll = iota((L, L), 0), iota((L, L), 1)
    tril01 = one_bf16(r_ll >= c_ll)
    ones_ll = jnp.ones((L, L), BF16)
    r_lw, c_lw = iota((L, W), 0), iota((L, W), 1)
    j_lw = c_lw & (HD - 1)
    eyepair = jnp.where(r_lw == j_lw, 1.0, 0.0)
    incl_p = r_lw >= j_lw
    strict_p = r_lw > j_lw
    levels = _levels(L)
    lmask = {}
    sel01 = {}
    for b in levels:
        sh = b.bit_length() - 1
        rb, cb = r_lw >> sh, j_lw >> sh
        lmask[b] = (rb - cb == 1) & ((cb & 1) == 0)
        sel01[b] = one_bf16(c_ll == (((r_ll >> (sh + 1)) << (sh + 1)) + (b - 1)))
    r_ww, c_ww = iota((W, W), 0), iota((W, W), 1)
    bdiag = (r_ww >> 6) == (c_ww >> 6)
    ones_bd = one_bf16(bdiag)
    ones_ww = jnp.ones((W, W), BF16)
    eye_ww = one_bf16(r_ww == c_ww)
    lane = iota((1, W), 1)
    lo_half = lane < HD
    is_dt = (lane >= GL_DECAY_A) & (lane < GL_I_D)
    is_i = (lane >= GL_I_D) & (lane < GL_F_D)
    is_f = (lane >= GL_F_D) & (lane < GL_F_D + H)
    is_cum = is_dt | is_f
    exp_bases = (GL_DECAY_A, GL_BETA, GL_DT_B, GL_DT_B2, GL_F_D, GL_I_D)
    exp01 = jnp.concatenate([one_bf16(r_ww == (base + 2 * p + (c_ww >> 6)))
                             for base in exp_bases for p in range(NP)], axis=1)

    def bd(x):
        return jnp.where(bdiag, jnp.concatenate([x, x], axis=0), 0.0)

    def dot2(x, m01):
        hi, lo = _split2(x)
        return _dot(hi, m01) + _dot(lo, m01)

    def half_sums(x):
        return dot2(x, ones_bd)

    def row_form(x):
        return _dot01(ones_ll, _split3(x * eyepair))

    def transpose_bf16(x):
        return _nt(eye_ww, x)

    pl_sm = lblog_ref[...]
    pl_sm = jnp.exp(pl_sm - jnp.max(pl_sm, axis=0, keepdims=True))
    pl_sm = pl_sm / jnp.sum(pl_sm, axis=0, keepdims=True)
    lb = pl_sm[0:1, :]
    for i in range(1, layer + 1):
        lb = lb + pl_sm[i:i + 1, :]
    lb = lb - pl_sm[0:1, :]

    @pl.when(c == 0)
    def _():
        cbuf[:, R0 - CONV_TAIL:R0, CB_A:CB_A + CONV_A_CH] = ca_in[...]
        cbuf[:, R0 - CONV_TAIL:R0, CB_B:CB_B + CONV_B_CH] = cb_in[...]
        zero = jnp.zeros((W, W), f32)
        for s in range(P):
            m_row = dm_in[s]
            for p in range(NP):
                h0, h1 = 2 * p, 2 * p + 1
                for ref, src in ((sa, sa_in), (sc, sc_in), (dc, dc_in)):
                    ref[s, p] = zero
                    ref[s, p, 0:HD, 0:HD] = src[s, h0]
                    ref[s, p, HD:W, HD:W] = src[s, h1]
                sb[s, p] = zero
                sb[s, p, p * HD:(p + 1) * HD, 0:HD] = sb_in[s, h0]
                sb[s, p, p * HD:(p + 1) * HD, HD:W] = sb_in[s, h1]
                dn[s, p] = jnp.concatenate([dn_in[s, h0:h0 + 1, :], dn_in[s, h1:h1 + 1, :]], axis=1)
                dm[s, p] = jnp.where(lo_half, m_row[:, h0:h0 + 1], m_row[:, h1:h1 + 1])

    cbuf[:, R0:R1, CB_A:CB_A + CONV_A_CH] = u_ref[:, :, OFF_A_QKV:OFF_A_QKV + CONV_A_CH]
    cbuf[:, R0:R1, CB_B:CB_B + CONV_B_CH] = u_ref[:, :, OFF_B_XBC:OFF_B_XBC + CONV_B_CH]

    def seq_chains(s):
        def seg(off, width):
            return u_ref[s, :, off:off + width]

        def conv(off, width, w_ref, b_ref):
            acc = b_ref[...] + cbuf[s, R0:R1, off:off + width] * w_ref[CONV_K - 1:CONV_K, :]
            for j in range(CONV_K - 1):
                lo = R0 - CONV_TAIL + j
                acc = acc + cbuf[s, lo:lo + L, off:off + width] * w_ref[j:j + 1, :]
            return acc

        pre = seg(OFF_GATES, W) + gbias_ref[...]
        e = jnp.exp(-jnp.abs(pre))
        l1p = jnp.log1p(e)
        softplus = jnp.maximum(pre, 0.0) + l1p
        logsig = jnp.minimum(pre, 0.0) - l1p
        logdec = jnp.where(is_dt, -jnp.exp(galog_ref[...]) * softplus, jnp.where(is_f, logsig, 0.0))
        gval = jnp.where(lane < GL_DECAY_A, _sigmoid(pre), jnp.where(is_i, pre, softplus))
        gcum = _dot01(tril01, _split3(logdec))
        src = _split3(jnp.where(is_cum, gcum, gval))
        expd = _dot(src[0], exp01) + _dot(src[1], exp01) + _dot(src[2], exp01)

        def expanded(k, p):
            o = (k * NP + p) * W
            return expd[:, o:o + W]

        qkv = _silu(conv(CB_A, CONV_A_CH, caw_ref, cab_ref))

        def chain_a(p):
            ps = slice(p * W, (p + 1) * W)
            q = qkv[:, p * W:(p + 1) * W]
            k = qkv[:, GROUP_WIDTH + p * W:GROUP_WIDTH + (p + 1) * W]
            v = qkv[:, 2 * GROUP_WIDTH + p * W:2 * GROUP_WIDTH + (p + 1) * W]
            ssq = half_sums(q * q)
            ssk = half_sums(k * k)
            g = expanded(0, p)
            beta = expanded(1, p)
            grow = row_form(g)
            yield
            q = q * lax.rsqrt(ssq + EPS) * (HD ** -0.5)
            k = k * lax.rsqrt(ssk + EPS)
            eg = jnp.exp(g)
            g_last = g[L - 1:L, :]
            kbd = bd(k).astype(BF16)
            kk = _nt(k, kbd)
            qk = _nt(q, kbd)
            kw_t = transpose_bf16(k * jnp.exp(g_last - g))
            dmat = jnp.where(incl_p, jnp.exp(jnp.minimum(g - grow, 0.0)), 0.0)
            yield
            nmat = jnp.where(strict_p, beta * kk * dmat, 0.0)
            tinv = eyepair - jnp.where(lmask[1], nmat, 0.0)
            for b in levels[1:]:
                t_bd = bd(tinv).astype(BF16)
                x = dot2(jnp.where(lmask[b], nmat, 0.0), t_bd)
                yield
                x_hi, x_lo = _split2(bd(x))
                t_hi = tinv.astype(BF16)
                tinv = tinv - (_dot(t_hi, x_hi) + _dot(t_hi, x_lo))
                yield
            r_hi, r_lo = _split2(jnp.concatenate([bd(beta * v), bd((beta * eg) * k)], axis=1))
            t_hi = tinv.astype(BF16)
            sol = _dot(t_hi, r_hi) + _dot(t_hi, r_lo)
            s0 = sa[s, p]
            qs = _mm(q, s0)
            yield
            unew = sol[:, :W] - _mm(sol[:, W:], s0)
            yield
            o = eg * qs + _mm(qk * dmat, bd(unew))
            sa[s, p] = jnp.exp(g_last) * s0 + jnp.where(bdiag, _mm(kw_t, unew), 0.0)
            ms = half_sums(o * o) * (1.0 / HD)
            yield
            a_z = seg(OFF_A_Z + p * W, W)
            y_ref[s, :, ps] = o * lax.rsqrt(ms + EPS) * nag_ref[:, ps] * _silu(a_z)

        xbc = _silu(conv(CB_B, CONV_B_CH, cbw_ref, cbb_ref))
        b_all = xbc[:, GROUP_WIDTH:GROUP_WIDTH + W]
        c_all = xbc[:, GROUP_WIDTH + W:GROUP_WIDTH + 2 * W]
        b_t = transpose_bf16(b_all)

        def chain_b(p):
            ps = slice(p * W, (p + 1) * W)
            in_grp = (lane >> 6) == p
            b_grp = jnp.where(in_grp, b_all, 0.0)
            cb = _nt(c_all, jnp.concatenate([b_grp, b_grp], axis=0))
            g = expanded(2, p)
            dt = expanded(3, p)
            grow = row_form(g)
            yield
            g_last = g[L - 1:L, :]
            xh = xbc[:, ps]
            v = xh * dt
            s0 = sb[s, p]
            dmat = jnp.where(incl_p, jnp.exp(jnp.minimum(g - grow, 0.0)), 0.0)
            o = jnp.exp(g) * _mm(c_all, s0) + _mm(cb * dmat, bd(v))
            upd = _dot(b_t, (v * jnp.exp(g_last - g)).astype(BF16))
            sb[s, p] = jnp.exp(g_last) * s0 + jnp.where((r_ww >> 6) == p, upd, 0.0)
            yield
            ob = (o + dskip_ref[:, ps] * xh) * _silu(seg(OFF_B_Z + p * W, W))
            ms = dot2(ob * ob, ones_ww) * (1.0 / W)
            yield
            y_ref[s, :, GROUP_WIDTH + p * W:GROUP_WIDTH + (p + 1) * W] = ob * lax.rsqrt(ms + EPS) * nbg_ref[:, ps]

        fgate = lb + (1.0 - lb) * _sigmoid(seg(OFF_C_F, GROUP_WIDTH))
        gc_parts = _split3(_dot01(tril01, _split3(jnp.log(fgate))))
        gc = gc_parts[0].astype(f32) + gc_parts[1].astype(f32) + gc_parts[2].astype(f32)
        kc = 1.0 - fgate
        pref = {b: _dot01(sel01[b], gc_parts) for b in levels}

        def chain_c(p):
            ps = slice(p * W, (p + 1) * W)
            q = seg(OFF_C_Q + p * W, W)
            v = seg(OFF_C_I + p * W, W)
            k, g = kc[:, ps], gc[:, ps]
            g_last = g[L - 1:L, :]
            kw_t = transpose_bf16(k * jnp.exp(g_last - g))
            e_parts = _split3(jnp.where(r_ww == c_ww, jnp.broadcast_to(g_last, (W, W)), 0.0))
            e_col = jnp.exp(_dot(e_parts[0], ones_ww) + _dot(e_parts[1], ones_ww) + _dot(e_parts[2], ones_ww))
            qk_diag = half_sums(q * k)
            yield
            att = None
            for b in levels:
                pb = pref[b][:, ps]
                kd = bd(k * jnp.exp(jnp.minimum(pb - g, 0.0))).astype(BF16)
                part = jnp.where(lmask[b], _nt(q * jnp.exp(jnp.minimum(g - pb, 0.0)), kd), 0.0)
                att = part if att is None else att + part
                yield
            s0 = sc[s, p]
            o = _mm(q * jnp.exp(g), s0) + _mm(att, bd(v)) + qk_diag * v
            sc[s, p] = e_col * s0 + jnp.where(bdiag, _mm(kw_t, v), 0.0)
            ms = half_sums(o * o) * (1.0 / HD)
            yield
            c_g = seg(OFF_C_G + p * W, W)
            y_ref[s, :, 2 * GROUP_WIDTH + p * W:2 * GROUP_WIDTH + (p + 1) * W] = (
                o * lax.rsqrt(ms + EPS) * ncg_ref[:, ps] * _sigmoid(c_g))

        def chain_d(p):
            ps = slice(p * W, (p + 1) * W)
            q = seg(OFF_D_Q + p * W, W)
            k = seg(OFF_D_K + p * W, W) * (HD ** -0.5)
            v = seg(OFF_D_V + p * W, W)
            bcum = expanded(4, p)
            ipre = expanded(5, p)
            m0 = dm[s, p]
            a = ipre - bcum
            arow = row_form(a)
            qk = _nt(q, bd(k).astype(BF16))
            cmem = dc[s, p]
            qc = _mm(q, cmem)
            nvec = dn[s, p]
            qn = half_sums(q * nvec)
            cm = a
            sh = 1
            while sh < L:
                cm = jnp.maximum(cm, jnp.where(r_lw >= sh, pltpu.roll(cm, sh, axis=0), NEG))
                sh *= 2
            yield
            m_r = bcum + jnp.maximum(m0, cm)
            pmat = jnp.where(incl_p, jnp.exp(bcum + arow - m_r), 0.0) * qk
            s_init = jnp.exp(bcum + m0 - m_r)
            num = s_init * qc + _mm(pmat, bd(v))
            den = s_init * qn + half_sums(pmat)
            m_last = m_r[L - 1:L, :]
            b_last = bcum[L - 1:L, :]
            scale = jnp.exp(b_last + m0 - m_last)
            kw = k * jnp.exp(b_last - bcum + ipre - m_last)
            kw_t = transpose_bf16(kw)
            yield
            dc[s, p] = scale * cmem + jnp.where(bdiag, _mm(kw_t, v), 0.0)
            dn[s, p] = scale * nvec + jnp.sum(kw, axis=0, keepdims=True)
            dm[s, p] = m_last
            hh = num / jnp.maximum(jnp.abs(den), jnp.exp(-m_r))
            ms = half_sums(hh * hh) * (1.0 / HD)
            yield
            d_o = seg(OFF_D_O + p * W, W)
            y_ref[s, :, 3 * GROUP_WIDTH + p * W:3 * GROUP_WIDTH + (p + 1) * W] = (
                hh * lax.rsqrt(ms + EPS) * ndg_ref[:, ps] * _sigmoid(d_o))

        return ([chain_a(p) for p in range(NP)] + [chain_c(p) for p in range(NP)]
                + [chain_b(p) for p in range(NP)] + [chain_d(p) for p in range(NP)])

    chains = []
    for s in range(P):
        chains += seq_chains(s)
    _run_interleaved(chains)

    tail_a = cbuf[:, R1 - CONV_TAIL:R1, CB_A:CB_A + CONV_A_CH]
    tail_b = cbuf[:, R1 - CONV_TAIL:R1, CB_B:CB_B + CONV_B_CH]
    cbuf[:, R0 - CONV_TAIL:R0, CB_A:CB_A + CONV_A_CH] = tail_a
    cbuf[:, R0 - CONV_TAIL:R0, CB_B:CB_B + CONV_B_CH] = tail_b

    @pl.when(c == NC - 1)
    def _():
        ca_out[...] = tail_a
        cb_out[...] = tail_b
        for s in range(P):
            m_row = jnp.zeros((1, W), f32)
            for p in range(NP):
                h0, h1 = 2 * p, 2 * p + 1
                for ref, dst in ((sa, sa_out), (sc, sc_out), (dc, dc_out)):
                    dst[s, h0] = ref[s, p, 0:HD, 0:HD]
                    dst[s, h1] = ref[s, p, HD:W, HD:W]
                sb_out[s, h0] = sb[s, p, p * HD:(p + 1) * HD, 0:HD]
                sb_out[s, h1] = sb[s, p, p * HD:(p + 1) * HD, HD:W]
                nrow = dn[s, p]
                dn_out[s, h0:h0 + 1, :] = nrow[:, 0:HD]
                dn_out[s, h1:h1 + 1, :] = nrow[:, HD:W]
                mp = dm[s, p]
                m_row = jnp.where(lane == h0, mp[:, 0:1], jnp.where(lane == h1, mp[:, HD:HD + 1], m_row))
            dm_out[s] = m_row


def _mixers(u, states, params, *, layer, T, L, P):
    bsz, ttot, _ = u.shape
    nc = ttot // T
    H, HD = N_HEADS, HEAD_DIM
    ca, s_a, cb, s_b, s_c, d_c, d_n, d_m = states
    d_m = jnp.pad(d_m, ((0, 0), (0, 0), (0, LANES - H))).reshape(d_m.shape[0], bsz, 1, LANES)

    def per_seq(shape):
        nd = len(shape)
        return pl.BlockSpec((P,) + tuple(shape), lambda b, c: (b,) + (0,) * nd)

    def per_seq_in(x, shape):
        nd = len(shape)
        lidx = layer if x.shape[0] > 1 else 0
        return pl.BlockSpec((None, P) + tuple(shape), lambda b, c: (lidx, b) + (0,) * nd)

    def whole(x):
        nd = x.ndim
        return pl.BlockSpec(x.shape, lambda b, c: (0,) * nd)

    state_dims = [(CONV_TAIL, CONV_A_CH), (H, HD, HD), (CONV_TAIL, CONV_B_CH), (H, HD, HD), (H, HD, HD),
                  (H, HD, HD), (H, HD), (1, LANES)]
    state_specs = [per_seq(d) for d in state_dims]
    state_shapes = [jax.ShapeDtypeStruct((bsz,) + d, F32) for d in state_dims]
    if T == CHUNK and L == CHUNK:
        body = functools.partial(_mixer_pair_kernel, NC=nc, P=P, layer=layer)
        pair_state = (P, H // 2, LANES, LANES)
        scratch = ([pltpu.VMEM((P, ROW0 + L, CONV_A_CH + CONV_B_CH), F32)]
                   + [pltpu.VMEM(pair_state, F32) for _ in range(4)]
                   + [pltpu.VMEM((P, H // 2, 1, LANES), F32) for _ in range(2)])
    else:
        body = functools.partial(_mixer_kernel, L=L, T=T, NC=nc, P=P, layer=layer)
        scratch = ([pltpu.VMEM((P, ROW0 + L, U_COLS), F32)]
                   + [pltpu.VMEM((P,) + d, F32) for d in state_dims[1:2] + state_dims[3:]])
    outs = pl.pallas_call(
        body,
        grid=(bsz // P, nc),
        in_specs=[pl.BlockSpec((P, T, U_COLS), lambda b, c: (b, c, 0))]
                 + [per_seq_in(x, d) for x, d in zip((ca, s_a, cb, s_b, s_c, d_c, d_n, d_m), state_dims)]
                 + [whole(p) for p in params],
        out_specs=[pl.BlockSpec((P, T, D_MODEL), lambda b, c: (b, c, 0))] + state_specs,
        out_shape=[jax.ShapeDtypeStruct((bsz, ttot, D_MODEL), F32)] + state_shapes,
        scratch_shapes=scratch,
        compiler_params=pltpu.CompilerParams(dimension_semantics=("arbitrary", "arbitrary"),
                                             vmem_limit_bytes=VMEM_LIMIT),
        name="mixers_l%d_t%d" % (layer, T),
    )(u, ca, s_a, cb, s_b, s_c, d_c, d_n, d_m, *params)
    y, new = outs[0], list(outs[1:])
    new[7] = new[7].reshape(bsz, LANES)[:, :H]
    return y, tuple(new)


DEC_GATE_ROWS = 128


def _sample_pre_kernel(u_ref, ca_ref, cb_ref, caw_ref, cab_ref, cbw_ref, cbb_ref, gbias_ref, galog_ref, lblog_ref,
                       at_ref, bt_ref, ct_ref, dt_ref, gt_ref, ca_out, cb_out, hist, *, layer, T):
    t = pl.program_id(0)
    na = CONV_A_CH

    @pl.when(t == 0)
    def _():
        hist[0:CONV_TAIL, :, 0:na] = ca_ref[...]
        hist[0:CONV_TAIL, :, na:] = cb_ref[...]

    hist[CONV_TAIL + t, :, 0:na] = u_ref[:, OFF_A_QKV:OFF_A_QKV + CONV_A_CH]
    hist[CONV_TAIL + t, :, na:] = u_ref[:, OFF_B_XBC:OFF_B_XBC + CONV_B_CH]
    w_all = jnp.concatenate([caw_ref[...], cbw_ref[...]], axis=1)
    acc = jnp.concatenate([cab_ref[...], cbb_ref[...]], axis=1)
    for j in range(CONV_K):
        acc = acc + hist[t + j] * w_all[j:j + 1, :]
    act = _silu(acc)
    at_ref[...] = act[:, 0:na].T
    bt_ref[...] = act[:, na:].T

    lane = lax.broadcasted_iota(jnp.int32, (1, LANES), 1)
    pre = u_ref[:, OFF_GATES:OFF_GATES + LANES] + gbias_ref[...]
    e = jnp.exp(-jnp.abs(pre))
    l1p = jnp.log1p(e)
    softplus = jnp.maximum(pre, 0.0) + l1p
    logsig = jnp.minimum(pre, 0.0) - l1p
    is_dt = (lane >= GL_DECAY_A) & (lane < GL_I_D)
    is_i = (lane >= GL_I_D) & (lane < GL_F_D)
    is_f = (lane >= GL_F_D) & (lane < GL_F_D + N_HEADS)
    decay = jnp.exp(-jnp.exp(galog_ref[...]) * softplus)
    gates = jnp.where(lane < GL_DECAY_A, _sigmoid(pre),
                      jnp.where(is_dt, decay, jnp.where(is_i, pre, jnp.where(is_f, logsig, softplus))))
    gt_ref[...] = gates.T

    pl_sm = lblog_ref[...]
    pl_sm = jnp.exp(pl_sm - jnp.max(pl_sm, axis=0, keepdims=True))
    pl_sm = pl_sm / jnp.sum(pl_sm, axis=0, keepdims=True)
    lb = pl_sm[0:1, :]
    for i in range(1, layer + 1):
        lb = lb + pl_sm[i:i + 1, :]
    lb = lb - pl_sm[0:1, :]
    fgate = lb + (1.0 - lb) * _sigmoid(u_ref[:, OFF_C_F:OFF_C_F + GROUP_WIDTH])
    gw = GROUP_WIDTH
    ct_ref[0:gw, :] = u_ref[:, OFF_C_Q:OFF_C_Q + gw].T
    ct_ref[gw:2 * gw, :] = fgate.T
    ct_ref[2 * gw:3 * gw, :] = u_ref[:, OFF_C_I:OFF_C_I + gw].T
    dt_ref[0:gw, :] = u_ref[:, OFF_D_Q:OFF_D_Q + gw].T
    dt_ref[gw:2 * gw, :] = (u_ref[:, OFF_D_K:OFF_D_K + gw] * (HEAD_DIM ** -0.5)).T
    dt_ref[2 * gw:3 * gw, :] = u_ref[:, OFF_D_V:OFF_D_V + gw].T

    @pl.when(t == T - 1)
    def _():
        ca_out[...] = hist[T:T + CONV_TAIL, :, 0:na]
        cb_out[...] = hist[T:T + CONV_TAIL, :, na:]


def _colsum(x):
    return jnp.sum(x, axis=0, keepdims=True)


def _dec_a_kernel(q_ref, k_ref, v_ref, a_ref, beta_ref, s_in, s_out, o_ref, *, T):
    HD, B = HEAD_DIM, LANES
    for t in range(T):
        ln = slice(t * B, (t + 1) * B)
        q, k, v = q_ref[:, ln], k_ref[:, ln], v_ref[:, ln]
        q = q * (lax.rsqrt(_colsum(q * q) + EPS) * (HD ** -0.5))
        k = k * lax.rsqrt(_colsum(k * k) + EPS)
        a, beta = a_ref[:, ln], beta_ref[:, ln]
        src = s_in if t == 0 else s_out
        ks = [jnp.zeros((HD, B), F32), jnp.zeros((HD, B), F32)]
        for dk in range(HD):
            ks[dk & 1] = ks[dk & 1] + k[dk:dk + 1, :] * src[dk]
        unew = beta * (v - a * (ks[0] + ks[1]))
        o = [jnp.zeros((HD, B), F32), jnp.zeros((HD, B), F32)]
        for dk in range(HD):
            sn = a * src[dk] + k[dk:dk + 1, :] * unew
            s_out[dk] = sn
            o[dk & 1] = o[dk & 1] + q[dk:dk + 1, :] * sn
        ot = o[0] + o[1]
        o_ref[:, ln] = ot * lax.rsqrt(_colsum(ot * ot) * (1.0 / HD) + EPS)


def _dec_b_kernel(x_ref, b_ref, c_ref, a_ref, dt_ref, skip_ref, s_in, s_out, o_ref, *, T):
    HD, B = HEAD_DIM, LANES
    for t in range(T):
        ln = slice(t * B, (t + 1) * B)
        x, bm, cm = x_ref[:, ln], b_ref[:, ln], c_ref[:, ln]
        a = a_ref[:, ln]
        xdt = x * dt_ref[:, ln]
        src = s_in if t == 0 else s_out
        o = [jnp.zeros((HD, B), F32), jnp.zeros((HD, B), F32)]
        for n in range(HD):
            sn = a * src[n] + bm[n:n + 1, :] * xdt
            s_out[n] = sn
            o[n & 1] = o[n & 1] + cm[n:n + 1, :] * sn
        o_ref[:, ln] = o[0] + o[1] + skip_ref[...] * x


def _dec_c_kernel(q_ref, f_ref, v_ref, s_in, s_out, o_ref, *, T):
    HD, B = HEAD_DIM, LANES
    for t in range(T):
        ln = slice(t * B, (t + 1) * B)
        q, f, v = q_ref[:, ln], f_ref[:, ln], v_ref[:, ln]
        k = 1.0 - f
        src = s_in if t == 0 else s_out
        o = [jnp.zeros((HD, B), F32), jnp.zeros((HD, B), F32)]
        for dk in range(HD):
            sn = f[dk:dk + 1, :] * src[dk] + k[dk:dk + 1, :] * v
            s_out[dk] = sn
            o[dk & 1] = o[dk & 1] + q[dk:dk + 1, :] * sn
        ot = o[0] + o[1]
        o_ref[:, ln] = ot * lax.rsqrt(_colsum(ot * ot) * (1.0 / HD) + EPS)


def _dec_d_kernel(q_ref, k_ref, v_ref, i_ref, f_ref, c_in, n_in, m_in, c_out, n_out, m_out, o_ref, *, T):
    HD, B = HEAD_DIM, LANES
    m = m_in[...]
    nvec = n_in[...]
    for t in range(T):
        ln = slice(t * B, (t + 1) * B)
        q, k, v = q_ref[:, ln], k_ref[:, ln], v_ref[:, ln]
        ipre, logf = i_ref[:, ln], f_ref[:, ln]
        m_new = jnp.maximum(logf + m, ipre)
        fs = jnp.exp(logf + m - m_new)
        kw = k * jnp.exp(ipre - m_new)
        src = c_in if t == 0 else c_out
        num = [jnp.zeros((HD, B), F32), jnp.zeros((HD, B), F32)]
        for dk in range(HD):
            cn = fs * src[dk] + kw[dk:dk + 1, :] * v
            c_out[dk] = cn
            num[dk & 1] = num[dk & 1] + q[dk:dk + 1, :] * cn
        nvec = fs * nvec + kw
        den = _colsum(q * nvec)
        hh = (num[0] + num[1]) / jnp.maximum(jnp.abs(den), jnp.exp(-m_new))
        o_ref[:, ln] = hh * lax.rsqrt(_colsum(hh * hh) * (1.0 / HD) + EPS)
        m = m_new
    n_out[...] = nvec
    m_out[...] = m


def _sample_post_kernel(oa_ref, ob_ref, oc_ref, od_ref, u_ref, nag_ref, nbg_ref, ncg_ref, ndg_ref, y_ref):
    gw = GROUP_WIDTH
    y_ref[:, 0:gw] = oa_ref[...].T * nag_ref[...] * _silu(u_ref[:, OFF_A_Z:OFF_A_Z + gw])
    ob = ob_ref[...].T * _silu(u_ref[:, OFF_B_Z:OFF_B_Z + gw])
    grp = gw // G_B
    for g in range(G_B):
        sl = slice(g * grp, (g + 1) * grp)
        y_ref[:, gw + g * grp:gw + (g + 1) * grp] = _rms(ob[:, sl]) * nbg_ref[:, sl]
    y_ref[:, 2 * gw:3 * gw] = oc_ref[...].T * ncg_ref[...] * _sigmoid(u_ref[:, OFF_C_G:OFF_C_G + gw])
    y_ref[:, 3 * gw:4 * gw] = od_ref[...].T * ndg_ref[...] * _sigmoid(u_ref[:, OFF_D_O:OFF_D_O + gw])


def _sample_mixers(u, states_t, params, *, layer, T):
    B, H, HD, GW = LANES, N_HEADS, HEAD_DIM, GROUP_WIDTH
    ca_t, sa_t, cb_t, sb_t, sc_t, dc_t, dn_t, dm_t = states_t
    (caw, cab, cbw, cbb, gbias, galog, nag, nbg, ncg, ndg, dskip, lblog) = params
    cp1 = pltpu.CompilerParams(dimension_semantics=("arbitrary",), vmem_limit_bytes=VMEM_LIMIT)
    n = T * B

    def whole(x):
        nd = x.ndim
        return pl.BlockSpec(x.shape, lambda i: (0,) * nd)

    def lane_blk(rows):
        return pl.BlockSpec((rows, B), lambda t: (0, t))

    a_t, b_t, c_t, d_t, g_t, ca_new, cb_new = pl.pallas_call(
        functools.partial(_sample_pre_kernel, layer=layer, T=T),
        grid=(T,),
        in_specs=[pl.BlockSpec((B, U_COLS), lambda t: (t, 0)),
                  pl.BlockSpec((None, CONV_TAIL, B, CONV_A_CH), lambda t: (layer, 0, 0, 0)),
                  pl.BlockSpec((None, CONV_TAIL, B, CONV_B_CH), lambda t: (layer, 0, 0, 0))]
                 + [whole(p) for p in (caw, cab, cbw, cbb, gbias, galog, lblog)],
        out_specs=[lane_blk(CONV_A_CH), lane_blk(CONV_B_CH), lane_blk(3 * GW), lane_blk(3 * GW),
                   lane_blk(DEC_GATE_ROWS),
                   pl.BlockSpec((CONV_TAIL, B, CONV_A_CH), lambda t: (0, 0, 0)),
                   pl.BlockSpec((CONV_TAIL, B, CONV_B_CH), lambda t: (0, 0, 0))],
        out_shape=[jax.ShapeDtypeStruct((CONV_A_CH, n), F32), jax.ShapeDtypeStruct((CONV_B_CH, n), F32),
                   jax.ShapeDtypeStruct((3 * GW, n), F32), jax.ShapeDtypeStruct((3 * GW, n), F32),
                   jax.ShapeDtypeStruct((DEC_GATE_ROWS, n), F32),
                   jax.ShapeDtypeStruct((CONV_TAIL, B, CONV_A_CH), F32),
                   jax.ShapeDtypeStruct((CONV_TAIL, B, CONV_B_CH), F32)],
        scratch_shapes=[pltpu.VMEM((CONV_TAIL + T, B, CONV_A_CH + CONV_B_CH), F32)],
        compiler_params=cp1, name="sample_pre_l%d" % layer,
    )(u, ca_t, cb_t, caw, cab, cbw, cbb, gbias, galog, lblog)

    g3 = g_t.reshape(DEC_GATE_ROWS, 1, n)

    def head_rows(sec):
        return pl.BlockSpec((HD, n), lambda h: (sec * H + h, 0))

    def gate_row(base):
        return pl.BlockSpec((None, 1, n), lambda h: (base + h, 0, 0))

    def state_in(x):
        nd = x.ndim - 2
        return pl.BlockSpec((None, None) + x.shape[2:], lambda h: (layer, h) + (0,) * nd)

    def state_out(shape):
        nd = len(shape)
        return pl.BlockSpec((None,) + tuple(shape), lambda h: (h,) + (0,) * nd)

    st_shape = (HD, HD, B)
    o_spec = pl.BlockSpec((HD, n), lambda h: (h, 0))
    o_shape = jax.ShapeDtypeStruct((GW, n), F32)
    st_out = jax.ShapeDtypeStruct((H,) + st_shape, F32)

    sa_new, oa = pl.pallas_call(
        functools.partial(_dec_a_kernel, T=T), grid=(H,),
        in_specs=[head_rows(0), head_rows(1), head_rows(2), gate_row(GL_DECAY_A), gate_row(GL_BETA), state_in(sa_t)],
        out_specs=[state_out(st_shape), o_spec], out_shape=[st_out, o_shape],
        compiler_params=cp1, name="dec_a_l%d" % layer,
    )(a_t, a_t, a_t, g3, g3, sa_t)

    hpg = H // G_B
    sb_new, ob = pl.pallas_call(
        functools.partial(_dec_b_kernel, T=T), grid=(H,),
        in_specs=[head_rows(0),
                  pl.BlockSpec((HD, n), lambda h: (H + h // hpg, 0)),
                  pl.BlockSpec((HD, n), lambda h: (H + G_B + h // hpg, 0)),
                  gate_row(GL_DT_B), gate_row(GL_DT_B2),
                  pl.BlockSpec((None, 1, B), lambda h: (h, 0, 0)), state_in(sb_t)],
        out_specs=[state_out(st_shape), o_spec], out_shape=[st_out, o_shape],
        compiler_params=cp1, name="dec_b_l%d" % layer,
    )(b_t, b_t, b_t, g3, g3, dskip, sb_t)

    sc_new, oc = pl.pallas_call(
        functools.partial(_dec_c_kernel, T=T), grid=(H,),
        in_specs=[head_rows(0), head_rows(1), head_rows(2), state_in(sc_t)],
        out_specs=[state_out(st_shape), o_spec], out_shape=[st_out, o_shape],
        compiler_params=cp1, name="dec_c_l%d" % layer,
    )(c_t, c_t, c_t, sc_t)

    dm3 = dm_t.reshape(dm_t.shape[0], H, 1, B)
    dc_new, dn_new, dm_new, od = pl.pallas_call(
        functools.partial(_dec_d_kernel, T=T), grid=(H,),
        in_specs=[head_rows(0), head_rows(1), head_rows(2), gate_row(GL_I_D), gate_row(GL_F_D),
                  state_in(dc_t), state_in(dn_t), state_in(dm3)],
        out_specs=[state_out(st_shape), state_out((HD, B)), state_out((1, B)), o_spec],
        out_shape=[st_out, jax.ShapeDtypeStruct((H, HD, B), F32), jax.ShapeDtypeStruct((H, 1, B), F32), o_shape],
        compiler_params=cp1, name="dec_d_l%d" % layer,
    )(d_t, d_t, d_t, g3, g3, dc_t, dn_t, dm3)

    y = pl.pallas_call(
        _sample_post_kernel, grid=(T,),
        in_specs=[lane_blk(GW)] * 4 + [pl.BlockSpec((B, U_COLS), lambda t: (t, 0))]
                 + [whole(p) for p in (nag, nbg, ncg, ndg)],
        out_specs=pl.BlockSpec((B, D_MODEL), lambda t: (t, 0)),
        out_shape=jax.ShapeDtypeStruct((n, D_MODEL), F32),
        compiler_params=cp1, name="sample_post_l%d" % layer,
    )(oa, ob, oc, od, u, nag, nbg, ncg, ndg)
    return y, (ca_new, sa_new, cb_new, sb_new, sc_new, dc_new, dn_new, dm_new.reshape(H, B))


def _prep_w_in_kernel(w_ref, o_ref):
    w = w_ref[...]
    o_ref[:, 0:1024] = w[:, 0:1024].astype(BF16)
    o_ref[:, 1024:1792] = w[:, 1032:1800].astype(BF16)
    o_ref[:, 1792:OFF_GATES] = w[:, 1804:3852].astype(BF16)
    gates = jnp.concatenate([w[:, 1024:1032], w[:, 1800:1804], w[:, 3852:3860], w[:, 1800:1804],
                             jnp.zeros((w.shape[0], U_COLS - OFF_GATES - N_GATE_COLS), F32)], axis=1)
    o_ref[:, OFF_GATES:U_COLS] = gates.astype(BF16)


def _prep_w_in(w_in):
    depth, d, cols = w_in.shape
    rows = 256
    return pl.pallas_call(
        _prep_w_in_kernel,
        grid=(depth, d // rows),
        in_specs=[pl.BlockSpec((None, rows, cols), lambda l, i: (l, i, 0))],
        out_specs=pl.BlockSpec((None, rows, U_COLS), lambda l, i: (l, i, 0)),
        out_shape=jax.ShapeDtypeStruct((depth, d, U_COLS), BF16),
        compiler_params=pltpu.CompilerParams(dimension_semantics=("arbitrary", "arbitrary"),
                                             vmem_limit_bytes=VMEM_LIMIT),
        name="prep_w_in",
    )(w_in)


def _gate_row(parts):
    row = jnp.zeros((LANES,), F32)
    for off, val in parts:
        row = lax.dynamic_update_slice(row, val.astype(F32), (off,))
    return row.reshape(1, LANES)


def _decoder(x, states, seq_t, chunk, seqs_per_step, emb_g, emb_b, lb_logits, w_in_p, layer_w):
    bsz, t, _ = x.shape
    n = bsz * t
    h = x.reshape(n, D_MODEL)
    new_states = []
    for l in range(DEPTH):
        (conv_a_w, conv_a_b, a_log_a, dt_bias_a, norm_a_g, conv_b_w, conv_b_b, a_log_b, dt_bias_b,
         d_skip_b, norm_b_g, norm_c_g, i_bias_d, f_bias_d, norm_d_g, w_out, ln1_g, ln1_b, w_up, w_down,
         ln2_g, ln2_b) = [w[l] for w in layer_w]
        if l == 0:
            h, u = _inproj(h, w_in_p, l, emb_g.reshape(1, -1), emb_b.reshape(1, -1), True)
        else:
            (u,) = _inproj(h, w_in_p, l, emb_g.reshape(1, -1), emb_b.reshape(1, -1), False)
        params = (conv_a_w, conv_a_b.reshape(1, -1), conv_b_w, conv_b_b.reshape(1, -1),
                  _gate_row([(GL_DECAY_A, dt_bias_a), (GL_DT_B, dt_bias_b), (GL_I_D, i_bias_d), (GL_F_D, f_bias_d),
                             (GL_DT_B2, dt_bias_b)]),
                  _gate_row([(GL_DECAY_A, a_log_a), (GL_DT_B, a_log_b)]),
                  norm_a_g.reshape(1, -1), norm_b_g.reshape(1, -1), norm_c_g.reshape(1, -1),
                  norm_d_g.reshape(1, -1), jnp.repeat(d_skip_b, HEAD_DIM).reshape(1, -1), lb_logits)
        y, st = _mixers(u.reshape(bsz, t, U_COLS), states, params,
                        layer=l, T=seq_t, L=chunk, P=seqs_per_step)
        new_states.append(st)
        h = _outffn(y.reshape(n, D_MODEL), h, w_out.astype(BF16), ln1_g.reshape(1, -1), ln1_b.reshape(1, -1),
                    w_up.astype(BF16), w_down.astype(BF16), ln2_g.reshape(1, -1), ln2_b.reshape(1, -1))
    return h.reshape(bsz, t, D_MODEL), tuple(jnp.stack(z) for z in zip(*new_states))


def _sample_decoder(x, states, emb_g, emb_b, lb_logits, w_in_p, layer_w):
    bsz, t, _ = x.shape
    n = bsz * t
    h = jnp.transpose(x, (1, 0, 2)).reshape(n, D_MODEL)
    ca, sa, cb, sb, sc, dc, dn, dm = states
    states_t = (jnp.transpose(ca, (0, 2, 1, 3)), jnp.transpose(sa, (0, 2, 3, 4, 1)),
                jnp.transpose(cb, (0, 2, 1, 3)), jnp.transpose(sb, (0, 2, 3, 4, 1)),
                jnp.transpose(sc, (0, 2, 3, 4, 1)), jnp.transpose(dc, (0, 2, 3, 4, 1)),
                jnp.transpose(dn, (0, 2, 3, 1)), jnp.transpose(dm, (0, 2, 1)))
    new_states = []
    for l in range(DEPTH):
        (conv_a_w, conv_a_b, a_log_a, dt_bias_a, norm_a_g, conv_b_w, conv_b_b, a_log_b, dt_bias_b,
         d_skip_b, norm_b_g, norm_c_g, i_bias_d, f_bias_d, norm_d_g, w_out, ln1_g, ln1_b, w_up, w_down,
         ln2_g, ln2_b) = [w[l] for w in layer_w]
        if l == 0:
            h, u = _inproj(h, w_in_p, l, emb_g.reshape(1, -1), emb_b.reshape(1, -1), True)
        else:
            (u,) = _inproj(h, w_in_p, l, emb_g.reshape(1, -1), emb_b.reshape(1, -1), False)
        params = (conv_a_w, conv_a_b.reshape(1, -1), conv_b_w, conv_b_b.reshape(1, -1),
                  _gate_row([(GL_DECAY_A, dt_bias_a), (GL_DT_B, dt_bias_b), (GL_I_D, i_bias_d), (GL_F_D, f_bias_d),
                             (GL_DT_B2, dt_bias_b)]),
                  _gate_row([(GL_DECAY_A, a_log_a), (GL_DT_B, a_log_b)]),
                  norm_a_g.reshape(1, -1), norm_b_g.reshape(1, -1), norm_c_g.reshape(1, -1),
                  norm_d_g.reshape(1, -1), jnp.broadcast_to(d_skip_b[:, None, None], (N_HEADS, 1, LANES)),
                  lb_logits)
        y, st = _sample_mixers(u, states_t, params, layer=l, T=t)
        new_states.append(st)
        h = _outffn(y, h, w_out.astype(BF16), ln1_g.reshape(1, -1), ln1_b.reshape(1, -1),
                    w_up.astype(BF16), w_down.astype(BF16), ln2_g.reshape(1, -1), ln2_b.reshape(1, -1))
    ca_n, sa_n, cb_n, sb_n, sc_n, dc_n, dn_n, dm_n = (jnp.stack(z) for z in zip(*new_states))
    out_states = (jnp.transpose(ca_n, (0, 2, 1, 3)), jnp.transpose(sa_n, (0, 4, 1, 2, 3)),
                  jnp.transpose(cb_n, (0, 2, 1, 3)), jnp.transpose(sb_n, (0, 4, 1, 2, 3)),
                  jnp.transpose(sc_n, (0, 4, 1, 2, 3)), jnp.transpose(dc_n, (0, 4, 1, 2, 3)),
                  jnp.transpose(dn_n, (0, 3, 1, 2)), jnp.transpose(dm_n, (0, 2, 1)))
    return jnp.transpose(h.reshape(t, bsz, D_MODEL), (1, 0, 2)), out_states


def kernel(x_prompt, x_sample, state_a_conv, state_a_ssm, state_b_conv, state_b_ssm, state_c_ssm, state_d_cmem, state_d_nvec, state_d_mstab, emb_ln_g, emb_ln_b, lb_logits_c, w_in, conv_a_w, conv_a_b, a_log_a, dt_bias_a, norm_a_g, conv_b_w, conv_b_b, a_log_b, dt_bias_b, d_skip_b, norm_b_g, norm_c_g, i_bias_d, f_bias_d, norm_d_g, w_out, ln1_g, ln1_b, w_up, w_down, ln2_g, ln2_b):
    layer_w = (conv_a_w, conv_a_b, a_log_a, dt_bias_a, norm_a_g, conv_b_w, conv_b_b, a_log_b, dt_bias_b,
               d_skip_b, norm_b_g, norm_c_g, i_bias_d, f_bias_d, norm_d_g, w_out, ln1_g, ln1_b, w_up, w_down,
               ln2_g, ln2_b)
    sample_states = (state_a_conv, state_a_ssm, state_b_conv, state_b_ssm,
                     state_c_ssm, state_d_cmem, state_d_nvec, state_d_mstab)
    prompt_states = tuple(jnp.zeros((1, x_prompt.shape[0]) + s.shape[2:], F32) for s in sample_states)
    w_in_p = _prep_w_in(w_in)
    y_p, ps = _decoder(x_prompt, prompt_states, CHUNK, CHUNK, SEQS_PER_STEP_PROMPT,
                       emb_ln_g, emb_ln_b, lb_logits_c, w_in_p, layer_w)
    y_s, ss = _sample_decoder(x_sample, sample_states, emb_ln_g, emb_ln_b, lb_logits_c, w_in_p, layer_w)
    return (y_p, y_s) + ps + ss
```

```python
import functools

import jax
import jax.numpy as jnp
from jax import lax
from jax.experimental import pallas as pl
from jax.experimental.pallas import tpu as pltpu

F32 = jnp.float32
BF16 = jnp.bfloat16

D_MODEL = 1024
DEPTH = 2
N_HEADS = 4
HEAD_DIM = 64
GROUP_WIDTH = N_HEADS * HEAD_DIM
G_B = 2
CONV_K = 4
CHUNK = 64
D_FF = 4 * D_MODEL
EPS = 1e-6
NEG = -1e30
ALPHA = (2 * DEPTH) ** 0.25

CONV_A_CH = 3 * GROUP_WIDTH
CONV_B_CH = GROUP_WIDTH + 2 * G_B * HEAD_DIM

OFF_A_QKV = 0
OFF_A_Z = 768
OFF_B_Z = 1024
OFF_B_XBC = 1280
OFF_C_Q = 1792
OFF_C_F = 2048
OFF_C_I = 2304
OFF_C_G = 2560
OFF_D_Q = 2816
OFF_D_K = 3072
OFF_D_V = 3328
OFF_D_O = 3584
OFF_GATES = 3840
U_COLS = 3968
LANES = 128
GL_BETA = 0
GL_DECAY_A = 4
GL_DT_B = 8
GL_I_D = 12
GL_F_D = 16
GL_DT_B2 = 20
N_GATE_COLS = 24

CONV_TAIL = CONV_K - 1
ROW0 = 8

VMEM_LIMIT = 56 * 1024 * 1024
TOKEN_TILE = 256
SEQS_PER_STEP_PROMPT = 4
START_C = 3
START_D = 8
START_B = 11
SEQ_STAGGER = 0
SEQS_PER_STEP_SAMPLE = 4
SAMPLE_ROWS = 8


def _layernorm(x, g, b):
    mu = jnp.mean(x, axis=-1, keepdims=True)
    xc = x - mu
    var = jnp.mean(xc * xc, axis=-1, keepdims=True)
    return xc * lax.rsqrt(var + EPS) * g + b


def _sigmoid(x):
    return 1.0 / (1.0 + jnp.exp(-x))


def _silu(x):
    return x * _sigmoid(x)


def _dot(a, b):
    return jnp.dot(a, b, preferred_element_type=F32)


def _mm(a, b):
    return _dot(a.astype(BF16), b.astype(BF16))


def _nt(a, b):
    return lax.dot_general(a.astype(BF16), b.astype(BF16), (((1,), (1,)), ((), ())),
                           preferred_element_type=F32)


def _tn(a, b):
    n = a.shape[1]
    eye = (lax.broadcasted_iota(jnp.int32, (n, n), 0) == lax.broadcasted_iota(jnp.int32, (n, n), 1))
    a_t = _nt(jnp.where(eye, 1.0, 0.0), a)
    return _dot(a_t.astype(BF16), b.astype(BF16))


def _split2(x):
    hi = x.astype(BF16)
    return hi, (x - hi.astype(F32)).astype(BF16)


def _split3(x):
    hi = x.astype(BF16)
    r1 = x - hi.astype(F32)
    mid = r1.astype(BF16)
    r2 = r1 - mid.astype(F32)
    return hi, mid, r2.astype(BF16)


def _dot01(m01, parts):
    acc = _dot(m01, parts[0])
    for p in parts[1:]:
        acc = acc + _dot(m01, p)
    return acc


def _rms(x):
    return x * lax.rsqrt(jnp.mean(x * x, axis=-1, keepdims=True) + EPS)


def _levels(L):
    out, b = [], 1
    while 2 * b <= L:
        out.append(b)
        b *= 2
    return out


def _interleaved(chains):
    pending = [ch if isinstance(ch, tuple) else (0, ch) for ch in chains]
    rnd = 0
    while pending:
        alive = []
        for start, ch in pending:
            if start > rnd:
                alive.append((start, ch))
                continue
            try:
                next(ch)
                alive.append((start, ch))
            except StopIteration:
                pass
        pending = alive
        rnd += 1
        yield


def _run_interleaved(chains):
    for _ in _interleaved(chains):
        pass


def _inproj_kernel(x_ref, w_ref, g_ref, b_ref, *out_refs, apply_ln):
    x = x_ref[...]
    if apply_ln:
        x = _layernorm(x, g_ref[...], b_ref[...])
        out_refs[0][...] = x
    out_refs[-1][...] = _dot(x.astype(BF16), w_ref[...])


def _inproj(x, w_in_p, layer, g, b, apply_ln):
    n = x.shape[0]
    tm = min(TOKEN_TILE, n)
    grid = (n // tm,)
    const = lambda i: (0, 0)
    out_shape = [jax.ShapeDtypeStruct((n, U_COLS), F32)]
    out_specs = [pl.BlockSpec((tm, U_COLS), lambda i: (i, 0))]
    if apply_ln:
        out_shape = [jax.ShapeDtypeStruct((n, D_MODEL), F32)] + out_shape
        out_specs = [pl.BlockSpec((tm, D_MODEL), lambda i: (i, 0))] + out_specs
    return pl.pallas_call(
        functools.partial(_inproj_kernel, apply_ln=apply_ln),
        grid=grid,
        in_specs=[pl.BlockSpec((tm, D_MODEL), lambda i: (i, 0)),
                  pl.BlockSpec((None, D_MODEL, U_COLS), lambda i: (layer, 0, 0), pipeline_mode=pl.Buffered(1)),
                  pl.BlockSpec((1, D_MODEL), const),
                  pl.BlockSpec((1, D_MODEL), const)],
        out_specs=out_specs,
        out_shape=out_shape,
        compiler_params=pltpu.CompilerParams(dimension_semantics=("arbitrary",),
                                             vmem_limit_bytes=VMEM_LIMIT),
        name="inproj_ln" if apply_ln else "inproj",
    )(x, w_in_p, g, b)


def _outffn_kernel(mix_ref, x_ref, wo_ref, g1_ref, b1_ref, wu_ref, wd_ref, g2_ref, b2_ref, o_ref):
    m = _dot(mix_ref[...].astype(BF16), wo_ref[...])
    h1 = _layernorm(ALPHA * x_ref[...] + m, g1_ref[...], b1_ref[...])
    up = _dot(h1.astype(BF16), wu_ref[...])
    act = jnp.square(jnp.maximum(up, 0.0))
    ff = _dot(act.astype(BF16), wd_ref[...])
    o_ref[...] = _layernorm(ALPHA * h1 + ff, g2_ref[...], b2_ref[...])


def _outffn(mix, x, wo, g1, b1, wu, wd, g2, b2):
    n = x.shape[0]
    tm = min(TOKEN_TILE, n)
    const = lambda i: (0, 0)
    tile = pl.BlockSpec((tm, D_MODEL), lambda i: (i, 0))
    row = pl.BlockSpec((1, D_MODEL), const)
    return pl.pallas_call(
        _outffn_kernel,
        grid=(n // tm,),
        in_specs=[tile, tile,
                  pl.BlockSpec((D_MODEL, D_MODEL), const, pipeline_mode=pl.Buffered(1)),
                  row, row,
                  pl.BlockSpec((D_MODEL, D_FF), const, pipeline_mode=pl.Buffered(1)),
                  pl.BlockSpec((D_FF, D_MODEL), const, pipeline_mode=pl.Buffered(1)),
                  row, row],
        out_specs=tile,
        out_shape=jax.ShapeDtypeStruct((n, D_MODEL), F32),
        compiler_params=pltpu.CompilerParams(dimension_semantics=("arbitrary",),
                                             vmem_limit_bytes=VMEM_LIMIT),
        name="outffn",
    )(mix, x, wo, g1, b1, wu, wd, g2, b2)


def _mixer_kernel(u_ref, ca_in, sa_in, cb_in, sb_in, sc_in, dc_in, dn_in, dm_in,
                  caw_ref, cab_ref, cbw_ref, cbb_ref, gbias_ref, galog_ref,
                  nag_ref, nbg_ref, ncg_ref, ndg_ref, dskip_ref, lblog_ref,
                  y_ref, ca_out, sa_out, cb_out, sb_out, sc_out, dc_out, dn_out, dm_out,
                  ub, sa, sb, sc, dc, dn, dm,
                  *, L, T, NC, P, layer):
    c = pl.program_id(1)
    H, HD = N_HEADS, HEAD_DIM
    R0, R1 = ROW0, ROW0 + L
    padded = T < L
    a_cols = slice(OFF_A_QKV, OFF_A_QKV + CONV_A_CH)
    b_cols = slice(OFF_B_XBC, OFF_B_XBC + CONV_B_CH)

    @pl.when(c == 0)
    def _():
        ub[:, R0 - CONV_TAIL:R0, a_cols] = ca_in[...]
        ub[:, R0 - CONV_TAIL:R0, b_cols] = cb_in[...]
        sa[...] = sa_in[...]
        sb[...] = sb_in[...]
        sc[...] = sc_in[...]
        dc[...] = dc_in[...]
        dn[...] = dn_in[...]
        dm[...] = dm_in[...]

    ub[:, R0:R0 + T, :] = u_ref[...]
    if padded:
        ub[:, R0 + T:R1, :] = jnp.zeros((P, L - T, U_COLS), F32)

    ri = lax.broadcasted_iota(jnp.int32, (L, L), 0)
    ci = lax.broadcasted_iota(jnp.int32, (L, L), 1)
    incl = ri >= ci
    strict = ri > ci
    tril01 = jnp.where(incl, 1.0, 0.0).astype(BF16)
    eye = jnp.where(ri == ci, 1.0, 0.0)
    levels = _levels(L)

    def level_mask(b):
        sh = b.bit_length() - 1
        rb = ri >> sh
        cb = ci >> sh
        return (rb - cb == 1) & ((cb & 1) == 0)

    lmask = {b: level_mask(b) for b in levels}
    sel01 = {}
    for b in levels:
        sh = b.bit_length()
        sel01[b] = jnp.where(ci == (((ri >> sh) << sh) + (b - 1)), 1.0, 0.0).astype(BF16)
    rowv = lax.broadcasted_iota(jnp.int32, (L, 1), 0) < T
    lane = lax.broadcasted_iota(jnp.int32, (1, LANES), 1)
    is_dt = (lane >= GL_DECAY_A) & (lane < GL_I_D)
    is_i = (lane >= GL_I_D) & (lane < GL_F_D)
    is_f = (lane >= GL_F_D) & (lane < GL_F_D + H)

    pl_sm = lblog_ref[...]
    pl_sm = jnp.exp(pl_sm - jnp.max(pl_sm, axis=0, keepdims=True))
    pl_sm = pl_sm / jnp.sum(pl_sm, axis=0, keepdims=True)
    lb = pl_sm[0:1, :]
    for i in range(1, layer + 1):
        lb = lb + pl_sm[i:i + 1, :]
    lb = lb - pl_sm[0:1, :]

    m_last_out = {}

    def seq_chains(s):
        def seg(off, width):
            return ub[s, R0:R1, off:off + width]

        def conv(off, width, w_ref, b_ref):
            acc = b_ref[...] + ub[s, R0:R1, off:off + width] * w_ref[CONV_K - 1:CONV_K, :]
            for j in range(CONV_K - 1):
                lo = R0 - CONV_TAIL + j
                acc = acc + ub[s, lo:lo + L, off:off + width] * w_ref[j:j + 1, :]
            return acc

        def put_y(off, val):
            y_ref[s, :, off:off + val.shape[1]] = val[:T] if padded else val

        pre = seg(OFF_GATES, LANES) + gbias_ref[...]
        e = jnp.exp(-jnp.abs(pre))
        l1p = jnp.log1p(e)
        softplus = jnp.maximum(pre, 0.0) + l1p
        logsig = jnp.minimum(pre, 0.0) - l1p
        logdec = jnp.where(is_dt, -jnp.exp(galog_ref[...]) * softplus, jnp.where(is_f, logsig, 0.0))
        gval = jnp.where(lane < GL_DECAY_A, _sigmoid(pre), jnp.where(is_i, pre, softplus))
        if padded:
            logdec = jnp.where(rowv, logdec, 0.0)
            gval = jnp.where(rowv, gval, jnp.where(is_i, NEG, 0.0))
        gcum = _dot01(tril01, _split3(logdec))
        gt = jnp.where(is_i, gval, gcum).T

        def col(x, j):
            return x[:, j:j + 1]

        def decay_mat(j):
            return jnp.where(incl, jnp.exp(jnp.minimum(col(gcum, j) - gt[j:j + 1, :], 0.0)), 0.0)

        qkv = _silu(conv(OFF_A_QKV, CONV_A_CH, caw_ref, cab_ref))
        a_z = seg(OFF_A_Z, GROUP_WIDTH)

        def chain_a(h):
            sl = slice(h * HD, (h + 1) * HD)
            q = qkv[:, sl]
            k = qkv[:, GROUP_WIDTH + h * HD:GROUP_WIDTH + (h + 1) * HD]
            v = qkv[:, 2 * GROUP_WIDTH + h * HD:2 * GROUP_WIDTH + (h + 1) * HD]
            q = q * lax.rsqrt(jnp.sum(q * q, axis=-1, keepdims=True) + EPS) * (HD ** -0.5)
            k = k * lax.rsqrt(jnp.sum(k * k, axis=-1, keepdims=True) + EPS)
            g = col(gcum, GL_DECAY_A + h)
            beta = col(gval, GL_BETA + h)
            dmat = decay_mat(GL_DECAY_A + h)
            eg = jnp.exp(g)
            kb = k.astype(BF16)
            kk = _nt(kb, kb)
            qk = _nt(q, kb)
            yield
            nmat = jnp.where(strict, beta * kk * dmat, 0.0)
            tinv = eye - jnp.where(lmask[1], nmat, 0.0)
            for b in levels[1:]:
                t_hi = tinv.astype(BF16)
                o_hi, o_lo = _split2(jnp.where(lmask[b], nmat, 0.0))
                x = _dot(o_hi, t_hi) + _dot(o_lo, t_hi)
                yield
                x_hi, x_lo = _split2(x)
                tinv = tinv - (_dot(t_hi, x_hi) + _dot(t_hi, x_lo))
                yield
            r_hi, r_lo = _split2(jnp.concatenate([beta * v, (beta * eg) * k], axis=-1))
            t_hi = tinv.astype(BF16)
            sol = _dot(t_hi, r_hi) + _dot(t_hi, r_lo)
            s0 = sa[s, h]
            qs = _mm(q, s0)
            yield
            unew = sol[:, :HD] - _mm(sol[:, HD:], s0)
            yield
            o = eg * qs + _mm(qk * dmat, unew)
            g_last = g[L - 1:L, :]
            sa[s, h] = jnp.exp(g_last) * s0 + _tn(k * jnp.exp(g_last - g), unew)
            yield
            put_y(h * HD, _rms(o) * nag_ref[:, sl] * _silu(a_z[:, sl]))

        xbc = _silu(conv(OFF_B_XBC, CONV_B_CH, cbw_ref, cbb_ref))
        b_z = seg(OFF_B_Z, GROUP_WIDTH)

        def chain_b(grp):
            bmat = xbc[:, GROUP_WIDTH + grp * HD:GROUP_WIDTH + (grp + 1) * HD]
            cmat = xbc[:, GROUP_WIDTH + G_B * HD + grp * HD:GROUP_WIDTH + G_B * HD + (grp + 1) * HD]
            cb = _nt(cmat, bmat)
            yield
            outs = []
            for h in range(grp * (H // G_B), (grp + 1) * (H // G_B)):
                sl = slice(h * HD, (h + 1) * HD)
                g = col(gcum, GL_DT_B + h)
                dt = col(gval, GL_DT_B + h)
                xh = xbc[:, sl]
                v = xh * dt
                s0 = sb[s, h]
                o = jnp.exp(g) * _mm(cmat, s0) + _mm(cb * decay_mat(GL_DT_B + h), v)
                g_last = g[L - 1:L, :]
                sb[s, h] = jnp.exp(g_last) * s0 + _tn(bmat * jnp.exp(g_last - g), v)
                yield
                outs.append(o + dskip_ref[:, sl] * xh)
            gw = GROUP_WIDTH // G_B
            gsl = slice(grp * gw, (grp + 1) * gw)
            ob = jnp.concatenate(outs, axis=-1) * _silu(b_z[:, gsl])
            put_y(GROUP_WIDTH + grp * gw, _rms(ob) * nbg_ref[:, gsl])

        fgate = lb + (1.0 - lb) * _sigmoid(seg(OFF_C_F, GROUP_WIDTH))
        if padded:
            fgate = jnp.where(rowv, fgate, 1.0)
        gc_parts = _split3(_dot01(tril01, _split3(jnp.log(fgate))))
        gc = gc_parts[0].astype(F32) + gc_parts[1].astype(F32) + gc_parts[2].astype(F32)
        kc = 1.0 - fgate
        c_q = seg(OFF_C_Q, GROUP_WIDTH)
        c_v = seg(OFF_C_I, GROUP_WIDTH)
        c_g = seg(OFF_C_G, GROUP_WIDTH)
        pref = {b: _dot01(sel01[b], gc_parts) for b in levels}
        g_last_t = jnp.broadcast_to(gc[L - 1:L, :], (8, GROUP_WIDTH)).T

        def chain_c(h):
            sl = slice(h * HD, (h + 1) * HD)
            q, k, v, g = c_q[:, sl], kc[:, sl], c_v[:, sl], gc[:, sl]
            att = None
            for b in levels:
                p = pref[b][:, sl]
                part = jnp.where(lmask[b],
                                 _nt(q * jnp.exp(jnp.minimum(g - p, 0.0)), k * jnp.exp(jnp.minimum(p - g, 0.0))),
                                 0.0)
                att = part if att is None else att + part
                yield
            s0 = sc[s, h]
            o = _mm(q * jnp.exp(g), s0) + _mm(att, v) + jnp.sum(q * k, axis=-1, keepdims=True) * v
            g_last = g[L - 1:L, :]
            sc[s, h] = jnp.exp(g_last_t[sl, 0:1]) * s0 + _tn(k * jnp.exp(g_last - g), v)
            yield
            put_y(2 * GROUP_WIDTH + h * HD, _rms(o) * ncg_ref[:, sl] * _sigmoid(c_g[:, sl]))

        d_q = seg(OFF_D_Q, GROUP_WIDTH)
        d_k = seg(OFF_D_K, GROUP_WIDTH)
        d_v = seg(OFF_D_V, GROUP_WIDTH)
        d_o = seg(OFF_D_O, GROUP_WIDTH)
        m_row = dm[s]

        def chain_d(h):
            sl = slice(h * HD, (h + 1) * HD)
            q, k, v = d_q[:, sl], d_k[:, sl] * (HD ** -0.5), d_v[:, sl]
            bcum = col(gcum, GL_F_D + h)
            m0 = m_row[:, h:h + 1]
            dmat = jnp.where(incl, bcum - gt[GL_F_D + h:GL_F_D + h + 1, :] + gt[GL_I_D + h:GL_I_D + h + 1, :], NEG)
            init = bcum + m0
            m_r = jnp.maximum(init, jnp.max(dmat, axis=-1, keepdims=True))
            qk = _nt(q, k)
            cmem = dc[s, h]
            qc = _mm(q, cmem)
            yield
            p = jnp.where(incl, jnp.exp(dmat - m_r), 0.0) * qk
            s_init = jnp.exp(init - m_r)
            nvec = dn[s, h:h + 1, :]
            num = s_init * qc + _mm(p, v)
            den = s_init * jnp.sum(q * nvec, axis=-1, keepdims=True) + jnp.sum(p, axis=-1, keepdims=True)
            m_last = m_r[L - 1:L, :]
            b_last = bcum[L - 1:L, :]
            scale = jnp.exp(b_last + m0 - m_last)
            kw = k * jnp.exp(b_last - bcum + col(gval, GL_I_D + h) - m_last)
            dc[s, h] = scale * cmem + _tn(kw, v)
            dn[s, h:h + 1, :] = scale * nvec + jnp.sum(kw, axis=0, keepdims=True)
            m_last_out[(s, h)] = m_last
            yield
            hh = num / jnp.maximum(jnp.abs(den), jnp.exp(-m_r))
            put_y(3 * GROUP_WIDTH + h * HD, _rms(hh) * ndg_ref[:, sl] * _sigmoid(d_o[:, sl]))

        return ([chain_a(h) for h in range(H)] + [chain_b(g) for g in range(G_B)]
                + [chain_c(h) for h in range(H)] + [chain_d(h) for h in range(H)])

    chains = []
    for s in range(P):
        chains += seq_chains(s)
    _run_interleaved(chains)

    for s in range(P):
        m_new = dm[s]
        for h in range(H):
            m_new = jnp.where(lane == h, m_last_out[(s, h)], m_new)
        dm[s] = m_new

    tail_a = ub[:, R0 + T - CONV_TAIL:R0 + T, a_cols]
    tail_b = ub[:, R0 + T - CONV_TAIL:R0 + T, b_cols]
    ub[:, R0 - CONV_TAIL:R0, a_cols] = tail_a
    ub[:, R0 - CONV_TAIL:R0, b_cols] = tail_b

    @pl.when(c == NC - 1)
    def _():
        ca_out[...] = tail_a
        cb_out[...] = tail_b
        sa_out[...] = sa[...]
        sb_out[...] = sb[...]
        sc_out[...] = sc[...]
        dc_out[...] = dc[...]
        dn_out[...] = dn[...]
        dm_out[...] = dm[...]


def _mixer_pair_kernel(u_ref, ca_in, sa_in, cb_in, sb_in, sc_in, dc_in, dn_in, dm_in,
                       caw_ref, cab_ref, cbw_ref, cbb_ref, gbias_ref, galog_ref,
                       nag_ref, nbg_ref, ncg_ref, ndg_ref, dskip_ref, lblog_ref,
                       y_ref, ca_out, sa_out, cb_out, sb_out, sc_out, dc_out, dn_out, dm_out,
                       cbuf, sa, sb, sc, dc, dn, dm,
                       *, NC, P, layer):
    c = pl.program_id(1)
    H, HD, L, W = N_HEADS, HEAD_DIM, CHUNK, LANES
    NP = H // 2
    R0, R1 = ROW0, ROW0 + L
    CB_A, CB_B = 0, CONV_A_CH
    f32 = F32

    def iota(shape, d):
        return lax.broadcasted_iota(jnp.int32, shape, d)

    def one_bf16(mask):
        return jnp.where(mask, 1.0, 0.0).astype(BF16)

    r_ll, c_ll = iota((L, L), 0), iota((L, L), 1)
    tril01 = one_bf16(r_ll >= c_ll)
    r_lw, c_lw = iota((L, W), 0), iota((L, W), 1)
    r_lg = iota((L, GROUP_WIDTH), 0)
    j_lw = c_lw & (HD - 1)
    incl_p = r_lw >= j_lw
    strict_p = r_lw > j_lw
    levels = _levels(L)
    lmask = {}
    for b in levels:
        sh = b.bit_length() - 1
        rb, cb = r_lw >> sh, j_lw >> sh
        lmask[b] = (rb - cb == 1) & ((cb & 1) == 0)
    r_ww, c_ww = iota((W, W), 0), iota((W, W), 1)
    bdiag = (r_ww >> 6) == (c_ww >> 6)
    ones_bd = one_bf16(bdiag)
    ones_ww = jnp.ones((W, W), BF16)
    eye_ww = one_bf16(r_ww == c_ww)
    lane = iota((1, W), 1)
    lo_half = lane < HD
    is_dt = (lane >= GL_DECAY_A) & (lane < GL_I_D)
    is_i = (lane >= GL_I_D) & (lane < GL_F_D)
    is_f = (lane >= GL_F_D) & (lane < GL_F_D + H)
    is_cum = is_dt | is_f
    exp_bases = (GL_DECAY_A, GL_BETA, GL_DT_B, GL_DT_B2, GL_F_D, GL_I_D)
    exp01 = jnp.concatenate([one_bf16(r_ww == (base + 2 * p + (c_ww >> 6)))
                             for base in exp_bases for p in range(NP)], axis=1)

    def bd(x):
        xb = x.astype(BF16)
        return jnp.concatenate([xb, xb], axis=0) * ones_bd

    def half_sums(x):
        return _dot(x.astype(BF16), ones_bd)

    def transpose_bf16(x):
        return _nt(eye_ww, x)

    pl_sm = lblog_ref[...]
    pl_sm = jnp.exp(pl_sm - jnp.max(pl_sm, axis=0, keepdims=True))
    pl_sm = pl_sm / jnp.sum(pl_sm, axis=0, keepdims=True)
    lb = pl_sm[0:1, :]
    for i in range(1, layer + 1):
        lb = lb + pl_sm[i:i + 1, :]
    lb = lb - pl_sm[0:1, :]

    @pl.when(c == 0)
    def _():
        cbuf[:, R0 - CONV_TAIL:R0, CB_A:CB_A + CONV_A_CH] = ca_in[...]
        cbuf[:, R0 - CONV_TAIL:R0, CB_B:CB_B + CONV_B_CH] = cb_in[...]
        zero = jnp.zeros((W, W), f32)
        for s in range(P):
            m_row = dm_in[s]
            for p in range(NP):
                h0, h1 = 2 * p, 2 * p + 1
                for ref, src in ((sa, sa_in), (sc, sc_in), (dc, dc_in)):
                    ref[s, p] = zero
                    ref[s, p, 0:HD, 0:HD] = src[s, h0]
                    ref[s, p, HD:W, HD:W] = src[s, h1]
                sb[s, p] = zero
                sb[s, p, p * HD:(p + 1) * HD, 0:HD] = sb_in[s, h0]
                sb[s, p, p * HD:(p + 1) * HD, HD:W] = sb_in[s, h1]
                dn[s, p] = jnp.concatenate([dn_in[s, h0:h0 + 1, :], dn_in[s, h1:h1 + 1, :]], axis=1)
                dm[s, p] = jnp.where(lo_half, m_row[:, h0:h0 + 1], m_row[:, h1:h1 + 1])

    cbuf[:, R0:R1, CB_A:CB_A + CONV_A_CH] = u_ref[:, :, OFF_A_QKV:OFF_A_QKV + CONV_A_CH]
    cbuf[:, R0:R1, CB_B:CB_B + CONV_B_CH] = u_ref[:, :, OFF_B_XBC:OFF_B_XBC + CONV_B_CH]

    def seq_chains(s):
        def seg(off, width):
            return u_ref[s, :, off:off + width]

        def conv(off, width, w_ref, b_ref):
            acc = b_ref[...] + cbuf[s, R0:R1, off:off + width] * w_ref[CONV_K - 1:CONV_K, :]
            for j in range(CONV_K - 1):
                lo = R0 - CONV_TAIL + j
                acc = acc + cbuf[s, lo:lo + L, off:off + width] * w_ref[j:j + 1, :]
            return acc

        pre = seg(OFF_GATES, W) + gbias_ref[...]
        e = jnp.exp(-jnp.abs(pre))
        l1p = jnp.log1p(e)
        softplus = jnp.maximum(pre, 0.0) + l1p
        logsig = jnp.minimum(pre, 0.0) - l1p
        logdec = jnp.where(is_dt, -jnp.exp(galog_ref[...]) * softplus, jnp.where(is_f, logsig, 0.0))
        gval = jnp.where(lane < GL_DECAY_A, _sigmoid(pre), jnp.where(is_i, pre, softplus))
        gcum = _dot01(tril01, _split3(logdec))
        gsrc = jnp.where(is_cum, gcum, gval)
        src = _split3(gsrc)
        expd = _dot(src[0], exp01) + _dot(src[1], exp01) + _dot(src[2], exp01)
        gsrc_t = gsrc.T

        def expanded(k, p):
            o = (k * NP + p) * W
            return expd[:, o:o + W]

        def row_pair(base, p):
            r = base + 2 * p
            return jnp.concatenate([gsrc_t[r:r + 1, :], gsrc_t[r + 1:r + 2, :]], axis=1)

        qkv = _silu(conv(CB_A, CONV_A_CH, caw_ref, cab_ref))

        def chain_a(p):
            ps = slice(p * W, (p + 1) * W)
            q = qkv[:, p * W:(p + 1) * W]
            k = qkv[:, GROUP_WIDTH + p * W:GROUP_WIDTH + (p + 1) * W]
            v = qkv[:, 2 * GROUP_WIDTH + p * W:2 * GROUP_WIDTH + (p + 1) * W]
            ssq = half_sums(q * q)
            ssk = half_sums(k * k)
            g = expanded(0, p)
            beta = expanded(1, p)
            grow = row_pair(GL_DECAY_A, p)
            yield
            q = q * lax.rsqrt(ssq + EPS) * (HD ** -0.5)
            k = k * lax.rsqrt(ssk + EPS)
            eg = jnp.exp(g)
            g_last = g[L - 1:L, :]
            kbd = bd(k)
            kk = _nt(k, kbd)
            qk = _nt(q, kbd)
            kw_t = transpose_bf16(k * jnp.exp(g_last - g))
            dmat = jnp.where(incl_p, jnp.exp(jnp.minimum(g - grow, 0.0)), 0.0)
            yield
            nmat = jnp.where(strict_p, beta * kk * dmat, 0.0)
            low = -jnp.where(lmask[1], nmat, 0.0)
            for b in levels[1:]:
                off = jnp.where(lmask[b], nmat, 0.0)
                x = off + _dot(off.astype(BF16), bd(low))
                yield
                low = low - x - _dot(low.astype(BF16), bd(x))
                yield
            rv = beta * v
            rk = (beta * eg) * k
            rhs_bd = jnp.concatenate([bd(rv), bd(rk)], axis=1)
            sol = jnp.concatenate([rv, rk], axis=1) + _dot(low.astype(BF16), rhs_bd)
            s0 = sa[s, p]
            qs = _mm(q, s0)
            yield
            unew = sol[:, :W] - _mm(sol[:, W:], s0)
            yield
            o = eg * qs + _mm(qk * dmat, bd(unew))
            sa[s, p] = jnp.exp(g_last) * s0 + jnp.where(bdiag, _mm(kw_t, unew), 0.0)
            ms = half_sums(o * o) * (1.0 / HD)
            yield
            a_z = seg(OFF_A_Z + p * W, W)
            y_ref[s, :, ps] = o * lax.rsqrt(ms + EPS) * nag_ref[:, ps] * _silu(a_z)

        xbc = _silu(conv(CB_B, CONV_B_CH, cbw_ref, cbb_ref))
        b_all = xbc[:, GROUP_WIDTH:GROUP_WIDTH + W]
        c_all = xbc[:, GROUP_WIDTH + W:GROUP_WIDTH + 2 * W]
        b_t = transpose_bf16(b_all)

        def chain_b(p):
            ps = slice(p * W, (p + 1) * W)
            in_grp = (lane >> 6) == p
            b_grp = jnp.where(in_grp, b_all, 0.0)
            cb = _nt(c_all, jnp.concatenate([b_grp, b_grp], axis=0))
            g = expanded(2, p)
            dt = expanded(3, p)
            grow = row_pair(GL_DT_B, p)
            yield
            g_last = g[L - 1:L, :]
            xh = xbc[:, ps]
            v = xh * dt
            s0 = sb[s, p]
            dmat = jnp.where(incl_p, jnp.exp(jnp.minimum(g - grow, 0.0)), 0.0)
            o = jnp.exp(g) * _mm(c_all, s0) + _mm(cb * dmat, bd(v))
            upd = _dot(b_t, (v * jnp.exp(g_last - g)).astype(BF16))
            sb[s, p] = jnp.exp(g_last) * s0 + jnp.where((r_ww >> 6) == p, upd, 0.0)
            yield
            ob = (o + dskip_ref[:, ps] * xh) * _silu(seg(OFF_B_Z + p * W, W))
            ms = _dot((ob * ob).astype(BF16), ones_ww) * (1.0 / W)
            yield
            y_ref[s, :, GROUP_WIDTH + p * W:GROUP_WIDTH + (p + 1) * W] = ob * lax.rsqrt(ms + EPS) * nbg_ref[:, ps]

        fgate = lb + (1.0 - lb) * _sigmoid(seg(OFF_C_F, GROUP_WIDTH))
        gc = _dot01(tril01, _split3(jnp.log(fgate)))
        kc = 1.0 - fgate
        pref = {}
        for b in levels:
            blk = 2 * b
            if blk >= 8:
                pref[b] = jnp.concatenate(
                    [jnp.broadcast_to(gc[i * blk + b - 1:i * blk + b, :], (blk, GROUP_WIDTH)) for i in range(L // blk)],
                    axis=0)
            else:
                acc = gc
                for d in range(blk):
                    sh = d - (b - 1)
                    if sh != 0:
                        acc = jnp.where((r_lg & (blk - 1)) == d, pltpu.roll(gc, sh % L, axis=0), acc)
                pref[b] = acc

        def chain_c(p):
            ps = slice(p * W, (p + 1) * W)
            q = seg(OFF_C_Q + p * W, W)
            v = seg(OFF_C_I + p * W, W)
            k, g = kc[:, ps], gc[:, ps]
            g_last = g[L - 1:L, :]
            kw_t = transpose_bf16(k * jnp.exp(g_last - g))
            e_parts = _split3(jnp.where(r_ww == c_ww, jnp.broadcast_to(g_last, (W, W)), 0.0))
            e_col = jnp.exp(_dot(e_parts[0], ones_ww) + _dot(e_parts[1], ones_ww) + _dot(e_parts[2], ones_ww))
            qk_diag = half_sums(q * k)
            yield
            att = None
            for b in levels:
                pb = pref[b][:, ps]
                kd = bd(k * jnp.exp(jnp.minimum(pb - g, 0.0))).astype(BF16)
                part = jnp.where(lmask[b], _nt(q * jnp.exp(jnp.minimum(g - pb, 0.0)), kd), 0.0)
                att = part if att is None else att + part
                yield
            s0 = sc[s, p]
            o = _mm(q * jnp.exp(g), s0) + _mm(att, bd(v)) + qk_diag * v
            sc[s, p] = e_col * s0 + jnp.where(bdiag, _mm(kw_t, v), 0.0)
            ms = half_sums(o * o) * (1.0 / HD)
            yield
            c_g = seg(OFF_C_G + p * W, W)
            y_ref[s, :, 2 * GROUP_WIDTH + p * W:2 * GROUP_WIDTH + (p + 1) * W] = (
                o * lax.rsqrt(ms + EPS) * ncg_ref[:, ps] * _sigmoid(c_g))

        def chain_d(p):
            ps = slice(p * W, (p + 1) * W)
            q = seg(OFF_D_Q + p * W, W)
            k = seg(OFF_D_K + p * W, W) * (HD ** -0.5)
            v = seg(OFF_D_V + p * W, W)
            bcum = expanded(4, p)
            ipre = expanded(5, p)
            m0 = dm[s, p]
            a = ipre - bcum
            arow = row_pair(GL_I_D, p) - row_pair(GL_F_D, p)
            qk = _nt(q, bd(k).astype(BF16))
            cmem = dc[s, p]
            qc = _mm(q, cmem)
            nvec = dn[s, p]
            qn = half_sums(q * nvec)
            cm = a
            sh = 1
            while sh < L:
                cm = jnp.maximum(cm, jnp.where(r_lw >= sh, pltpu.roll(cm, sh, axis=0), NEG))
                sh *= 2
            yield
            m_r = bcum + jnp.maximum(m0, cm)
            pmat = jnp.where(incl_p, jnp.exp(bcum + arow - m_r), 0.0) * qk
            s_init = jnp.exp(bcum + m0 - m_r)
            num = s_init * qc + _mm(pmat, bd(v))
            den = s_init * qn + half_sums(pmat)
            m_last = m_r[L - 1:L, :]
            b_last = bcum[L - 1:L, :]
            scale = jnp.exp(b_last + m0 - m_last)
            kw = k * jnp.exp(b_last - bcum + ipre - m_last)
            kw_t = transpose_bf16(kw)
            yield
            dc[s, p] = scale * cmem + jnp.where(bdiag, _mm(kw_t, v), 0.0)
            dn[s, p] = scale * nvec + jnp.sum(kw, axis=0, keepdims=True)
            dm[s, p] = m_last
            hh = num / jnp.maximum(jnp.abs(den), jnp.exp(-m_r))
            ms = half_sums(hh * hh) * (1.0 / HD)
            yield
            d_o = seg(OFF_D_O + p * W, W)
            y_ref[s, :, 3 * GROUP_WIDTH + p * W:3 * GROUP_WIDTH + (p + 1) * W] = (
                hh * lax.rsqrt(ms + EPS) * ndg_ref[:, ps] * _sigmoid(d_o))

        return ([chain_a(p) for p in range(NP)] + [(START_C, chain_c(p)) for p in range(NP)]
                + [(START_D, chain_d(p)) for p in range(NP)] + [(START_B, chain_b(p)) for p in range(NP)])

    def seq_driver(s):
        chains = seq_chains(s)
        yield
        yield from _interleaved(chains)

    _run_interleaved([(s * SEQ_STAGGER, seq_driver(s)) for s in range(P)])

    tail_a = cbuf[:, R1 - CONV_TAIL:R1, CB_A:CB_A + CONV_A_CH]
    tail_b = cbuf[:, R1 - CONV_TAIL:R1, CB_B:CB_B + CONV_B_CH]
    cbuf[:, R0 - CONV_TAIL:R0, CB_A:CB_A + CONV_A_CH] = tail_a
    cbuf[:, R0 - CONV_TAIL:R0, CB_B:CB_B + CONV_B_CH] = tail_b

    @pl.when(c == NC - 1)
    def _():
        ca_out[...] = tail_a
        cb_out[...] = tail_b
        for s in range(P):
            m_row = jnp.zeros((1, W), f32)
            for p in range(NP):
                h0, h1 = 2 * p, 2 * p + 1
                for ref, dst in ((sa, sa_out), (sc, sc_out), (dc, dc_out)):
                    dst[s, h0] = ref[s, p, 0:HD, 0:HD]
                    dst[s, h1] = ref[s, p, HD:W, HD:W]
                sb_out[s, h0] = sb[s, p, p * HD:(p + 1) * HD, 0:HD]
                sb_out[s, h1] = sb[s, p, p * HD:(p + 1) * HD, HD:W]
                nrow = dn[s, p]
                dn_out[s, h0:h0 + 1, :] = nrow[:, 0:HD]
                dn_out[s, h1:h1 + 1, :] = nrow[:, HD:W]
                mp = dm[s, p]
                m_row = jnp.where(lane == h0, mp[:, 0:1], jnp.where(lane == h1, mp[:, HD:HD + 1], m_row))
            dm_out[s] = m_row


def _mixers(u, states, params, *, layer, T, L, P):
    bsz, ttot, _ = u.shape
    nc = ttot // T
    H, HD = N_HEADS, HEAD_DIM
    ca, s_a, cb, s_b, s_c, d_c, d_n, d_m = states
    d_m = jnp.pad(d_m, ((0, 0), (0, 0), (0, LANES - H))).reshape(d_m.shape[0], bsz, 1, LANES)

    def per_seq(shape):
        nd = len(shape)
        return pl.BlockSpec((P,) + tuple(shape), lambda b, c: (b,) + (0,) * nd)

    def per_seq_in(x, shape):
        nd = len(shape)
        lidx = layer if x.shape[0] > 1 else 0
        return pl.BlockSpec((None, P) + tuple(shape), lambda b, c: (lidx, b) + (0,) * nd)

    def whole(x):
        nd = x.ndim
        return pl.BlockSpec(x.shape, lambda b, c: (0,) * nd)

    state_dims = [(CONV_TAIL, CONV_A_CH), (H, HD, HD), (CONV_TAIL, CONV_B_CH), (H, HD, HD), (H, HD, HD),
                  (H, HD, HD), (H, HD), (1, LANES)]
    state_specs = [per_seq(d) for d in state_dims]
    state_shapes = [jax.ShapeDtypeStruct((bsz,) + d, F32) for d in state_dims]
    if T == CHUNK and L == CHUNK:
        body = functools.partial(_mixer_pair_kernel, NC=nc, P=P, layer=layer)
        pair_state = (P, H // 2, LANES, LANES)
        scratch = ([pltpu.VMEM((P, ROW0 + L, CONV_A_CH + CONV_B_CH), F32)]
                   + [pltpu.VMEM(pair_state, F32) for _ in range(4)]
                   + [pltpu.VMEM((P, H // 2, 1, LANES), F32) for _ in range(2)])
    else:
        body = functools.partial(_mixer_kernel, L=L, T=T, NC=nc, P=P, layer=layer)
        scratch = ([pltpu.VMEM((P, ROW0 + L, U_COLS), F32)]
                   + [pltpu.VMEM((P,) + d, F32) for d in state_dims[1:2] + state_dims[3:]])
    outs = pl.pallas_call(
        body,
        grid=(bsz // P, nc),
        in_specs=[pl.BlockSpec((P, T, U_COLS), lambda b, c: (b, c, 0))]
                 + [per_seq_in(x, d) for x, d in zip((ca, s_a, cb, s_b, s_c, d_c, d_n, d_m), state_dims)]
                 + [whole(p) for p in params],
        out_specs=[pl.BlockSpec((P, T, D_MODEL), lambda b, c: (b, c, 0))] + state_specs,
        out_shape=[jax.ShapeDtypeStruct((bsz, ttot, D_MODEL), F32)] + state_shapes,
        scratch_shapes=scratch,
        compiler_params=pltpu.CompilerParams(dimension_semantics=("arbitrary", "arbitrary"),
                                             vmem_limit_bytes=VMEM_LIMIT),
        name="mixers_l%d_t%d" % (layer, T),
    )(u, ca, s_a, cb, s_b, s_c, d_c, d_n, d_m, *params)
    y, new = outs[0], list(outs[1:])
    new[7] = new[7].reshape(bsz, LANES)[:, :H]
    return y, tuple(new)


DEC_GATE_ROWS = 128


def _sample_pre_kernel(u_ref, ca_ref, cb_ref, caw_ref, cab_ref, cbw_ref, cbb_ref, gbias_ref, galog_ref, lblog_ref,
                       at_ref, bt_ref, ct_ref, dt_ref, gt_ref, ca_out, cb_out, hist, *, layer, T):
    t = pl.program_id(0)
    na = CONV_A_CH

    @pl.when(t == 0)
    def _():
        hist[0:CONV_TAIL, :, 0:na] = ca_ref[...]
        hist[0:CONV_TAIL, :, na:] = cb_ref[...]

    hist[CONV_TAIL + t, :, 0:na] = u_ref[:, OFF_A_QKV:OFF_A_QKV + CONV_A_CH]
    hist[CONV_TAIL + t, :, na:] = u_ref[:, OFF_B_XBC:OFF_B_XBC + CONV_B_CH]
    w_all = jnp.concatenate([caw_ref[...], cbw_ref[...]], axis=1)
    acc = jnp.concatenate([cab_ref[...], cbb_ref[...]], axis=1)
    for j in range(CONV_K):
        acc = acc + hist[t + j] * w_all[j:j + 1, :]
    act = _silu(acc)
    at_ref[...] = act[:, 0:na].T
    bt_ref[...] = act[:, na:].T

    lane = lax.broadcasted_iota(jnp.int32, (1, LANES), 1)
    pre = u_ref[:, OFF_GATES:OFF_GATES + LANES] + gbias_ref[...]
    e = jnp.exp(-jnp.abs(pre))
    l1p = jnp.log1p(e)
    softplus = jnp.maximum(pre, 0.0) + l1p
    logsig = jnp.minimum(pre, 0.0) - l1p
    is_dt = (lane >= GL_DECAY_A) & (lane < GL_I_D)
    is_i = (lane >= GL_I_D) & (lane < GL_F_D)
    is_f = (lane >= GL_F_D) & (lane < GL_F_D + N_HEADS)
    decay = jnp.exp(-jnp.exp(galog_ref[...]) * softplus)
    gates = jnp.where(lane < GL_DECAY_A, _sigmoid(pre),
                      jnp.where(is_dt, decay, jnp.where(is_i, pre, jnp.where(is_f, logsig, softplus))))
    gt_ref[...] = gates.T

    pl_sm = lblog_ref[...]
    pl_sm = jnp.exp(pl_sm - jnp.max(pl_sm, axis=0, keepdims=True))
    pl_sm = pl_sm / jnp.sum(pl_sm, axis=0, keepdims=True)
    lb = pl_sm[0:1, :]
    for i in range(1, layer + 1):
        lb = lb + pl_sm[i:i + 1, :]
    lb = lb - pl_sm[0:1, :]
    fgate = lb + (1.0 - lb) * _sigmoid(u_ref[:, OFF_C_F:OFF_C_F + GROUP_WIDTH])
    gw = GROUP_WIDTH
    ct_ref[0:gw, :] = u_ref[:, OFF_C_Q:OFF_C_Q + gw].T
    ct_ref[gw:2 * gw, :] = fgate.T
    ct_ref[2 * gw:3 * gw, :] = u_ref[:, OFF_C_I:OFF_C_I + gw].T
    dt_ref[0:gw, :] = u_ref[:, OFF_D_Q:OFF_D_Q + gw].T
    dt_ref[gw:2 * gw, :] = (u_ref[:, OFF_D_K:OFF_D_K + gw] * (HEAD_DIM ** -0.5)).T
    dt_ref[2 * gw:3 * gw, :] = u_ref[:, OFF_D_V:OFF_D_V + gw].T

    @pl.when(t == T - 1)
    def _():
        ca_out[...] = hist[T:T + CONV_TAIL, :, 0:na]
        cb_out[...] = hist[T:T + CONV_TAIL, :, na:]


def _colsum(x):
    return jnp.sum(x, axis=0, keepdims=True)


def _dec_a_kernel(q_ref, k_ref, v_ref, a_ref, beta_ref, s_in, s_out, o_ref, *, T):
    HD, B = HEAD_DIM, LANES
    for t in range(T):
        ln = slice(t * B, (t + 1) * B)
        q, k, v = q_ref[:, ln], k_ref[:, ln], v_ref[:, ln]
        q = q * (lax.rsqrt(_colsum(q * q) + EPS) * (HD ** -0.5))
        k = k * lax.rsqrt(_colsum(k * k) + EPS)
        a, beta = a_ref[:, ln], beta_ref[:, ln]
        src = s_in if t == 0 else s_out
        ks = [jnp.zeros((HD, B), F32), jnp.zeros((HD, B), F32)]
        for dk in range(HD):
            ks[dk & 1] = ks[dk & 1] + k[dk:dk + 1, :] * src[dk]
        unew = beta * (v - a * (ks[0] + ks[1]))
        o = [jnp.zeros((HD, B), F32), jnp.zeros((HD, B), F32)]
        for dk in range(HD):
            sn = a * src[dk] + k[dk:dk + 1, :] * unew
            s_out[dk] = sn
            o[dk & 1] = o[dk & 1] + q[dk:dk + 1, :] * sn
        ot = o[0] + o[1]
        o_ref[:, ln] = ot * lax.rsqrt(_colsum(ot * ot) * (1.0 / HD) + EPS)


def _dec_b_kernel(x_ref, b_ref, c_ref, a_ref, dt_ref, skip_ref, s_in, s_out, o_ref, *, T):
    HD, B = HEAD_DIM, LANES
    for t in range(T):
        ln = slice(t * B, (t + 1) * B)
        x, bm, cm = x_ref[:, ln], b_ref[:, ln], c_ref[:, ln]
        a = a_ref[:, ln]
        xdt = x * dt_ref[:, ln]
        src = s_in if t == 0 else s_out
        o = [jnp.zeros((HD, B), F32), jnp.zeros((HD, B), F32)]
        for n in range(HD):
            sn = a * src[n] + bm[n:n + 1, :] * xdt
            s_out[n] = sn
            o[n & 1] = o[n & 1] + cm[n:n + 1, :] * sn
        o_ref[:, ln] = o[0] + o[1] + skip_ref[...] * x


def _dec_c_kernel(q_ref, f_ref, v_ref, s_in, s_out, o_ref, *, T):
    HD, B = HEAD_DIM, LANES
    for t in range(T):
        ln = slice(t * B, (t + 1) * B)
        q, f, v = q_ref[:, ln], f_ref[:, ln], v_ref[:, ln]
        k = 1.0 - f
        src = s_in if t == 0 else s_out
        o = [jnp.zeros((HD, B), F32), jnp.zeros((HD, B), F32)]
        for dk in range(HD):
            sn = f[dk:dk + 1, :] * src[dk] + k[dk:dk + 1, :] * v
            s_out[dk] = sn
            o[dk & 1] = o[dk & 1] + q[dk:dk + 1, :] * sn
        ot = o[0] + o[1]
        o_ref[:, ln] = ot * lax.rsqrt(_colsum(ot * ot) * (1.0 / HD) + EPS)


def _dec_d_kernel(q_ref, k_ref, v_ref, i_ref, f_ref, c_in, n_in, m_in, c_out, n_out, m_out, o_ref, *, T):
    HD, B = HEAD_DIM, LANES
    m = m_in[...]
    nvec = n_in[...]
    for t in range(T):
        ln = slice(t * B, (t + 1) * B)
        q, k, v = q_ref[:, ln], k_ref[:, ln], v_ref[:, ln]
        ipre, logf = i_ref[:, ln], f_ref[:, ln]
        m_new = jnp.maximum(logf + m, ipre)
        fs = jnp.exp(logf + m - m_new)
        kw = k * jnp.exp(ipre - m_new)
        src = c_in if t == 0 else c_out
        num = [jnp.zeros((HD, B), F32), jnp.zeros((HD, B), F32)]
        for dk in range(HD):
            cn = fs * src[dk] + kw[dk:dk + 1, :] * v
            c_out[dk] = cn
            num[dk & 1] = num[dk & 1] + q[dk:dk + 1, :] * cn
        nvec = fs * nvec + kw
        den = _colsum(q * nvec)
        hh = (num[0] + num[1]) / jnp.maximum(jnp.abs(den), jnp.exp(-m_new))
        o_ref[:, ln] = hh * lax.rsqrt(_colsum(hh * hh) * (1.0 / HD) + EPS)
        m = m_new
    n_out[...] = nvec
    m_out[...] = m


def _sample_post_kernel(oa_ref, ob_ref, oc_ref, od_ref, u_ref, nag_ref, nbg_ref, ncg_ref, ndg_ref, y_ref):
    gw = GROUP_WIDTH
    y_ref[:, 0:gw] = oa_ref[...].T * nag_ref[...] * _silu(u_ref[:, OFF_A_Z:OFF_A_Z + gw])
    ob = ob_ref[...].T * _silu(u_ref[:, OFF_B_Z:OFF_B_Z + gw])
    grp = gw // G_B
    for g in range(G_B):
        sl = slice(g * grp, (g + 1) * grp)
        y_ref[:, gw + g * grp:gw + (g + 1) * grp] = _rms(ob[:, sl]) * nbg_ref[:, sl]
    y_ref[:, 2 * gw:3 * gw] = oc_ref[...].T * ncg_ref[...] * _sigmoid(u_ref[:, OFF_C_G:OFF_C_G + gw])
    y_ref[:, 3 * gw:4 * gw] = od_ref[...].T * ndg_ref[...] * _sigmoid(u_ref[:, OFF_D_O:OFF_D_O + gw])


def _sample_mixers(u, states_t, params, *, layer, T):
    B, H, HD, GW = LANES, N_HEADS, HEAD_DIM, GROUP_WIDTH
    ca_t, sa_t, cb_t, sb_t, sc_t, dc_t, dn_t, dm_t = states_t
    (caw, cab, cbw, cbb, gbias, galog, nag, nbg, ncg, ndg, dskip, lblog) = params
    cp1 = pltpu.CompilerParams(dimension_semantics=("arbitrary",), vmem_limit_bytes=VMEM_LIMIT)
    n = T * B

    def whole(x):
        nd = x.ndim
        return pl.BlockSpec(x.shape, lambda i: (0,) * nd)

    def lane_blk(rows):
        return pl.BlockSpec((rows, B), lambda t: (0, t))

    a_t, b_t, c_t, d_t, g_t, ca_new, cb_new = pl.pallas_call(
        functools.partial(_sample_pre_kernel, layer=layer, T=T),
        grid=(T,),
        in_specs=[pl.BlockSpec((B, U_COLS), lambda t: (t, 0)),
                  pl.BlockSpec((None, CONV_TAIL, B, CONV_A_CH), lambda t: (layer, 0, 0, 0)),
                  pl.BlockSpec((None, CONV_TAIL, B, CONV_B_CH), lambda t: (layer, 0, 0, 0))]
                 + [whole(p) for p in (caw, cab, cbw, cbb, gbias, galog, lblog)],
        out_specs=[lane_blk(CONV_A_CH), lane_blk(CONV_B_CH), lane_blk(3 * GW), lane_blk(3 * GW),
                   lane_blk(DEC_GATE_ROWS),
                   pl.BlockSpec((CONV_TAIL, B, CONV_A_CH), lambda t: (0, 0, 0)),
                   pl.BlockSpec((CONV_TAIL, B, CONV_B_CH), lambda t: (0, 0, 0))],
        out_shape=[jax.ShapeDtypeStruct((CONV_A_CH, n), F32), jax.ShapeDtypeStruct((CONV_B_CH, n), F32),
                   jax.ShapeDtypeStruct((3 * GW, n), F32), jax.ShapeDtypeStruct((3 * GW, n), F32),
                   jax.ShapeDtypeStruct((DEC_GATE_ROWS, n), F32),
                   jax.ShapeDtypeStruct((CONV_TAIL, B, CONV_A_CH), F32),
                   jax.ShapeDtypeStruct((CONV_TAIL, B, CONV_B_CH), F32)],
        scratch_shapes=[pltpu.VMEM((CONV_TAIL + T, B, CONV_A_CH + CONV_B_CH), F32)],
        compiler_params=cp1, name="sample_pre_l%d" % layer,
    )(u, ca_t, cb_t, caw, cab, cbw, cbb, gbias, galog, lblog)

    g3 = g_t.reshape(DEC_GATE_ROWS, 1, n)

    def head_rows(sec):
        return pl.BlockSpec((HD, n), lambda h: (sec * H + h, 0))

    def gate_row(base):
        return pl.BlockSpec((None, 1, n), lambda h: (base + h, 0, 0))

    def state_in(x):
        nd = x.ndim - 2
        return pl.BlockSpec((None, None) + x.shape[2:], lambda h: (layer, h) + (0,) * nd)

    def state_out(shape):
        nd = len(shape)
        return pl.BlockSpec((None,) + tuple(shape), lambda h: (h,) + (0,) * nd)

    st_shape = (HD, HD, B)
    o_spec = pl.BlockSpec((HD, n), lambda h: (h, 0))
    o_shape = jax.ShapeDtypeStruct((GW, n), F32)
    st_out = jax.ShapeDtypeStruct((H,) + st_shape, F32)

    sa_new, oa = pl.pallas_call(
        functools.partial(_dec_a_kernel, T=T), grid=(H,),
        in_specs=[head_rows(0), head_rows(1), head_rows(2), gate_row(GL_DECAY_A), gate_row(GL_BETA), state_in(sa_t)],
        out_specs=[state_out(st_shape), o_spec], out_shape=[st_out, o_shape],
        compiler_params=cp1, name="dec_a_l%d" % layer,
    )(a_t, a_t, a_t, g3, g3, sa_t)

    hpg = H // G_B
    sb_new, ob = pl.pallas_call(
        functools.partial(_dec_b_kernel, T=T), grid=(H,),
        in_specs=[head_rows(0),
                  pl.BlockSpec((HD, n), lambda h: (H + h // hpg, 0)),
                  pl.BlockSpec((HD, n), lambda h: (H + G_B + h // hpg, 0)),
                  gate_row(GL_DT_B), gate_row(GL_DT_B2),
                  pl.BlockSpec((None, 1, B), lambda h: (h, 0, 0)), state_in(sb_t)],
        out_specs=[state_out(st_shape), o_spec], out_shape=[st_out, o_shape],
        compiler_params=cp1, name="dec_b_l%d" % layer,
    )(b_t, b_t, b_t, g3, g3, dskip, sb_t)

    sc_new, oc = pl.pallas_call(
        functools.partial(_dec_c_kernel, T=T), grid=(H,),
        in_specs=[head_rows(0), head_rows(1), head_rows(2), state_in(sc_t)],
        out_specs=[state_out(st_shape), o_spec], out_shape=[st_out, o_shape],
        compiler_params=cp1, name="dec_c_l%d" % layer,
    )(c_t, c_t, c_t, sc_t)

    dm3 = dm_t.reshape(dm_t.shape[0], H, 1, B)
    dc_new, dn_new, dm_new, od = pl.pallas_call(
        functools.partial(_dec_d_kernel, T=T), grid=(H,),
        in_specs=[head_rows(0), head_rows(1), head_rows(2), gate_row(GL_I_D), gate_row(GL_F_D),
                  state_in(dc_t), state_in(dn_t), state_in(dm3)],
        out_specs=[state_out(st_shape), state_out((HD, B)), state_out((1, B)), o_spec],
        out_shape=[st_out, jax.ShapeDtypeStruct((H, HD, B), F32), jax.ShapeDtypeStruct((H, 1, B), F32), o_shape],
        compiler_params=cp1, name="dec_d_l%d" % layer,
    )(d_t, d_t, d_t, g3, g3, dc_t, dn_t, dm3)

    y = pl.pallas_call(
        _sample_post_kernel, grid=(T,),
        in_specs=[lane_blk(GW)] * 4 + [pl.BlockSpec((B, U_COLS), lambda t: (t, 0))]
                 + [whole(p) for p in (nag, nbg, ncg, ndg)],
        out_specs=pl.BlockSpec((B, D_MODEL), lambda t: (t, 0)),
        out_shape=jax.ShapeDtypeStruct((n, D_MODEL), F32),
        compiler_params=cp1, name="sample_post_l%d" % layer,
    )(oa, ob, oc, od, u, nag, nbg, ncg, ndg)
    return y, (ca_new, sa_new, cb_new, sb_new, sc_new, dc_new, dn_new, dm_new.reshape(H, B))


def _prep_w_in_kernel(w_ref, o_ref):
    w = w_ref[...]
    o_ref[:, 0:1024] = w[:, 0:1024].astype(BF16)
    o_ref[:, 1024:1792] = w[:, 1032:1800].astype(BF16)
    o_ref[:, 1792:OFF_GATES] = w[:, 1804:3852].astype(BF16)
    gates = jnp.concatenate([w[:, 1024:1032], w[:, 1800:1804], w[:, 3852:3860], w[:, 1800:1804],
                             jnp.zeros((w.shape[0], U_COLS - OFF_GATES - N_GATE_COLS), F32)], axis=1)
    o_ref[:, OFF_GATES:U_COLS] = gates.astype(BF16)


def _prep_w_in(w_in):
    depth, d, cols = w_in.shape
    rows = 256
    return pl.pallas_call(
        _prep_w_in_kernel,
        grid=(depth, d // rows),
        in_specs=[pl.BlockSpec((None, rows, cols), lambda l, i: (l, i, 0))],
        out_specs=pl.BlockSpec((None, rows, U_COLS), lambda l, i: (l, i, 0)),
        out_shape=jax.ShapeDtypeStruct((depth, d, U_COLS), BF16),
        compiler_params=pltpu.CompilerParams(dimension_semantics=("arbitrary", "arbitrary"),
                                             vmem_limit_bytes=VMEM_LIMIT),
        name="prep_w_in",
    )(w_in)


def _gate_row(parts):
    row = jnp.zeros((LANES,), F32)
    for off, val in parts:
        row = lax.dynamic_update_slice(row, val.astype(F32), (off,))
    return row.reshape(1, LANES)


def _decoder(x, states, seq_t, chunk, seqs_per_step, emb_g, emb_b, lb_logits, w_in_p, layer_w):
    bsz, t, _ = x.shape
    n = bsz * t
    h = x.reshape(n, D_MODEL)
    new_states = []
    for l in range(DEPTH):
        (conv_a_w, conv_a_b, a_log_a, dt_bias_a, norm_a_g, conv_b_w, conv_b_b, a_log_b, dt_bias_b,
         d_skip_b, norm_b_g, norm_c_g, i_bias_d, f_bias_d, norm_d_g, w_out, ln1_g, ln1_b, w_up, w_down,
         ln2_g, ln2_b) = [w[l] for w in layer_w]
        if l == 0:
            h, u = _inproj(h, w_in_p, l, emb_g.reshape(1, -1), emb_b.reshape(1, -1), True)
        else:
            (u,) = _inproj(h, w_in_p, l, emb_g.reshape(1, -1), emb_b.reshape(1, -1), False)
        params = (conv_a_w, conv_a_b.reshape(1, -1), conv_b_w, conv_b_b.reshape(1, -1),
                  _gate_row([(GL_DECAY_A, dt_bias_a), (GL_DT_B, dt_bias_b), (GL_I_D, i_bias_d), (GL_F_D, f_bias_d),
                             (GL_DT_B2, dt_bias_b)]),
                  _gate_row([(GL_DECAY_A, a_log_a), (GL_DT_B, a_log_b)]),
                  norm_a_g.reshape(1, -1), norm_b_g.reshape(1, -1), norm_c_g.reshape(1, -1),
                  norm_d_g.reshape(1, -1), jnp.repeat(d_skip_b, HEAD_DIM).reshape(1, -1), lb_logits)
        y, st = _mixers(u.reshape(bsz, t, U_COLS), states, params,
                        layer=l, T=seq_t, L=chunk, P=seqs_per_step)
        new_states.append(st)
        h = _outffn(y.reshape(n, D_MODEL), h, w_out.astype(BF16), ln1_g.reshape(1, -1), ln1_b.reshape(1, -1),
                    w_up.astype(BF16), w_down.astype(BF16), ln2_g.reshape(1, -1), ln2_b.reshape(1, -1))
    return h.reshape(bsz, t, D_MODEL), tuple(jnp.stack(z) for z in zip(*new_states))


def _sample_decoder(x, states, emb_g, emb_b, lb_logits, w_in_p, layer_w):
    bsz, t, _ = x.shape
    n = bsz * t
    h = jnp.transpose(x, (1, 0, 2)).reshape(n, D_MODEL)
    ca, sa, cb, sb, sc, dc, dn, dm = states
    states_t = (jnp.transpose(ca, (0, 2, 1, 3)), jnp.transpose(sa, (0, 2, 3, 4, 1)),
                jnp.transpose(cb, (0, 2, 1, 3)), jnp.transpose(sb, (0, 2, 3, 4, 1)),
                jnp.transpose(sc, (0, 2, 3, 4, 1)), jnp.transpose(dc, (0, 2, 3, 4, 1)),
                jnp.transpose(dn, (0, 2, 3, 1)), jnp.transpose(dm, (0, 2, 1)))
    new_states = []
    for l in range(DEPTH):
        (conv_a_w, conv_a_b, a_log_a, dt_bias_a, norm_a_g, conv_b_w, conv_b_b, a_log_b, dt_bias_b,
         d_skip_b, norm_b_g, norm_c_g, i_bias_d, f_bias_d, norm_d_g, w_out, ln1_g, ln1_b, w_up, w_down,
         ln2_g, ln2_b) = [w[l] for w in layer_w]
        if l == 0:
            h, u = _inproj(h, w_in_p, l, emb_g.reshape(1, -1), emb_b.reshape(1, -1), True)
        else:
            (u,) = _inproj(h, w_in_p, l, emb_g.reshape(1, -1), emb_b.reshape(1, -1), False)
        params = (conv_a_w, conv_a_b.reshape(1, -1), conv_b_w, conv_b_b.reshape(1, -1),
                  _gate_row([(GL_DECAY_A, dt_bias_a), (GL_DT_B, dt_bias_b), (GL_I_D, i_bias_d), (GL_F_D, f_bias_d),
                             (GL_DT_B2, dt_bias_b)]),
                  _gate_row([(GL_DECAY_A, a_log_a), (GL_DT_B, a_log_b)]),
                  norm_a_g.reshape(1, -1), norm_b_g.reshape(1, -1), norm_c_g.reshape(1, -1),
                  norm_d_g.reshape(1, -1), jnp.broadcast_to(d_skip_b[:, None, None], (N_HEADS, 1, LANES)),
                  lb_logits)
        y, st = _sample_mixers(u, states_t, params, layer=l, T=t)
        new_states.append(st)
        h = _outffn(y, h, w_out.astype(BF16), ln1_g.reshape(1, -1), ln1_b.reshape(1, -1),
                    w_up.astype(BF16), w_down.astype(BF16), ln2_g.reshape(1, -1), ln2_b.reshape(1, -1))
    ca_n, sa_n, cb_n, sb_n, sc_n, dc_n, dn_n, dm_n = (jnp.stack(z) for z in zip(*new_states))
    out_states = (jnp.transpose(ca_n, (0, 2, 1, 3)), jnp.transpose(sa_n, (0, 4, 1, 2, 3)),
                  jnp.transpose(cb_n, (0, 2, 1, 3)), jnp.transpose(sb_n, (0, 4, 1, 2, 3)),
                  jnp.transpose(sc_n, (0, 4, 1, 2, 3)), jnp.transpose(dc_n, (0, 4, 1, 2, 3)),
                  jnp.transpose(dn_n, (0, 3, 1, 2)), jnp.transpose(dm_n, (0, 2, 1)))
    return jnp.transpose(h.reshape(t, bsz, D_MODEL), (1, 0, 2)), out_states


def kernel(x_prompt, x_sample, state_a_conv, state_a_ssm, state_b_conv, state_b_ssm, state_c_ssm, state_d_cmem, state_d_nvec, state_d_mstab, emb_ln_g, emb_ln_b, lb_logits_c, w_in, conv_a_w, conv_a_b, a_log_a, dt_bias_a, norm_a_g, conv_b_w, conv_b_b, a_log_b, dt_bias_b, d_skip_b, norm_b_g, norm_c_g, i_bias_d, f_bias_d, norm_d_g, w_out, ln1_g, ln1_b, w_up, w_down, ln2_g, ln2_b):
    layer_w = (conv_a_w, conv_a_b, a_log_a, dt_bias_a, norm_a_g, conv_b_w, conv_b_b, a_log_b, dt_bias_b,
               d_skip_b, norm_b_g, norm_c_g, i_bias_d, f_bias_d, norm_d_g, w_out, ln1_g, ln1_b, w_up, w_down,
               ln2_g, ln2_b)
    sample_states = (state_a_conv, state_a_ssm, state_b_conv, state_b_ssm,
                     state_c_ssm, state_d_cmem, state_d_nvec, state_d_mstab)
    prompt_states = tuple(jnp.zeros((1, x_prompt.shape[0]) + s.shape[2:], F32) for s in sample_states)
    w_in_p = _prep_w_in(w_in)
    y_p, ps = _decoder(x_prompt, prompt_states, CHUNK, CHUNK, SEQS_PER_STEP_PROMPT,
                       emb_ln_g, emb_ln_b, lb_logits_c, w_in_p, layer_w)
    y_s, ss = _sample_decoder(x_sample, sample_states, emb_ln_g, emb_ln_b, lb_logits_c, w_in_p, layer_w)
    return (y_p, y_s) + ps + ss
```

```python
import functools

import jax
import jax.numpy as jnp
from jax import lax
from jax.experimental import pallas as pl
from jax.experimental.pallas import tpu as pltpu

F32 = jnp.float32
BF16 = jnp.bfloat16

D_MODEL = 1024
DEPTH = 2
N_HEADS = 4
HEAD_DIM = 64
GROUP_WIDTH = N_HEADS * HEAD_DIM
G_B = 2
CONV_K = 4
CHUNK = 64
D_FF = 4 * D_MODEL
EPS = 1e-6
NEG = -1e30
ALPHA = (2 * DEPTH) ** 0.25
LOG2E = 1.4426950408889634

CONV_A_CH = 3 * GROUP_WIDTH
CONV_B_CH = GROUP_WIDTH + 2 * G_B * HEAD_DIM

OFF_A_QKV = 0
OFF_A_Z = 768
OFF_B_Z = 1024
OFF_B_XBC = 1280
OFF_C_Q = 1792
OFF_C_F = 2048
OFF_C_I = 2304
OFF_C_G = 2560
OFF_D_Q = 2816
OFF_D_K = 3072
OFF_D_V = 3328
OFF_D_O = 3584
OFF_GATES = 3840
U_COLS = 3968
LANES = 128
GL_BETA = 0
GL_DECAY_A = 4
GL_DT_B = 8
GL_I_D = 12
GL_F_D = 16
GL_DT_B2 = 20
N_GATE_COLS = 24

CONV_TAIL = CONV_K - 1
ROW0 = 8

VMEM_LIMIT = 56 * 1024 * 1024
TOKEN_TILE = 256
SEQS_PER_STEP_PROMPT = 4
START_C = 3
START_D = 8
START_B = 11
SEQ_STAGGER = 0


def _layernorm(x, g, b):
    mu = jnp.mean(x, axis=-1, keepdims=True)
    xc = x - mu
    var = jnp.mean(xc * xc, axis=-1, keepdims=True)
    return xc * lax.rsqrt(var + EPS) * g + b


def _sigmoid(x):
    return 1.0 / (1.0 + jnp.exp(-x))


def _silu(x):
    return x * _sigmoid(x)


def _dot(a, b):
    return jnp.dot(a, b, preferred_element_type=F32)


def _mm(a, b):
    return _dot(a.astype(BF16), b.astype(BF16))


def _nt(a, b):
    return lax.dot_general(a.astype(BF16), b.astype(BF16), (((1,), (1,)), ((), ())),
                           preferred_element_type=F32)


def _split3(x):
    hi = x.astype(BF16)
    r1 = x - hi.astype(F32)
    mid = r1.astype(BF16)
    r2 = r1 - mid.astype(F32)
    return hi, mid, r2.astype(BF16)


def _dot01(m01, parts):
    acc = _dot(m01, parts[0])
    for p in parts[1:]:
        acc = acc + _dot(m01, p)
    return acc


def _rms(x):
    return x * lax.rsqrt(jnp.mean(x * x, axis=-1, keepdims=True) + EPS)


def _levels(L):
    out, b = [], 1
    while 2 * b <= L:
        out.append(b)
        b *= 2
    return out


def _interleaved(chains):
    pending = [ch if isinstance(ch, tuple) else (0, ch) for ch in chains]
    rnd = 0
    while pending:
        alive = []
        for start, ch in pending:
            if start > rnd:
                alive.append((start, ch))
                continue
            try:
                next(ch)
                alive.append((start, ch))
            except StopIteration:
                pass
        pending = alive
        rnd += 1
        yield


def _run_interleaved(chains):
    for _ in _interleaved(chains):
        pass


def _inproj_kernel(x_ref, w_ref, g_ref, b_ref, *out_refs, apply_ln):
    x = x_ref[...]
    if apply_ln:
        x = _layernorm(x, g_ref[...], b_ref[...])
        out_refs[0][...] = x
    out_refs[-1][...] = _dot(x.astype(BF16), w_ref[...])


def _inproj(x, w_in_p, layer, g, b, apply_ln):
    n = x.shape[0]
    tm = min(TOKEN_TILE, n)
    grid = (n // tm,)
    const = lambda i: (0, 0)
    out_shape = [jax.ShapeDtypeStruct((n, U_COLS), F32)]
    out_specs = [pl.BlockSpec((tm, U_COLS), lambda i: (i, 0))]
    if apply_ln:
        out_shape = [jax.ShapeDtypeStruct((n, D_MODEL), F32)] + out_shape
        out_specs = [pl.BlockSpec((tm, D_MODEL), lambda i: (i, 0))] + out_specs
    return pl.pallas_call(
        functools.partial(_inproj_kernel, apply_ln=apply_ln),
        grid=grid,
        in_specs=[pl.BlockSpec((tm, D_MODEL), lambda i: (i, 0)),
                  pl.BlockSpec((None, D_MODEL, U_COLS), lambda i: (layer, 0, 0), pipeline_mode=pl.Buffered(1)),
                  pl.BlockSpec((1, D_MODEL), const),
                  pl.BlockSpec((1, D_MODEL), const)],
        out_specs=out_specs,
        out_shape=out_shape,
        compiler_params=pltpu.CompilerParams(dimension_semantics=("arbitrary",),
                                             vmem_limit_bytes=VMEM_LIMIT),
        name="inproj_ln" if apply_ln else "inproj",
    )(x, w_in_p, g, b)


def _outffn_kernel(mix_ref, x_ref, wo_ref, g1_ref, b1_ref, wu_ref, wd_ref, g2_ref, b2_ref, o_ref):
    m = _dot(mix_ref[...].astype(BF16), wo_ref[...])
    h1 = _layernorm(ALPHA * x_ref[...] + m, g1_ref[...], b1_ref[...])
    up = _dot(h1.astype(BF16), wu_ref[...])
    act = jnp.square(jnp.maximum(up, 0.0))
    ff = _dot(act.astype(BF16), wd_ref[...])
    o_ref[...] = _layernorm(ALPHA * h1 + ff, g2_ref[...], b2_ref[...])


def _outffn(mix, x, wo, g1, b1, wu, wd, g2, b2):
    n = x.shape[0]
    tm = min(TOKEN_TILE, n)
    const = lambda i: (0, 0)
    tile = pl.BlockSpec((tm, D_MODEL), lambda i: (i, 0))
    row = pl.BlockSpec((1, D_MODEL), const)
    return pl.pallas_call(
        _outffn_kernel,
        grid=(n // tm,),
        in_specs=[tile, tile,
                  pl.BlockSpec((D_MODEL, D_MODEL), const, pipeline_mode=pl.Buffered(1)),
                  row, row,
                  pl.BlockSpec((D_MODEL, D_FF), const, pipeline_mode=pl.Buffered(1)),
                  pl.BlockSpec((D_FF, D_MODEL), const, pipeline_mode=pl.Buffered(1)),
                  row, row],
        out_specs=tile,
        out_shape=jax.ShapeDtypeStruct((n, D_MODEL), F32),
        compiler_params=pltpu.CompilerParams(dimension_semantics=("arbitrary",),
                                             vmem_limit_bytes=VMEM_LIMIT),
        name="outffn",
    )(mix, x, wo, g1, b1, wu, wd, g2, b2)


def _mixer_pair_kernel(u_ref, ca_in, sa_in, cb_in, sb_in, sc_in, dc_in, dn_in, dm_in,
                       caw_ref, cab_ref, cbw_ref, cbb_ref, gbias_ref, galog_ref,
                       nag_ref, nbg_ref, ncg_ref, ndg_ref, dskip_ref, lblog_ref,
                       y_ref, ca_out, sa_out, cb_out, sb_out, sc_out, dc_out, dn_out, dm_out,
                       cbuf, sa, sb, sc, dc, dn, dm, exp01_s,
                       *, NC, P, layer):
    c = pl.program_id(1)
    H, HD, L, W = N_HEADS, HEAD_DIM, CHUNK, LANES
    NP = H // 2
    R0, R1 = ROW0, ROW0 + L
    CB_A, CB_B = 0, CONV_A_CH
    f32 = F32

    def iota(shape, d):
        return lax.broadcasted_iota(jnp.int32, shape, d)

    def one_bf16(mask):
        return jnp.where(mask, 1.0, 0.0).astype(BF16)

    r_ll, c_ll = iota((L, L), 0), iota((L, L), 1)
    tril01 = one_bf16(r_ll >= c_ll)
    r_lw, c_lw = iota((L, W), 0), iota((L, W), 1)
    r_lg = iota((L, GROUP_WIDTH), 0)
    j_lw = c_lw & (HD - 1)
    incl_p = r_lw >= j_lw
    incl01 = jnp.where(incl_p, 1.0, 0.0)
    strict01 = jnp.where(r_lw > j_lw, 1.0, 0.0)
    levels = _levels(L)
    lm01 = {}
    for b in levels:
        sh = b.bit_length() - 1
        rb, cb = r_lw >> sh, j_lw >> sh
        lm01[b] = jnp.where((rb - cb == 1) & ((cb & 1) == 0), 1.0, 0.0)
    r_ww, c_ww = iota((W, W), 0), iota((W, W), 1)
    bdiag = (r_ww >> 6) == (c_ww >> 6)
    bd01 = jnp.where(bdiag, 1.0, 0.0)
    ones_bd = bd01.astype(BF16)
    ones_ww = jnp.ones((W, W), BF16)
    lane = iota((1, W), 1)
    lo_half = lane < HD
    is_dt = (lane >= GL_DECAY_A) & (lane < GL_I_D)
    is_i = (lane >= GL_I_D) & (lane < GL_F_D)
    is_f = (lane >= GL_F_D) & (lane < GL_F_D + H)
    is_cum = is_dt | is_f
    cum_scale = jnp.where(is_dt, LOG2E, 1.0)

    @pl.when((pl.program_id(0) == 0) & (c == 0))
    def _():
        exp_bases = (GL_DECAY_A, GL_BETA, GL_DT_B, GL_DT_B2, GL_F_D, GL_I_D)
        for i, (base, p) in enumerate([(base, p) for base in exp_bases for p in range(NP)]):
            exp01_s[:, i * W:(i + 1) * W] = one_bf16(r_ww == (base + 2 * p + (c_ww >> 6)))

    exp01 = exp01_s[...]

    def bd(x):
        xb = x.astype(BF16)
        return jnp.concatenate([xb, xb], axis=0) * ones_bd

    def half_sums(x):
        return _dot(x.astype(BF16), ones_bd)

    def transpose_bf16(x):
        return x.T.astype(BF16)

    pl_sm = lblog_ref[...]
    pl_sm = jnp.exp(pl_sm - jnp.max(pl_sm, axis=0, keepdims=True))
    pl_sm = pl_sm / jnp.sum(pl_sm, axis=0, keepdims=True)
    lb = pl_sm[0:1, :]
    for i in range(1, layer + 1):
        lb = lb + pl_sm[i:i + 1, :]
    lb = lb - pl_sm[0:1, :]

    @pl.when(c == 0)
    def _():
        cbuf[:, R0 - CONV_TAIL:R0, CB_A:CB_A + CONV_A_CH] = ca_in[...]
        cbuf[:, R0 - CONV_TAIL:R0, CB_B:CB_B + CONV_B_CH] = cb_in[...]
        zero = jnp.zeros((W, W), f32)
        for s in range(P):
            m_row = dm_in[s]
            for p in range(NP):
                h0, h1 = 2 * p, 2 * p + 1
                for ref, src in ((sa, sa_in), (sc, sc_in), (dc, dc_in)):
                    ref[s, p] = zero
                    ref[s, p, 0:HD, 0:HD] = src[s, h0]
                    ref[s, p, HD:W, HD:W] = src[s, h1]
                sb[s, p] = zero
                sb[s, p, p * HD:(p + 1) * HD, 0:HD] = sb_in[s, h0]
                sb[s, p, p * HD:(p + 1) * HD, HD:W] = sb_in[s, h1]
                dn[s, p] = jnp.concatenate([dn_in[s, h0:h0 + 1, :], dn_in[s, h1:h1 + 1, :]], axis=1)
                dm[s, p] = jnp.where(lo_half, m_row[:, h0:h0 + 1], m_row[:, h1:h1 + 1])

    cbuf[:, R0:R1, CB_A:CB_A + CONV_A_CH] = u_ref[:, :, OFF_A_QKV:OFF_A_QKV + CONV_A_CH]
    cbuf[:, R0:R1, CB_B:CB_B + CONV_B_CH] = u_ref[:, :, OFF_B_XBC:OFF_B_XBC + CONV_B_CH]

    def seq_chains(s):
        def seg(off, width):
            return u_ref[s, :, off:off + width]

        def conv(off, width, w_ref, b_ref):
            acc = b_ref[...] + cbuf[s, R0:R1, off:off + width] * w_ref[CONV_K - 1:CONV_K, :]
            for j in range(CONV_K - 1):
                lo = R0 - CONV_TAIL + j
                acc = acc + cbuf[s, lo:lo + L, off:off + width] * w_ref[j:j + 1, :]
            return acc

        pre = seg(OFF_GATES, W) + gbias_ref[...]
        e = jnp.exp(-jnp.abs(pre))
        l1p = jnp.log1p(e)
        softplus = jnp.maximum(pre, 0.0) + l1p
        logsig = jnp.minimum(pre, 0.0) - l1p
        logdec = jnp.where(is_dt, -jnp.exp(galog_ref[...]) * softplus, jnp.where(is_f, logsig, 0.0))
        gval = jnp.where(lane < GL_DECAY_A, _sigmoid(pre), jnp.where(is_i, pre, softplus))
        gcum = _dot01(tril01, _split3(logdec))
        gsrc = jnp.where(is_cum, gcum * cum_scale, gval)
        src = _split3(gsrc)
        expd = _dot(src[0], exp01) + _dot(src[1], exp01) + _dot(src[2], exp01)
        gsrc_t = gsrc.T

        def expanded(k, p):
            o = (k * NP + p) * W
            return expd[:, o:o + W]

        def row_pair(base, p):
            r = base + 2 * p
            return jnp.concatenate([gsrc_t[r:r + 1, :], gsrc_t[r + 1:r + 2, :]], axis=1)

        qkv = _silu(conv(CB_A, CONV_A_CH, caw_ref, cab_ref))

        def chain_a(p):
            ps = slice(p * W, (p + 1) * W)
            q = qkv[:, p * W:(p + 1) * W]
            k = qkv[:, GROUP_WIDTH + p * W:GROUP_WIDTH + (p + 1) * W]
            v = qkv[:, 2 * GROUP_WIDTH + p * W:2 * GROUP_WIDTH + (p + 1) * W]
            ssq = half_sums(q * q)
            ssk = half_sums(k * k)
            g = expanded(0, p)
            beta = expanded(1, p)
            grow = row_pair(GL_DECAY_A, p)
            yield
            q = q * lax.rsqrt(ssq + EPS) * (HD ** -0.5)
            k = k * lax.rsqrt(ssk + EPS)
            eg = jnp.exp2(g)
            g_last = g[L - 1:L, :]
            kbd = bd(k)
            kk = _nt(k, kbd)
            qk = _nt(q, kbd)
            kw_t = transpose_bf16(k * jnp.exp2(g_last - g))
            dmat = jnp.exp2(jnp.minimum(g - grow, 0.0)) * incl01
            yield
            nmat = (beta * kk) * (dmat * strict01)
            low = -(nmat * lm01[1])
            for b in levels[1:]:
                off = nmat * lm01[b]
                x = off + _dot(off.astype(BF16), bd(low))
                yield
                low = low - x - _dot(low.astype(BF16), bd(x))
                yield
            rv = beta * v
            rk = (beta * eg) * k
            rhs_bd = jnp.concatenate([bd(rv), bd(rk)], axis=1)
            sol = jnp.concatenate([rv, rk], axis=1) + _dot(low.astype(BF16), rhs_bd)
            s0 = sa[s, p]
            qs = _mm(q, s0)
            yield
            unew = sol[:, :W] - _mm(sol[:, W:], s0)
            yield
            o = eg * qs + _mm(qk * dmat, bd(unew))
            sa[s, p] = jnp.exp2(g_last) * s0 + _mm(kw_t, unew) * bd01
            ms = half_sums(o * o) * (1.0 / HD)
            yield
            a_z = seg(OFF_A_Z + p * W, W)
            y_ref[s, :, ps] = o * lax.rsqrt(ms + EPS) * nag_ref[:, ps] * _silu(a_z)

        xbc = _silu(conv(CB_B, CONV_B_CH, cbw_ref, cbb_ref))
        b_all = xbc[:, GROUP_WIDTH:GROUP_WIDTH + W]
        c_all = xbc[:, GROUP_WIDTH + W:GROUP_WIDTH + 2 * W]
        b_t = transpose_bf16(b_all)

        def chain_b(p):
            ps = slice(p * W, (p + 1) * W)
            b_grp = b_all * jnp.where((lane >> 6) == p, 1.0, 0.0)
            cb = _nt(c_all, jnp.concatenate([b_grp, b_grp], axis=0))
            g = expanded(2, p)
            dt = expanded(3, p)
            grow = row_pair(GL_DT_B, p)
            yield
            g_last = g[L - 1:L, :]
            xh = xbc[:, ps]
            v = xh * dt
            s0 = sb[s, p]
            dmat = jnp.exp2(jnp.minimum(g - grow, 0.0)) * incl01
            o = jnp.exp2(g) * _mm(c_all, s0) + _mm(cb * dmat, bd(v))
            upd = _dot(b_t, (v * jnp.exp2(g_last - g)).astype(BF16))
            sb[s, p] = jnp.exp2(g_last) * s0 + upd * jnp.where((r_ww >> 6) == p, 1.0, 0.0)
            yield
            ob = (o + dskip_ref[:, ps] * xh) * _silu(seg(OFF_B_Z + p * W, W))
            ms = _dot((ob * ob).astype(BF16), ones_ww) * (1.0 / W)
            yield
            y_ref[s, :, GROUP_WIDTH + p * W:GROUP_WIDTH + (p + 1) * W] = ob * lax.rsqrt(ms + EPS) * nbg_ref[:, ps]

        fgate = lb + (1.0 - lb) * _sigmoid(seg(OFF_C_F, GROUP_WIDTH))
        gc = _dot01(tril01, _split3(jnp.log(fgate))) * LOG2E
        kc = 1.0 - fgate
        pref = {}
        for b in levels:
            blk = 2 * b
            if blk >= 8:
                pref[b] = jnp.concatenate(
                    [jnp.broadcast_to(gc[i * blk + b - 1:i * blk + b, :], (blk, GROUP_WIDTH)) for i in range(L // blk)],
                    axis=0)
            else:
                acc = gc
                for d in range(blk):
                    sh = d - (b - 1)
                    if sh != 0:
                        acc = jnp.where((r_lg & (blk - 1)) == d, pltpu.roll(gc, sh % L, axis=0), acc)
                pref[b] = acc

        def chain_c(p):
            ps = slice(p * W, (p + 1) * W)
            q = seg(OFF_C_Q + p * W, W)
            v = seg(OFF_C_I + p * W, W)
            k, g = kc[:, ps], gc[:, ps]
            g_last = g[L - 1:L, :]
            kw_t = transpose_bf16(k * jnp.exp2(g_last - g))
            e_col = jnp.exp2(jnp.broadcast_to(g_last, (8, W)).T[:, 0:1])
            qk_diag = half_sums(q * k)
            yield
            att = None
            for b in levels:
                pb = pref[b][:, ps]
                kd = bd(k * jnp.exp2(jnp.minimum(pb - g, 0.0)))
                part = _nt(q * jnp.exp2(jnp.minimum(g - pb, 0.0)), kd) * lm01[b]
                att = part if att is None else att + part
                yield
            s0 = sc[s, p]
            o = _mm(q * jnp.exp2(g), s0) + _mm(att, bd(v)) + qk_diag * v
            sc[s, p] = e_col * s0 + _mm(kw_t, v) * bd01
            ms = half_sums(o * o) * (1.0 / HD)
            yield
            c_g = seg(OFF_C_G + p * W, W)
            y_ref[s, :, 2 * GROUP_WIDTH + p * W:2 * GROUP_WIDTH + (p + 1) * W] = (
                o * lax.rsqrt(ms + EPS) * ncg_ref[:, ps] * _sigmoid(c_g))

        def chain_d(p):
            ps = slice(p * W, (p + 1) * W)
            q = seg(OFF_D_Q + p * W, W)
            k = seg(OFF_D_K + p * W, W) * (HD ** -0.5)
            v = seg(OFF_D_V + p * W, W)
            bcum = expanded(4, p)
            ipre = expanded(5, p)
            m0 = dm[s, p]
            a = ipre - bcum
            arow = row_pair(GL_I_D, p) - row_pair(GL_F_D, p)
            qk = _nt(q, bd(k).astype(BF16))
            cmem = dc[s, p]
            qc = _mm(q, cmem)
            nvec = dn[s, p]
            qn = half_sums(q * nvec)
            cm = a
            sh = 1
            while sh < L:
                cm = jnp.maximum(cm, jnp.where(r_lw >= sh, pltpu.roll(cm, sh, axis=0), NEG))
                sh *= 2
            yield
            m_r = bcum + jnp.maximum(m0, cm)
            pmat = jnp.where(incl_p, jnp.exp(bcum + arow - m_r), 0.0) * qk
            s_init = jnp.exp(bcum + m0 - m_r)
            num = s_init * qc + _mm(pmat, bd(v))
            den = s_init * qn + half_sums(pmat)
            m_last = m_r[L - 1:L, :]
            b_last = bcum[L - 1:L, :]
            scale = jnp.exp(b_last + m0 - m_last)
            kw = k * jnp.exp(b_last - bcum + ipre - m_last)
            kw_t = transpose_bf16(kw)
            yield
            dc[s, p] = scale * cmem + _mm(kw_t, v) * bd01
            dn[s, p] = scale * nvec + jnp.sum(kw, axis=0, keepdims=True)
            dm[s, p] = m_last
            hh = num / jnp.maximum(jnp.abs(den), jnp.exp(-m_r))
            ms = half_sums(hh * hh) * (1.0 / HD)
            yield
            d_o = seg(OFF_D_O + p * W, W)
            y_ref[s, :, 3 * GROUP_WIDTH + p * W:3 * GROUP_WIDTH + (p + 1) * W] = (
                hh * lax.rsqrt(ms + EPS) * ndg_ref[:, ps] * _sigmoid(d_o))

        return ([chain_a(p) for p in range(NP)] + [(START_C, chain_c(p)) for p in range(NP)]
                + [(START_D, chain_d(p)) for p in range(NP)] + [(START_B, chain_b(p)) for p in range(NP)])

    def seq_driver(s):
        chains = seq_chains(s)
        yield
        yield from _interleaved(chains)

    _run_interleaved([(s * SEQ_STAGGER, seq_driver(s)) for s in range(P)])

    tail_a = cbuf[:, R1 - CONV_TAIL:R1, CB_A:CB_A + CONV_A_CH]
    tail_b = cbuf[:, R1 - CONV_TAIL:R1, CB_B:CB_B + CONV_B_CH]
    cbuf[:, R0 - CONV_TAIL:R0, CB_A:CB_A + CONV_A_CH] = tail_a
    cbuf[:, R0 - CONV_TAIL:R0, CB_B:CB_B + CONV_B_CH] = tail_b

    @pl.when(c == NC - 1)
    def _():
        ca_out[...] = tail_a
        cb_out[...] = tail_b
        for s in range(P):
            m_row = jnp.zeros((1, W), f32)
            for p in range(NP):
                h0, h1 = 2 * p, 2 * p + 1
                for ref, dst in ((sa, sa_out), (sc, sc_out), (dc, dc_out)):
                    dst[s, h0] = ref[s, p, 0:HD, 0:HD]
                    dst[s, h1] = ref[s, p, HD:W, HD:W]
                sb_out[s, h0] = sb[s, p, p * HD:(p + 1) * HD, 0:HD]
                sb_out[s, h1] = sb[s, p, p * HD:(p + 1) * HD, HD:W]
                nrow = dn[s, p]
                dn_out[s, h0:h0 + 1, :] = nrow[:, 0:HD]
                dn_out[s, h1:h1 + 1, :] = nrow[:, HD:W]
                mp = dm[s, p]
                m_row = jnp.where(lane == h0, mp[:, 0:1], jnp.where(lane == h1, mp[:, HD:HD + 1], m_row))
            dm_out[s] = m_row


def _mixers(u, states, params, *, layer, P):
    bsz, ttot, _ = u.shape
    T = L = CHUNK
    nc = ttot // T
    H, HD = N_HEADS, HEAD_DIM
    ca, s_a, cb, s_b, s_c, d_c, d_n, d_m = states
    d_m = jnp.pad(d_m, ((0, 0), (0, 0), (0, LANES - H))).reshape(d_m.shape[0], bsz, 1, LANES)

    def per_seq(shape):
        nd = len(shape)
        return pl.BlockSpec((P,) + tuple(shape), lambda b, c: (b,) + (0,) * nd)

    def per_seq_in(x, shape):
        nd = len(shape)
        lidx = layer if x.shape[0] > 1 else 0
        return pl.BlockSpec((None, P) + tuple(shape), lambda b, c: (lidx, b) + (0,) * nd)

    def whole(x):
        nd = x.ndim
        return pl.BlockSpec(x.shape, lambda b, c: (0,) * nd)

    state_dims = [(CONV_TAIL, CONV_A_CH), (H, HD, HD), (CONV_TAIL, CONV_B_CH), (H, HD, HD), (H, HD, HD),
                  (H, HD, HD), (H, HD), (1, LANES)]
    state_specs = [per_seq(d) for d in state_dims]
    state_shapes = [jax.ShapeDtypeStruct((bsz,) + d, F32) for d in state_dims]
    pair_state = (P, H // 2, LANES, LANES)
    scratch = ([pltpu.VMEM((P, ROW0 + L, CONV_A_CH + CONV_B_CH), F32)]
               + [pltpu.VMEM(pair_state, F32) for _ in range(4)]
               + [pltpu.VMEM((P, H // 2, 1, LANES), F32) for _ in range(2)]
               + [pltpu.VMEM((LANES, 6 * (H // 2) * LANES), BF16)])
    outs = pl.pallas_call(
        functools.partial(_mixer_pair_kernel, NC=nc, P=P, layer=layer),
        grid=(bsz // P, nc),
        in_specs=[pl.BlockSpec((P, T, U_COLS), lambda b, c: (b, c, 0))]
                 + [per_seq_in(x, d) for x, d in zip((ca, s_a, cb, s_b, s_c, d_c, d_n, d_m), state_dims)]
                 + [whole(p) for p in params],
        out_specs=[pl.BlockSpec((P, T, D_MODEL), lambda b, c: (b, c, 0))] + state_specs,
        out_shape=[jax.ShapeDtypeStruct((bsz, ttot, D_MODEL), F32)] + state_shapes,
        scratch_shapes=scratch,
        compiler_params=pltpu.CompilerParams(dimension_semantics=("arbitrary", "arbitrary"),
                                             vmem_limit_bytes=VMEM_LIMIT),
        name="mixers_l%d_t%d" % (layer, T),
    )(u, ca, s_a, cb, s_b, s_c, d_c, d_n, d_m, *params)
    y, new = outs[0], list(outs[1:])
    new[7] = new[7].reshape(bsz, LANES)[:, :H]
    return y, tuple(new)


DEC_GATE_ROWS = 128


def _sample_pre_kernel(u_ref, ca_ref, cb_ref, caw_ref, cab_ref, cbw_ref, cbb_ref, gbias_ref, galog_ref, lblog_ref,
                       at_ref, bt_ref, ct_ref, dt_ref, gt_ref, ca_out, cb_out, hist, *, layer, T):
    t = pl.program_id(0)
    na = CONV_A_CH

    @pl.when(t == 0)
    def _():
        hist[0:CONV_TAIL, :, 0:na] = ca_ref[...]
        hist[0:CONV_TAIL, :, na:] = cb_ref[...]

    hist[CONV_TAIL + t, :, 0:na] = u_ref[:, OFF_A_QKV:OFF_A_QKV + CONV_A_CH]
    hist[CONV_TAIL + t, :, na:] = u_ref[:, OFF_B_XBC:OFF_B_XBC + CONV_B_CH]
    w_all = jnp.concatenate([caw_ref[...], cbw_ref[...]], axis=1)
    acc = jnp.concatenate([cab_ref[...], cbb_ref[...]], axis=1)
    for j in range(CONV_K):
        acc = acc + hist[t + j] * w_all[j:j + 1, :]
    act = _silu(acc)
    at_ref[...] = act[:, 0:na].T
    bt_ref[...] = act[:, na:].T

    lane = lax.broadcasted_iota(jnp.int32, (1, LANES), 1)
    pre = u_ref[:, OFF_GATES:OFF_GATES + LANES] + gbias_ref[...]
    e = jnp.exp(-jnp.abs(pre))
    l1p = jnp.log1p(e)
    softplus = jnp.maximum(pre, 0.0) + l1p
    logsig = jnp.minimum(pre, 0.0) - l1p
    is_dt = (lane >= GL_DECAY_A) & (lane < GL_I_D)
    is_i = (lane >= GL_I_D) & (lane < GL_F_D)
    is_f = (lane >= GL_F_D) & (lane < GL_F_D + N_HEADS)
    decay = jnp.exp(-jnp.exp(galog_ref[...]) * softplus)
    gates = jnp.where(lane < GL_DECAY_A, _sigmoid(pre),
                      jnp.where(is_dt, decay, jnp.where(is_i, pre, jnp.where(is_f, logsig, softplus))))
    gt_ref[...] = gates.T

    pl_sm = lblog_ref[...]
    pl_sm = jnp.exp(pl_sm - jnp.max(pl_sm, axis=0, keepdims=True))
    pl_sm = pl_sm / jnp.sum(pl_sm, axis=0, keepdims=True)
    lb = pl_sm[0:1, :]
    for i in range(1, layer + 1):
        lb = lb + pl_sm[i:i + 1, :]
    lb = lb - pl_sm[0:1, :]
    fgate = lb + (1.0 - lb) * _sigmoid(u_ref[:, OFF_C_F:OFF_C_F + GROUP_WIDTH])
    gw = GROUP_WIDTH
    ct_ref[0:gw, :] = u_ref[:, OFF_C_Q:OFF_C_Q + gw].T
    ct_ref[gw:2 * gw, :] = fgate.T
    ct_ref[2 * gw:3 * gw, :] = u_ref[:, OFF_C_I:OFF_C_I + gw].T
    dt_ref[0:gw, :] = u_ref[:, OFF_D_Q:OFF_D_Q + gw].T
    dt_ref[gw:2 * gw, :] = (u_ref[:, OFF_D_K:OFF_D_K + gw] * (HEAD_DIM ** -0.5)).T
    dt_ref[2 * gw:3 * gw, :] = u_ref[:, OFF_D_V:OFF_D_V + gw].T

    @pl.when(t == T - 1)
    def _():
        ca_out[...] = hist[T:T + CONV_TAIL, :, 0:na]
        cb_out[...] = hist[T:T + CONV_TAIL, :, na:]


def _colsum(x):
    return jnp.sum(x, axis=0, keepdims=True)


def _dec_a_kernel(q_ref, k_ref, v_ref, a_ref, beta_ref, s_in, s_out, o_ref, *, T):
    HD, B = HEAD_DIM, LANES
    for t in range(T):
        ln = slice(t * B, (t + 1) * B)
        q, k, v = q_ref[:, ln], k_ref[:, ln], v_ref[:, ln]
        q = q * (lax.rsqrt(_colsum(q * q) + EPS) * (HD ** -0.5))
        k = k * lax.rsqrt(_colsum(k * k) + EPS)
        a, beta = a_ref[:, ln], beta_ref[:, ln]
        src = s_in if t == 0 else s_out
        ks = [jnp.zeros((HD, B), F32), jnp.zeros((HD, B), F32)]
        for dk in range(HD):
            ks[dk & 1] = ks[dk & 1] + k[dk:dk + 1, :] * src[dk]
        unew = beta * (v - a * (ks[0] + ks[1]))
        o = [jnp.zeros((HD, B), F32), jnp.zeros((HD, B), F32)]
        for dk in range(HD):
            sn = a * src[dk] + k[dk:dk + 1, :] * unew
            s_out[dk] = sn
            o[dk & 1] = o[dk & 1] + q[dk:dk + 1, :] * sn
        ot = o[0] + o[1]
        o_ref[:, ln] = ot * lax.rsqrt(_colsum(ot * ot) * (1.0 / HD) + EPS)


def _dec_b_kernel(x_ref, b_ref, c_ref, a_ref, dt_ref, skip_ref, s_in, s_out, o_ref, *, T):
    HD, B = HEAD_DIM, LANES
    for t in range(T):
        ln = slice(t * B, (t + 1) * B)
        x, bm, cm = x_ref[:, ln], b_ref[:, ln], c_ref[:, ln]
        a = a_ref[:, ln]
        xdt = x * dt_ref[:, ln]
        src = s_in if t == 0 else s_out
        o = [jnp.zeros((HD, B), F32), jnp.zeros((HD, B), F32)]
        for n in range(HD):
            sn = a * src[n] + bm[n:n + 1, :] * xdt
            s_out[n] = sn
            o[n & 1] = o[n & 1] + cm[n:n + 1, :] * sn
        o_ref[:, ln] = o[0] + o[1] + skip_ref[...] * x


def _dec_c_kernel(q_ref, f_ref, v_ref, s_in, s_out, o_ref, *, T):
    HD, B = HEAD_DIM, LANES
    for t in range(T):
        ln = slice(t * B, (t + 1) * B)
        q, f, v = q_ref[:, ln], f_ref[:, ln], v_ref[:, ln]
        k = 1.0 - f
        src = s_in if t == 0 else s_out
        o = [jnp.zeros((HD, B), F32), jnp.zeros((HD, B), F32)]
        for dk in range(HD):
            sn = f[dk:dk + 1, :] * src[dk] + k[dk:dk + 1, :] * v
            s_out[dk] = sn
            o[dk & 1] = o[dk & 1] + q[dk:dk + 1, :] * sn
        ot = o[0] + o[1]
        o_ref[:, ln] = ot * lax.rsqrt(_colsum(ot * ot) * (1.0 / HD) + EPS)


def _dec_d_kernel(q_ref, k_ref, v_ref, i_ref, f_ref, c_in, n_in, m_in, c_out, n_out, m_out, o_ref, *, T):
    HD, B = HEAD_DIM, LANES
    m = m_in[...]
    nvec = n_in[...]
    for t in range(T):
        ln = slice(t * B, (t + 1) * B)
        q, k, v = q_ref[:, ln], k_ref[:, ln], v_ref[:, ln]
        ipre, logf = i_ref[:, ln], f_ref[:, ln]
        m_new = jnp.maximum(logf + m, ipre)
        fs = jnp.exp(logf + m - m_new)
        kw = k * jnp.exp(ipre - m_new)
        src = c_in if t == 0 else c_out
        num = [jnp.zeros((HD, B), F32), jnp.zeros((HD, B), F32)]
        for dk in range(HD):
            cn = fs * src[dk] + kw[dk:dk + 1, :] * v
            c_out[dk] = cn
            num[dk & 1] = num[dk & 1] + q[dk:dk + 1, :] * cn
        nvec = fs * nvec + kw
        den = _colsum(q * nvec)
        hh = (num[0] + num[1]) / jnp.maximum(jnp.abs(den), jnp.exp(-m_new))
        o_ref[:, ln] = hh * lax.rsqrt(_colsum(hh * hh) * (1.0 / HD) + EPS)
        m = m_new
    n_out[...] = nvec
    m_out[...] = m


def _sample_post_kernel(oa_ref, ob_ref, oc_ref, od_ref, u_ref, nag_ref, nbg_ref, ncg_ref, ndg_ref, y_ref):
    gw = GROUP_WIDTH
    y_ref[:, 0:gw] = oa_ref[...].T * nag_ref[...] * _silu(u_ref[:, OFF_A_Z:OFF_A_Z + gw])
    ob = ob_ref[...].T * _silu(u_ref[:, OFF_B_Z:OFF_B_Z + gw])
    grp = gw // G_B
    for g in range(G_B):
        sl = slice(g * grp, (g + 1) * grp)
        y_ref[:, gw + g * grp:gw + (g + 1) * grp] = _rms(ob[:, sl]) * nbg_ref[:, sl]
    y_ref[:, 2 * gw:3 * gw] = oc_ref[...].T * ncg_ref[...] * _sigmoid(u_ref[:, OFF_C_G:OFF_C_G + gw])
    y_ref[:, 3 * gw:4 * gw] = od_ref[...].T * ndg_ref[...] * _sigmoid(u_ref[:, OFF_D_O:OFF_D_O + gw])


def _sample_mixers(u, states_t, params, *, layer, T):
    B, H, HD, GW = LANES, N_HEADS, HEAD_DIM, GROUP_WIDTH
    ca_t, sa_t, cb_t, sb_t, sc_t, dc_t, dn_t, dm_t = states_t
    (caw, cab, cbw, cbb, gbias, galog, nag, nbg, ncg, ndg, dskip, lblog) = params
    cp1 = pltpu.CompilerParams(dimension_semantics=("arbitrary",), vmem_limit_bytes=VMEM_LIMIT)
    n = T * B

    def whole(x):
        nd = x.ndim
        return pl.BlockSpec(x.shape, lambda i: (0,) * nd)

    def lane_blk(rows):
        return pl.BlockSpec((rows, B), lambda t: (0, t))

    a_t, b_t, c_t, d_t, g_t, ca_new, cb_new = pl.pallas_call(
        functools.partial(_sample_pre_kernel, layer=layer, T=T),
        grid=(T,),
        in_specs=[pl.BlockSpec((B, U_COLS), lambda t: (t, 0)),
                  pl.BlockSpec((None, CONV_TAIL, B, CONV_A_CH), lambda t: (layer, 0, 0, 0)),
                  pl.BlockSpec((None, CONV_TAIL, B, CONV_B_CH), lambda t: (layer, 0, 0, 0))]
                 + [whole(p) for p in (caw, cab, cbw, cbb, gbias, galog, lblog)],
        out_specs=[lane_blk(CONV_A_CH), lane_blk(CONV_B_CH), lane_blk(3 * GW), lane_blk(3 * GW),
                   lane_blk(DEC_GATE_ROWS),
                   pl.BlockSpec((CONV_TAIL, B, CONV_A_CH), lambda t: (0, 0, 0)),
                   pl.BlockSpec((CONV_TAIL, B, CONV_B_CH), lambda t: (0, 0, 0))],
        out_shape=[jax.ShapeDtypeStruct((CONV_A_CH, n), F32), jax.ShapeDtypeStruct((CONV_B_CH, n), F32),
                   jax.ShapeDtypeStruct((3 * GW, n), F32), jax.ShapeDtypeStruct((3 * GW, n), F32),
                   jax.ShapeDtypeStruct((DEC_GATE_ROWS, n), F32),
                   jax.ShapeDtypeStruct((CONV_TAIL, B, CONV_A_CH), F32),
                   jax.ShapeDtypeStruct((CONV_TAIL, B, CONV_B_CH), F32)],
        scratch_shapes=[pltpu.VMEM((CONV_TAIL + T, B, CONV_A_CH + CONV_B_CH), F32)],
        compiler_params=cp1, name="sample_pre_l%d" % layer,
    )(u, ca_t, cb_t, caw, cab, cbw, cbb, gbias, galog, lblog)

    g3 = g_t.reshape(DEC_GATE_ROWS, 1, n)

    def head_rows(sec):
        return pl.BlockSpec((HD, n), lambda h: (sec * H + h, 0))

    def gate_row(base):
        return pl.BlockSpec((None, 1, n), lambda h: (base + h, 0, 0))

    def state_in(x):
        nd = x.ndim - 2
        return pl.BlockSpec((None, None) + x.shape[2:], lambda h: (layer, h) + (0,) * nd)

    def state_out(shape):
        nd = len(shape)
        return pl.BlockSpec((None,) + tuple(shape), lambda h: (h,) + (0,) * nd)

    st_shape = (HD, HD, B)
    o_spec = pl.BlockSpec((HD, n), lambda h: (h, 0))
    o_shape = jax.ShapeDtypeStruct((GW, n), F32)
    st_out = jax.ShapeDtypeStruct((H,) + st_shape, F32)

    sa_new, oa = pl.pallas_call(
        functools.partial(_dec_a_kernel, T=T), grid=(H,),
        in_specs=[head_rows(0), head_rows(1), head_rows(2), gate_row(GL_DECAY_A), gate_row(GL_BETA), state_in(sa_t)],
        out_specs=[state_out(st_shape), o_spec], out_shape=[st_out, o_shape],
        compiler_params=cp1, name="dec_a_l%d" % layer,
    )(a_t, a_t, a_t, g3, g3, sa_t)

    hpg = H // G_B
    sb_new, ob = pl.pallas_call(
        functools.partial(_dec_b_kernel, T=T), grid=(H,),
        in_specs=[head_rows(0),
                  pl.BlockSpec((HD, n), lambda h: (H + h // hpg, 0)),
                  pl.BlockSpec((HD, n), lambda h: (H + G_B + h // hpg, 0)),
                  gate_row(GL_DT_B), gate_row(GL_DT_B2),
                  pl.BlockSpec((None, 1, B), lambda h: (h, 0, 0)), state_in(sb_t)],
        out_specs=[state_out(st_shape), o_spec], out_shape=[st_out, o_shape],
        compiler_params=cp1, name="dec_b_l%d" % layer,
    )(b_t, b_t, b_t, g3, g3, dskip, sb_t)

    sc_new, oc = pl.pallas_call(
        functools.partial(_dec_c_kernel, T=T), grid=(H,),
        in_specs=[head_rows(0), head_rows(1), head_rows(2), state_in(sc_t)],
        out_specs=[state_out(st_shape), o_spec], out_shape=[st_out, o_shape],
        compiler_params=cp1, name="dec_c_l%d" % layer,
    )(c_t, c_t, c_t, sc_t)

    dm3 = dm_t.reshape(dm_t.shape[0], H, 1, B)
    dc_new, dn_new, dm_new, od = pl.pallas_call(
        functools.partial(_dec_d_kernel, T=T), grid=(H,),
        in_specs=[head_rows(0), head_rows(1), head_rows(2), gate_row(GL_I_D), gate_row(GL_F_D),
                  state_in(dc_t), state_in(dn_t), state_in(dm3)],
        out_specs=[state_out(st_shape), state_out((HD, B)), state_out((1, B)), o_spec],
        out_shape=[st_out, jax.ShapeDtypeStruct((H, HD, B), F32), jax.ShapeDtypeStruct((H, 1, B), F32), o_shape],
        compiler_params=cp1, name="dec_d_l%d" % layer,
    )(d_t, d_t, d_t, g3, g3, dc_t, dn_t, dm3)

    y = pl.pallas_call(
        _sample_post_kernel, grid=(T,),
        in_specs=[lane_blk(GW)] * 4 + [pl.BlockSpec((B, U_COLS), lambda t: (t, 0))]
                 + [whole(p) for p in (nag, nbg, ncg, ndg)],
        out_specs=pl.BlockSpec((B, D_MODEL), lambda t: (t, 0)),
        out_shape=jax.ShapeDtypeStruct((n, D_MODEL), F32),
        compiler_params=cp1, name="sample_post_l%d" % layer,
    )(oa, ob, oc, od, u, nag, nbg, ncg, ndg)
    return y, (ca_new, sa_new, cb_new, sb_new, sc_new, dc_new, dn_new, dm_new.reshape(H, B))


def _prep_w_in_kernel(w_ref, o_ref):
    w = w_ref[...]
    o_ref[:, 0:1024] = w[:, 0:1024].astype(BF16)
    o_ref[:, 1024:1792] = w[:, 1032:1800].astype(BF16)
    o_ref[:, 1792:OFF_GATES] = w[:, 1804:3852].astype(BF16)
    gates = jnp.concatenate([w[:, 1024:1032], w[:, 1800:1804], w[:, 3852:3860], w[:, 1800:1804],
                             jnp.zeros((w.shape[0], U_COLS - OFF_GATES - N_GATE_COLS), F32)], axis=1)
    o_ref[:, OFF_GATES:U_COLS] = gates.astype(BF16)


def _prep_w_in(w_in):
    depth, d, cols = w_in.shape
    rows = 256
    return pl.pallas_call(
        _prep_w_in_kernel,
        grid=(depth, d // rows),
        in_specs=[pl.BlockSpec((None, rows, cols), lambda l, i: (l, i, 0))],
        out_specs=pl.BlockSpec((None, rows, U_COLS), lambda l, i: (l, i, 0)),
        out_shape=jax.ShapeDtypeStruct((depth, d, U_COLS), BF16),
        compiler_params=pltpu.CompilerParams(dimension_semantics=("arbitrary", "arbitrary"),
                                             vmem_limit_bytes=VMEM_LIMIT),
        name="prep_w_in",
    )(w_in)


def _gate_row(parts):
    row = jnp.zeros((LANES,), F32)
    for off, val in parts:
        row = lax.dynamic_update_slice(row, val.astype(F32), (off,))
    return row.reshape(1, LANES)


def _prompt_decoder(x, states, emb_g, emb_b, lb_logits, w_in_p, layer_w):
    bsz, t, _ = x.shape
    n = bsz * t
    h = x.reshape(n, D_MODEL)
    new_states = []
    for l in range(DEPTH):
        (conv_a_w, conv_a_b, a_log_a, dt_bias_a, norm_a_g, conv_b_w, conv_b_b, a_log_b, dt_bias_b,
         d_skip_b, norm_b_g, norm_c_g, i_bias_d, f_bias_d, norm_d_g, w_out, ln1_g, ln1_b, w_up, w_down,
         ln2_g, ln2_b) = [w[l] for w in layer_w]
        if l == 0:
            h, u = _inproj(h, w_in_p, l, emb_g.reshape(1, -1), emb_b.reshape(1, -1), True)
        else:
            (u,) = _inproj(h, w_in_p, l, emb_g.reshape(1, -1), emb_b.reshape(1, -1), False)
        params = (conv_a_w, conv_a_b.reshape(1, -1), conv_b_w, conv_b_b.reshape(1, -1),
                  _gate_row([(GL_DECAY_A, dt_bias_a), (GL_DT_B, dt_bias_b), (GL_I_D, i_bias_d), (GL_F_D, f_bias_d),
                             (GL_DT_B2, dt_bias_b)]),
                  _gate_row([(GL_DECAY_A, a_log_a), (GL_DT_B, a_log_b)]),
                  norm_a_g.reshape(1, -1), norm_b_g.reshape(1, -1), norm_c_g.reshape(1, -1),
                  norm_d_g.reshape(1, -1), jnp.repeat(d_skip_b, HEAD_DIM).reshape(1, -1), lb_logits)
        y, st = _mixers(u.reshape(bsz, t, U_COLS), states, params, layer=l, P=SEQS_PER_STEP_PROMPT)
        new_states.append(st)
        h = _outffn(y.reshape(n, D_MODEL), h, w_out.astype(BF16), ln1_g.reshape(1, -1), ln1_b.reshape(1, -1),
                    w_up.astype(BF16), w_down.astype(BF16), ln2_g.reshape(1, -1), ln2_b.reshape(1, -1))
    return h.reshape(bsz, t, D_MODEL), tuple(jnp.stack(z) for z in zip(*new_states))


def _sample_decoder(x, states, emb_g, emb_b, lb_logits, w_in_p, layer_w):
    bsz, t, _ = x.shape
    n = bsz * t
    h = jnp.transpose(x, (1, 0, 2)).reshape(n, D_MODEL)
    ca, sa, cb, sb, sc, dc, dn, dm = states
    states_t = (jnp.transpose(ca, (0, 2, 1, 3)), jnp.transpose(sa, (0, 2, 3, 4, 1)),
                jnp.transpose(cb, (0, 2, 1, 3)), jnp.transpose(sb, (0, 2, 3, 4, 1)),
                jnp.transpose(sc, (0, 2, 3, 4, 1)), jnp.transpose(dc, (0, 2, 3, 4, 1)),
                jnp.transpose(dn, (0, 2, 3, 1)), jnp.transpose(dm, (0, 2, 1)))
    new_states = []
    for l in range(DEPTH):
        (conv_a_w, conv_a_b, a_log_a, dt_bias_a, norm_a_g, conv_b_w, conv_b_b, a_log_b, dt_bias_b,
         d_skip_b, norm_b_g, norm_c_g, i_bias_d, f_bias_d, norm_d_g, w_out, ln1_g, ln1_b, w_up, w_down,
         ln2_g, ln2_b) = [w[l] for w in layer_w]
        if l == 0:
            h, u = _inproj(h, w_in_p, l, emb_g.reshape(1, -1), emb_b.reshape(1, -1), True)
        else:
            (u,) = _inproj(h, w_in_p, l, emb_g.reshape(1, -1), emb_b.reshape(1, -1), False)
        params = (conv_a_w, conv_a_b.reshape(1, -1), conv_b_w, conv_b_b.reshape(1, -1),
                  _gate_row([(GL_DECAY_A, dt_bias_a), (GL_DT_B, dt_bias_b), (GL_I_D, i_bias_d), (GL_F_D, f_bias_d),
                             (GL_DT_B2, dt_bias_b)]),
                  _gate_row([(GL_DECAY_A, a_log_a), (GL_DT_B, a_log_b)]),
                  norm_a_g.reshape(1, -1), norm_b_g.reshape(1, -1), norm_c_g.reshape(1, -1),
                  norm_d_g.reshape(1, -1), jnp.broadcast_to(d_skip_b[:, None, None], (N_HEADS, 1, LANES)),
                  lb_logits)
        y, st = _sample_mixers(u, states_t, params, layer=l, T=t)
        new_states.append(st)
        h = _outffn(y, h, w_out.astype(BF16), ln1_g.reshape(1, -1), ln1_b.reshape(1, -1),
                    w_up.astype(BF16), w_down.astype(BF16), ln2_g.reshape(1, -1), ln2_b.reshape(1, -1))
    ca_n, sa_n, cb_n, sb_n, sc_n, dc_n, dn_n, dm_n = (jnp.stack(z) for z in zip(*new_states))
    out_states = (jnp.transpose(ca_n, (0, 2, 1, 3)), jnp.transpose(sa_n, (0, 4, 1, 2, 3)),
                  jnp.transpose(cb_n, (0, 2, 1, 3)), jnp.transpose(sb_n, (0, 4, 1, 2, 3)),
                  jnp.transpose(sc_n, (0, 4, 1, 2, 3)), jnp.transpose(dc_n, (0, 4, 1, 2, 3)),
                  jnp.transpose(dn_n, (0, 3, 1, 2)), jnp.transpose(dm_n, (0, 2, 1)))
    return jnp.transpose(h.reshape(t, bsz, D_MODEL), (1, 0, 2)), out_states


def kernel(x_prompt, x_sample, state_a_conv, state_a_ssm, state_b_conv, state_b_ssm, state_c_ssm, state_d_cmem, state_d_nvec, state_d_mstab, emb_ln_g, emb_ln_b, lb_logits_c, w_in, conv_a_w, conv_a_b, a_log_a, dt_bias_a, norm_a_g, conv_b_w, conv_b_b, a_log_b, dt_bias_b, d_skip_b, norm_b_g, norm_c_g, i_bias_d, f_bias_d, norm_d_g, w_out, ln1_g, ln1_b, w_up, w_down, ln2_g, ln2_b):
    layer_w = (conv_a_w, conv_a_b, a_log_a, dt_bias_a, norm_a_g, conv_b_w, conv_b_b, a_log_b, dt_bias_b,
               d_skip_b, norm_b_g, norm_c_g, i_bias_d, f_bias_d, norm_d_g, w_out, ln1_g, ln1_b, w_up, w_down,
               ln2_g, ln2_b)
    sample_states = (state_a_conv, state_a_ssm, state_b_conv, state_b_ssm,
                     state_c_ssm, state_d_cmem, state_d_nvec, state_d_mstab)
    prompt_states = tuple(jnp.zeros((1, x_prompt.shape[0]) + s.shape[2:], F32) for s in sample_states)
    w_in_p = _prep_w_in(w_in)
    y_p, ps = _prompt_decoder(x_prompt, prompt_states, emb_ln_g, emb_ln_b, lb_logits_c, w_in_p, layer_w)
    y_s, ss = _sample_decoder(x_sample, sample_states, emb_ln_g, emb_ln_b, lb_logits_c, w_in_p, layer_w)
    return (y_p, y_s) + ps + ss
```

```python
import functools

import jax
import jax.numpy as jnp
from jax import lax
from jax.experimental import pallas as pl
from jax.experimental.pallas import tpu as pltpu

F32 = jnp.float32
BF16 = jnp.bfloat16

D_MODEL = 1024
DEPTH = 2
N_HEADS = 4
HEAD_DIM = 64
GROUP_WIDTH = N_HEADS * HEAD_DIM
G_B = 2
CONV_K = 4
CHUNK = 64
D_FF = 4 * D_MODEL
EPS = 1e-6
NEG = -1e30
ALPHA = (2 * DEPTH) ** 0.25
LOG2E = 1.4426950408889634

CONV_A_CH = 3 * GROUP_WIDTH
CONV_B_CH = GROUP_WIDTH + 2 * G_B * HEAD_DIM

OFF_A_QKV = 0
OFF_A_Z = 768
OFF_B_Z = 1024
OFF_B_XBC = 1280
OFF_C_Q = 1792
OFF_C_F = 2048
OFF_C_I = 2304
OFF_C_G = 2560
OFF_D_Q = 2816
OFF_D_K = 3072
OFF_D_V = 3328
OFF_D_O = 3584
OFF_GATES = 3840
U_COLS = 3968
LANES = 128
GL_BETA = 0
GL_DECAY_A = 4
GL_DT_B = 8
GL_I_D = 12
GL_F_D = 16
GL_DT_B2 = 20
N_GATE_COLS = 24

CONV_TAIL = CONV_K - 1
ROW0 = 8

VMEM_LIMIT = 56 * 1024 * 1024
TOKEN_TILE = 512
SUB_TILE = 256
SEQS_PER_STEP_PROMPT = 4
START_C = 3
START_D = 8
START_B = 11
SEQ_STAGGER = 0


def _layernorm(x, g, b):
    mu = jnp.mean(x, axis=-1, keepdims=True)
    xc = x - mu
    var = jnp.mean(xc * xc, axis=-1, keepdims=True)
    return xc * lax.rsqrt(var + EPS) * g + b


def _sigmoid(x):
    return 1.0 / (1.0 + jnp.exp(-x))


def _silu(x):
    return x * _sigmoid(x)


def _dot(a, b):
    return jnp.dot(a, b, preferred_element_type=F32)


def _mm(a, b):
    return _dot(a.astype(BF16), b.astype(BF16))


def _nt(a, b):
    return lax.dot_general(a.astype(BF16), b.astype(BF16), (((1,), (1,)), ((), ())),
                           preferred_element_type=F32)


def _split3(x):
    hi = x.astype(BF16)
    r1 = x - hi.astype(F32)
    mid = r1.astype(BF16)
    r2 = r1 - mid.astype(F32)
    return hi, mid, r2.astype(BF16)


def _dot01(m01, parts):
    acc = _dot(m01, parts[0])
    for p in parts[1:]:
        acc = acc + _dot(m01, p)
    return acc


def _rms(x):
    return x * lax.rsqrt(jnp.mean(x * x, axis=-1, keepdims=True) + EPS)


def _levels(L):
    out, b = [], 1
    while 2 * b <= L:
        out.append(b)
        b *= 2
    return out


def _interleaved(chains):
    pending = [ch if isinstance(ch, tuple) else (0, ch) for ch in chains]
    rnd = 0
    while pending:
        alive = []
        for start, ch in pending:
            if start > rnd:
                alive.append((start, ch))
                continue
            try:
                next(ch)
                alive.append((start, ch))
            except StopIteration:
                pass
        pending = alive
        rnd += 1
        yield


def _run_interleaved(chains):
    for _ in _interleaved(chains):
        pass


def _row_parts(n_rows):
    step = min(SUB_TILE, n_rows)
    return [slice(r, r + step) for r in range(0, n_rows, step)]


def _inproj_kernel(x_ref, w_ref, g_ref, b_ref, *out_refs, apply_ln):
    def part(rows):
        x = x_ref[rows, :]
        if apply_ln:
            x = _layernorm(x, g_ref[...], b_ref[...])
            out_refs[0][rows, :] = x
        yield
        out_refs[-1][rows, :] = _dot(x.astype(BF16), w_ref[...])

    _run_interleaved([part(rows) for rows in _row_parts(x_ref.shape[0])])


def _inproj(x, w_in_p, layer, g, b, apply_ln):
    n = x.shape[0]
    tm = min(TOKEN_TILE, n)
    grid = (n // tm,)
    const = lambda i: (0, 0)
    out_shape = [jax.ShapeDtypeStruct((n, U_COLS), F32)]
    out_specs = [pl.BlockSpec((tm, U_COLS), lambda i: (i, 0))]
    if apply_ln:
        out_shape = [jax.ShapeDtypeStruct((n, D_MODEL), F32)] + out_shape
        out_specs = [pl.BlockSpec((tm, D_MODEL), lambda i: (i, 0))] + out_specs
    return pl.pallas_call(
        functools.partial(_inproj_kernel, apply_ln=apply_ln),
        grid=grid,
        in_specs=[pl.BlockSpec((tm, D_MODEL), lambda i: (i, 0)),
                  pl.BlockSpec((None, D_MODEL, U_COLS), lambda i: (layer, 0, 0), pipeline_mode=pl.Buffered(1)),
                  pl.BlockSpec((1, D_MODEL), const),
                  pl.BlockSpec((1, D_MODEL), const)],
        out_specs=out_specs,
        out_shape=out_shape,
        compiler_params=pltpu.CompilerParams(dimension_semantics=("arbitrary",),
                                             vmem_limit_bytes=VMEM_LIMIT),
        name="inproj_ln" if apply_ln else "inproj",
    )(x, w_in_p, g, b)


def _outffn_kernel(mix_ref, x_ref, wo_ref, g1_ref, b1_ref, wu_ref, wd_ref, g2_ref, b2_ref, o_ref):
    def part(rows):
        m = _dot(mix_ref[rows, :].astype(BF16), wo_ref[...])
        yield
        h1 = _layernorm(ALPHA * x_ref[rows, :] + m, g1_ref[...], b1_ref[...])
        up = _dot(h1.astype(BF16), wu_ref[...])
        yield
        act = jnp.square(jnp.maximum(up, 0.0))
        ff = _dot(act.astype(BF16), wd_ref[...])
        yield
        o_ref[rows, :] = _layernorm(ALPHA * h1 + ff, g2_ref[...], b2_ref[...])

    _run_interleaved([part(rows) for rows in _row_parts(mix_ref.shape[0])])


def _outffn(mix, x, wo, g1, b1, wu, wd, g2, b2):
    n = x.shape[0]
    tm = min(TOKEN_TILE, n)
    const = lambda i: (0, 0)
    tile = pl.BlockSpec((tm, D_MODEL), lambda i: (i, 0))
    row = pl.BlockSpec((1, D_MODEL), const)
    return pl.pallas_call(
        _outffn_kernel,
        grid=(n // tm,),
        in_specs=[tile, tile,
                  pl.BlockSpec((D_MODEL, D_MODEL), const, pipeline_mode=pl.Buffered(1)),
                  row, row,
                  pl.BlockSpec((D_MODEL, D_FF), const, pipeline_mode=pl.Buffered(1)),
                  pl.BlockSpec((D_FF, D_MODEL), const, pipeline_mode=pl.Buffered(1)),
                  row, row],
        out_specs=tile,
        out_shape=jax.ShapeDtypeStruct((n, D_MODEL), F32),
        compiler_params=pltpu.CompilerParams(dimension_semantics=("arbitrary",),
                                             vmem_limit_bytes=VMEM_LIMIT),
        name="outffn",
    )(mix, x, wo, g1, b1, wu, wd, g2, b2)


def _mixer_pair_kernel(u_ref, ca_in, sa_in, cb_in, sb_in, sc_in, dc_in, dn_in, dm_in,
                       caw_ref, cab_ref, cbw_ref, cbb_ref, gbias_ref, galog_ref,
                       nag_ref, nbg_ref, ncg_ref, ndg_ref, dskip_ref, lblog_ref,
                       y_ref, ca_out, sa_out, cb_out, sb_out, sc_out, dc_out, dn_out, dm_out,
                       cbuf, sa, sb, sc, dc, dn, dm, exp01_s,
                       *, NC, P, layer):
    c = pl.program_id(1)
    H, HD, L, W = N_HEADS, HEAD_DIM, CHUNK, LANES
    NP = H // 2
    R0, R1 = ROW0, ROW0 + L
    CB_A, CB_B = 0, CONV_A_CH
    f32 = F32

    def iota(shape, d):
        return lax.broadcasted_iota(jnp.int32, shape, d)

    def one_bf16(mask):
        return jnp.where(mask, 1.0, 0.0).astype(BF16)

    r_ll, c_ll = iota((L, L), 0), iota((L, L), 1)
    tril01 = one_bf16(r_ll >= c_ll)
    r_lw, c_lw = iota((L, W), 0), iota((L, W), 1)
    r_lg = iota((L, GROUP_WIDTH), 0)
    j_lw = c_lw & (HD - 1)
    incl_p = r_lw >= j_lw
    incl01 = jnp.where(incl_p, 1.0, 0.0)
    strict01 = jnp.where(r_lw > j_lw, 1.0, 0.0)
    levels = _levels(L)
    lm01 = {}
    for b in levels:
        sh = b.bit_length() - 1
        rb, cb = r_lw >> sh, j_lw >> sh
        lm01[b] = jnp.where((rb - cb == 1) & ((cb & 1) == 0), 1.0, 0.0)
    r_ww, c_ww = iota((W, W), 0), iota((W, W), 1)
    bdiag = (r_ww >> 6) == (c_ww >> 6)
    bd01 = jnp.where(bdiag, 1.0, 0.0)
    ones_bd = bd01.astype(BF16)
    ones_ww = jnp.ones((W, W), BF16)
    lane = iota((1, W), 1)
    lo_half = lane < HD
    is_dt = (lane >= GL_DECAY_A) & (lane < GL_I_D)
    is_i = (lane >= GL_I_D) & (lane < GL_F_D)
    is_f = (lane >= GL_F_D) & (lane < GL_F_D + H)
    is_cum = is_dt | is_f
    cum_scale = jnp.where(is_dt, LOG2E, 1.0)

    @pl.when((pl.program_id(0) == 0) & (c == 0))
    def _():
        exp_bases = (GL_DECAY_A, GL_BETA, GL_DT_B, GL_DT_B2, GL_F_D, GL_I_D)
        for i, (base, p) in enumerate([(base, p) for base in exp_bases for p in range(NP)]):
            exp01_s[:, i * W:(i + 1) * W] = one_bf16(r_ww == (base + 2 * p + (c_ww >> 6)))

    exp01 = exp01_s[...]

    def bd(x):
        xb = x.astype(BF16)
        return jnp.concatenate([xb, xb], axis=0) * ones_bd

    def half_sums(x):
        return _dot(x.astype(BF16), ones_bd)

    def transpose_bf16(x):
        return x.T.astype(BF16)

    pl_sm = lblog_ref[...]
    pl_sm = jnp.exp(pl_sm - jnp.max(pl_sm, axis=0, keepdims=True))
    pl_sm = pl_sm / jnp.sum(pl_sm, axis=0, keepdims=True)
    lb = pl_sm[0:1, :]
    for i in range(1, layer + 1):
        lb = lb + pl_sm[i:i + 1, :]
    lb = lb - pl_sm[0:1, :]

    @pl.when(c == 0)
    def _():
        cbuf[:, R0 - CONV_TAIL:R0, CB_A:CB_A + CONV_A_CH] = ca_in[...]
        cbuf[:, R0 - CONV_TAIL:R0, CB_B:CB_B + CONV_B_CH] = cb_in[...]
        zero = jnp.zeros((W, W), f32)
        for s in range(P):
            m_row = dm_in[s]
            for p in range(NP):
                h0, h1 = 2 * p, 2 * p + 1
                for ref, src in ((sa, sa_in), (sc, sc_in), (dc, dc_in)):
                    ref[s, p] = zero
                    ref[s, p, 0:HD, 0:HD] = src[s, h0]
                    ref[s, p, HD:W, HD:W] = src[s, h1]
                sb[s, p] = zero
                sb[s, p, p * HD:(p + 1) * HD, 0:HD] = sb_in[s, h0]
                sb[s, p, p * HD:(p + 1) * HD, HD:W] = sb_in[s, h1]
                dn[s, p] = jnp.concatenate([dn_in[s, h0:h0 + 1, :], dn_in[s, h1:h1 + 1, :]], axis=1)
                dm[s, p] = jnp.where(lo_half, m_row[:, h0:h0 + 1], m_row[:, h1:h1 + 1])

    cbuf[:, R0:R1, CB_A:CB_A + CONV_A_CH] = u_ref[:, :, OFF_A_QKV:OFF_A_QKV + CONV_A_CH]
    cbuf[:, R0:R1, CB_B:CB_B + CONV_B_CH] = u_ref[:, :, OFF_B_XBC:OFF_B_XBC + CONV_B_CH]

    def seq_chains(s):
        def seg(off, width):
            return u_ref[s, :, off:off + width]

        def conv(off, width, w_ref, b_ref):
            acc = b_ref[...] + cbuf[s, R0:R1, off:off + width] * w_ref[CONV_K - 1:CONV_K, :]
            for j in range(CONV_K - 1):
                lo = R0 - CONV_TAIL + j
                acc = acc + cbuf[s, lo:lo + L, off:off + width] * w_ref[j:j + 1, :]
            return acc

        pre = seg(OFF_GATES, W) + gbias_ref[...]
        e = jnp.exp(-jnp.abs(pre))
        l1p = jnp.log1p(e)
        softplus = jnp.maximum(pre, 0.0) + l1p
        logsig = jnp.minimum(pre, 0.0) - l1p
        logdec = jnp.where(is_dt, -jnp.exp(galog_ref[...]) * softplus, jnp.where(is_f, logsig, 0.0))
        gval = jnp.where(lane < GL_DECAY_A, _sigmoid(pre), jnp.where(is_i, pre, softplus))
        gcum = _dot01(tril01, _split3(logdec))
        gsrc = jnp.where(is_cum, gcum * cum_scale, gval)
        src = _split3(gsrc)
        expd = _dot(src[0], exp01) + _dot(src[1], exp01) + _dot(src[2], exp01)
        gsrc_t = gsrc.T

        def expanded(k, p):
            o = (k * NP + p) * W
            return expd[:, o:o + W]

        def row_pair(base, p):
            r = base + 2 * p
            return jnp.concatenate([gsrc_t[r:r + 1, :], gsrc_t[r + 1:r + 2, :]], axis=1)

        qkv = _silu(conv(CB_A, CONV_A_CH, caw_ref, cab_ref))

        def chain_a(p):
            ps = slice(p * W, (p + 1) * W)
            q = qkv[:, p * W:(p + 1) * W]
            k = qkv[:, GROUP_WIDTH + p * W:GROUP_WIDTH + (p + 1) * W]
            v = qkv[:, 2 * GROUP_WIDTH + p * W:2 * GROUP_WIDTH + (p + 1) * W]
            ssq = half_sums(q * q)
            ssk = half_sums(k * k)
            g = expanded(0, p)
            beta = expanded(1, p)
            grow = row_pair(GL_DECAY_A, p)
            yield
            q = q * lax.rsqrt(ssq + EPS) * (HD ** -0.5)
            k = k * lax.rsqrt(ssk + EPS)
            eg = jnp.exp2(g)
            g_last = g[L - 1:L, :]
            kbd = bd(k)
            kk = _nt(k, kbd)
            qk = _nt(q, kbd)
            kw_t = transpose_bf16(k * jnp.exp2(g_last - g))
            dmat = jnp.exp2(jnp.minimum(g - grow, 0.0)) * incl01
            yield
            nmat = (beta * kk) * (dmat * strict01)
            low = -(nmat * lm01[1])
            for b in levels[1:]:
                off = nmat * lm01[b]
                x = off + _dot(off.astype(BF16), bd(low))
                yield
                low = low - x - _dot(low.astype(BF16), bd(x))
                yield
            rv = beta * v
            rk = (beta * eg) * k
            rhs_bd = jnp.concatenate([bd(rv), bd(rk)], axis=1)
            sol = jnp.concatenate([rv, rk], axis=1) + _dot(low.astype(BF16), rhs_bd)
            s0 = sa[s, p]
            qs = _mm(q, s0)
            yield
            unew = sol[:, :W] - _mm(sol[:, W:], s0)
            yield
            o = eg * qs + _mm(qk * dmat, bd(unew))
            sa[s, p] = jnp.exp2(g_last) * s0 + _mm(kw_t, unew) * bd01
            ms = half_sums(o * o) * (1.0 / HD)
            yield
            a_z = seg(OFF_A_Z + p * W, W)
            y_ref[s, :, ps] = o * lax.rsqrt(ms + EPS) * nag_ref[:, ps] * _silu(a_z)

        xbc = _silu(conv(CB_B, CONV_B_CH, cbw_ref, cbb_ref))
        b_all = xbc[:, GROUP_WIDTH:GROUP_WIDTH + W]
        c_all = xbc[:, GROUP_WIDTH + W:GROUP_WIDTH + 2 * W]
        b_t = transpose_bf16(b_all)

        def chain_b(p):
            ps = slice(p * W, (p + 1) * W)
            b_grp = b_all * jnp.where((lane >> 6) == p, 1.0, 0.0)
            cb = _nt(c_all, jnp.concatenate([b_grp, b_grp], axis=0))
            g = expanded(2, p)
            dt = expanded(3, p)
            grow = row_pair(GL_DT_B, p)
            yield
            g_last = g[L - 1:L, :]
            xh = xbc[:, ps]
            v = xh * dt
            s0 = sb[s, p]
            dmat = jnp.exp2(jnp.minimum(g - grow, 0.0)) * incl01
            o = jnp.exp2(g) * _mm(c_all, s0) + _mm(cb * dmat, bd(v))
            upd = _dot(b_t, (v * jnp.exp2(g_last - g)).astype(BF16))
            sb[s, p] = jnp.exp2(g_last) * s0 + upd * jnp.where((r_ww >> 6) == p, 1.0, 0.0)
            yield
            ob = (o + dskip_ref[:, ps] * xh) * _silu(seg(OFF_B_Z + p * W, W))
            ms = _dot((ob * ob).astype(BF16), ones_ww) * (1.0 / W)
            yield
            y_ref[s, :, GROUP_WIDTH + p * W:GROUP_WIDTH + (p + 1) * W] = ob * lax.rsqrt(ms + EPS) * nbg_ref[:, ps]

        fgate = lb + (1.0 - lb) * _sigmoid(seg(OFF_C_F, GROUP_WIDTH))
        gc = _dot01(tril01, _split3(jnp.log(fgate))) * LOG2E
        kc = 1.0 - fgate
        pref = {}
        for b in levels:
            blk = 2 * b
            if blk >= 8:
                pref[b] = jnp.concatenate(
                    [jnp.broadcast_to(gc[i * blk + b - 1:i * blk + b, :], (blk, GROUP_WIDTH)) for i in range(L // blk)],
                    axis=0)
            else:
                acc = gc
                for d in range(blk):
                    sh = d - (b - 1)
                    if sh != 0:
                        acc = jnp.where((r_lg & (blk - 1)) == d, pltpu.roll(gc, sh % L, axis=0), acc)
                pref[b] = acc

        def chain_c(p):
            ps = slice(p * W, (p + 1) * W)
            q = seg(OFF_C_Q + p * W, W)
            v = seg(OFF_C_I + p * W, W)
            k, g = kc[:, ps], gc[:, ps]
            g_last = g[L - 1:L, :]
            kw_t = transpose_bf16(k * jnp.exp2(g_last - g))
            e_col = jnp.exp2(jnp.broadcast_to(g_last, (8, W)).T[:, 0:1])
            qk_diag = half_sums(q * k)
            yield
            att = None
            for b in levels:
                pb = pref[b][:, ps]
                kd = bd(k * jnp.exp2(jnp.minimum(pb - g, 0.0)))
                part = _nt(q * jnp.exp2(jnp.minimum(g - pb, 0.0)), kd) * lm01[b]
                att = part if att is None else att + part
                yield
            s0 = sc[s, p]
            o = _mm(q * jnp.exp2(g), s0) + _mm(att, bd(v)) + qk_diag * v
            sc[s, p] = e_col * s0 + _mm(kw_t, v) * bd01
            ms = half_sums(o * o) * (1.0 / HD)
            yield
            c_g = seg(OFF_C_G + p * W, W)
            y_ref[s, :, 2 * GROUP_WIDTH + p * W:2 * GROUP_WIDTH + (p + 1) * W] = (
                o * lax.rsqrt(ms + EPS) * ncg_ref[:, ps] * _sigmoid(c_g))

        def chain_d(p):
            ps = slice(p * W, (p + 1) * W)
            q = seg(OFF_D_Q + p * W, W)
            k = seg(OFF_D_K + p * W, W) * (HD ** -0.5)
            v = seg(OFF_D_V + p * W, W)
            bcum = expanded(4, p)
            ipre = expanded(5, p)
            m0 = dm[s, p]
            a = ipre - bcum
            arow = row_pair(GL_I_D, p) - row_pair(GL_F_D, p)
            qk = _nt(q, bd(k).astype(BF16))
            cmem = dc[s, p]
            qc = _mm(q, cmem)
            nvec = dn[s, p]
            qn = half_sums(q * nvec)
            cm = a
            sh = 1
            while sh < L:
                cm = jnp.maximum(cm, jnp.where(r_lw >= sh, pltpu.roll(cm, sh, axis=0), NEG))
                sh *= 2
            yield
            m_r = bcum + jnp.maximum(m0, cm)
            pmat = jnp.where(incl_p, jnp.exp(bcum + arow - m_r), 0.0) * qk
            s_init = jnp.exp(bcum + m0 - m_r)
            num = s_init * qc + _mm(pmat, bd(v))
            den = s_init * qn + half_sums(pmat)
            m_last = m_r[L - 1:L, :]
            b_last = bcum[L - 1:L, :]
            scale = jnp.exp(b_last + m0 - m_last)
            kw = k * jnp.exp(b_last - bcum + ipre - m_last)
            kw_t = transpose_bf16(kw)
            yield
            dc[s, p] = scale * cmem + _mm(kw_t, v) * bd01
            dn[s, p] = scale * nvec + jnp.sum(kw, axis=0, keepdims=True)
            dm[s, p] = m_last
            hh = num / jnp.maximum(jnp.abs(den), jnp.exp(-m_r))
            ms = half_sums(hh * hh) * (1.0 / HD)
            yield
            d_o = seg(OFF_D_O + p * W, W)
            y_ref[s, :, 3 * GROUP_WIDTH + p * W:3 * GROUP_WIDTH + (p + 1) * W] = (
                hh * lax.rsqrt(ms + EPS) * ndg_ref[:, ps] * _sigmoid(d_o))

        return ([chain_a(p) for p in range(NP)] + [(START_C, chain_c(p)) for p in range(NP)]
                + [(START_D, chain_d(p)) for p in range(NP)] + [(START_B, chain_b(p)) for p in range(NP)])

    def seq_driver(s):
        chains = seq_chains(s)
        yield
        yield from _interleaved(chains)

    _run_interleaved([(s * SEQ_STAGGER, seq_driver(s)) for s in range(P)])

    tail_a = cbuf[:, R1 - CONV_TAIL:R1, CB_A:CB_A + CONV_A_CH]
    tail_b = cbuf[:, R1 - CONV_TAIL:R1, CB_B:CB_B + CONV_B_CH]
    cbuf[:, R0 - CONV_TAIL:R0, CB_A:CB_A + CONV_A_CH] = tail_a
    cbuf[:, R0 - CONV_TAIL:R0, CB_B:CB_B + CONV_B_CH] = tail_b

    @pl.when(c == NC - 1)
    def _():
        ca_out[...] = tail_a
        cb_out[...] = tail_b
        for s in range(P):
            m_row = jnp.zeros((1, W), f32)
            for p in range(NP):
                h0, h1 = 2 * p, 2 * p + 1
                for ref, dst in ((sa, sa_out), (sc, sc_out), (dc, dc_out)):
                    dst[s, h0] = ref[s, p, 0:HD, 0:HD]
                    dst[s, h1] = ref[s, p, HD:W, HD:W]
                sb_out[s, h0] = sb[s, p, p * HD:(p + 1) * HD, 0:HD]
                sb_out[s, h1] = sb[s, p, p * HD:(p + 1) * HD, HD:W]
                nrow = dn[s, p]
                dn_out[s, h0:h0 + 1, :] = nrow[:, 0:HD]
                dn_out[s, h1:h1 + 1, :] = nrow[:, HD:W]
                mp = dm[s, p]
                m_row = jnp.where(lane == h0, mp[:, 0:1], jnp.where(lane == h1, mp[:, HD:HD + 1], m_row))
            dm_out[s] = m_row


def _mixers(u, states, params, *, layer, P):
    bsz, ttot, _ = u.shape
    T = L = CHUNK
    nc = ttot // T
    H, HD = N_HEADS, HEAD_DIM
    ca, s_a, cb, s_b, s_c, d_c, d_n, d_m = states
    d_m = jnp.pad(d_m, ((0, 0), (0, 0), (0, LANES - H))).reshape(d_m.shape[0], bsz, 1, LANES)

    def per_seq(shape):
        nd = len(shape)
        return pl.BlockSpec((P,) + tuple(shape), lambda b, c: (b,) + (0,) * nd)

    def per_seq_in(x, shape):
        nd = len(shape)
        lidx = layer if x.shape[0] > 1 else 0
        return pl.BlockSpec((None, P) + tuple(shape), lambda b, c: (lidx, b) + (0,) * nd)

    def whole(x):
        nd = x.ndim
        return pl.BlockSpec(x.shape, lambda b, c: (0,) * nd)

    state_dims = [(CONV_TAIL, CONV_A_CH), (H, HD, HD), (CONV_TAIL, CONV_B_CH), (H, HD, HD), (H, HD, HD),
                  (H, HD, HD), (H, HD), (1, LANES)]
    state_specs = [per_seq(d) for d in state_dims]
    state_shapes = [jax.ShapeDtypeStruct((bsz,) + d, F32) for d in state_dims]
    pair_state = (P, H // 2, LANES, LANES)
    scratch = ([pltpu.VMEM((P, ROW0 + L, CONV_A_CH + CONV_B_CH), F32)]
               + [pltpu.VMEM(pair_state, F32) for _ in range(4)]
               + [pltpu.VMEM((P, H // 2, 1, LANES), F32) for _ in range(2)]
               + [pltpu.VMEM((LANES, 6 * (H // 2) * LANES), BF16)])
    outs = pl.pallas_call(
        functools.partial(_mixer_pair_kernel, NC=nc, P=P, layer=layer),
        grid=(bsz // P, nc),
        in_specs=[pl.BlockSpec((P, T, U_COLS), lambda b, c: (b, c, 0))]
                 + [per_seq_in(x, d) for x, d in zip((ca, s_a, cb, s_b, s_c, d_c, d_n, d_m), state_dims)]
                 + [whole(p) for p in params],
        out_specs=[pl.BlockSpec((P, T, D_MODEL), lambda b, c: (b, c, 0))] + state_specs,
        out_shape=[jax.ShapeDtypeStruct((bsz, ttot, D_MODEL), F32)] + state_shapes,
        scratch_shapes=scratch,
        compiler_params=pltpu.CompilerParams(dimension_semantics=("arbitrary", "arbitrary"),
                                             vmem_limit_bytes=VMEM_LIMIT),
        name="mixers_l%d_t%d" % (layer, T),
    )(u, ca, s_a, cb, s_b, s_c, d_c, d_n, d_m, *params)
    y, new = outs[0], list(outs[1:])
    new[7] = new[7].reshape(bsz, LANES)[:, :H]
    return y, tuple(new)


DEC_GATE_ROWS = 128


def _sample_pre_kernel(u_ref, ca_ref, cb_ref, caw_ref, cab_ref, cbw_ref, cbb_ref, gbias_ref, galog_ref, lblog_ref,
                       at_ref, bt_ref, ct_ref, dt_ref, gt_ref, ca_out, cb_out, hist, *, layer, T):
    t = pl.program_id(0)
    na = CONV_A_CH

    @pl.when(t == 0)
    def _():
        hist[0:CONV_TAIL, :, 0:na] = ca_ref[...]
        hist[0:CONV_TAIL, :, na:] = cb_ref[...]

    hist[CONV_TAIL + t, :, 0:na] = u_ref[:, OFF_A_QKV:OFF_A_QKV + CONV_A_CH]
    hist[CONV_TAIL + t, :, na:] = u_ref[:, OFF_B_XBC:OFF_B_XBC + CONV_B_CH]
    w_all = jnp.concatenate([caw_ref[...], cbw_ref[...]], axis=1)
    acc = jnp.concatenate([cab_ref[...], cbb_ref[...]], axis=1)
    for j in range(CONV_K):
        acc = acc + hist[t + j] * w_all[j:j + 1, :]
    act = _silu(acc)
    at_ref[...] = act[:, 0:na].T
    bt_ref[...] = act[:, na:].T

    lane = lax.broadcasted_iota(jnp.int32, (1, LANES), 1)
    pre = u_ref[:, OFF_GATES:OFF_GATES + LANES] + gbias_ref[...]
    e = jnp.exp(-jnp.abs(pre))
    l1p = jnp.log1p(e)
    softplus = jnp.maximum(pre, 0.0) + l1p
    logsig = jnp.minimum(pre, 0.0) - l1p
    is_dt = (lane >= GL_DECAY_A) & (lane < GL_I_D)
    is_i = (lane >= GL_I_D) & (lane < GL_F_D)
    is_f = (lane >= GL_F_D) & (lane < GL_F_D + N_HEADS)
    decay = jnp.exp(-jnp.exp(galog_ref[...]) * softplus)
    gates = jnp.where(lane < GL_DECAY_A, _sigmoid(pre),
                      jnp.where(is_dt, decay, jnp.where(is_i, pre, jnp.where(is_f, logsig, softplus))))
    gt_ref[...] = gates.T

    pl_sm = lblog_ref[...]
    pl_sm = jnp.exp(pl_sm - jnp.max(pl_sm, axis=0, keepdims=True))
    pl_sm = pl_sm / jnp.sum(pl_sm, axis=0, keepdims=True)
    lb = pl_sm[0:1, :]
    for i in range(1, layer + 1):
        lb = lb + pl_sm[i:i + 1, :]
    lb = lb - pl_sm[0:1, :]
    fgate = lb + (1.0 - lb) * _sigmoid(u_ref[:, OFF_C_F:OFF_C_F + GROUP_WIDTH])
    gw = GROUP_WIDTH
    ct_ref[0:gw, :] = u_ref[:, OFF_C_Q:OFF_C_Q + gw].T
    ct_ref[gw:2 * gw, :] = fgate.T
    ct_ref[2 * gw:3 * gw, :] = u_ref[:, OFF_C_I:OFF_C_I + gw].T
    dt_ref[0:gw, :] = u_ref[:, OFF_D_Q:OFF_D_Q + gw].T
    dt_ref[gw:2 * gw, :] = (u_ref[:, OFF_D_K:OFF_D_K + gw] * (HEAD_DIM ** -0.5)).T
    dt_ref[2 * gw:3 * gw, :] = u_ref[:, OFF_D_V:OFF_D_V + gw].T

    @pl.when(t == T - 1)
    def _():
        ca_out[...] = hist[T:T + CONV_TAIL, :, 0:na]
        cb_out[...] = hist[T:T + CONV_TAIL, :, na:]


def _colsum(x):
    return jnp.sum(x, axis=0, keepdims=True)


def _dec_a_kernel(q_ref, k_ref, v_ref, a_ref, beta_ref, s_in, s_out, o_ref, *, T):
    HD, B = HEAD_DIM, LANES
    for t in range(T):
        ln = slice(t * B, (t + 1) * B)
        q, k, v = q_ref[:, ln], k_ref[:, ln], v_ref[:, ln]
        q = q * (lax.rsqrt(_colsum(q * q) + EPS) * (HD ** -0.5))
        k = k * lax.rsqrt(_colsum(k * k) + EPS)
        a, beta = a_ref[:, ln], beta_ref[:, ln]
        src = s_in if t == 0 else s_out
        ks = [jnp.zeros((HD, B), F32), jnp.zeros((HD, B), F32)]
        for dk in range(HD):
            ks[dk & 1] = ks[dk & 1] + k[dk:dk + 1, :] * src[dk]
        unew = beta * (v - a * (ks[0] + ks[1]))
        o = [jnp.zeros((HD, B), F32), jnp.zeros((HD, B), F32)]
        for dk in range(HD):
            sn = a * src[dk] + k[dk:dk + 1, :] * unew
            s_out[dk] = sn
            o[dk & 1] = o[dk & 1] + q[dk:dk + 1, :] * sn
        ot = o[0] + o[1]
        o_ref[:, ln] = ot * lax.rsqrt(_colsum(ot * ot) * (1.0 / HD) + EPS)


def _dec_b_kernel(x_ref, b_ref, c_ref, a_ref, dt_ref, skip_ref, s_in, s_out, o_ref, *, T):
    HD, B = HEAD_DIM, LANES
    for t in range(T):
        ln = slice(t * B, (t + 1) * B)
        x, bm, cm = x_ref[:, ln], b_ref[:, ln], c_ref[:, ln]
        a = a_ref[:, ln]
        xdt = x * dt_ref[:, ln]
        src = s_in if t == 0 else s_out
        o = [jnp.zeros((HD, B), F32), jnp.zeros((HD, B), F32)]
        for n in range(HD):
            sn = a * src[n] + bm[n:n + 1, :] * xdt
            s_out[n] = sn
            o[n & 1] = o[n & 1] + cm[n:n + 1, :] * sn
        o_ref[:, ln] = o[0] + o[1] + skip_ref[...] * x


def _dec_c_kernel(q_ref, f_ref, v_ref, s_in, s_out, o_ref, *, T):
    HD, B = HEAD_DIM, LANES
    for t in range(T):
        ln = slice(t * B, (t + 1) * B)
        q, f, v = q_ref[:, ln], f_ref[:, ln], v_ref[:, ln]
        k = 1.0 - f
        src = s_in if t == 0 else s_out
        o = [jnp.zeros((HD, B), F32), jnp.zeros((HD, B), F32)]
        for dk in range(HD):
            sn = f[dk:dk + 1, :] * src[dk] + k[dk:dk + 1, :] * v
            s_out[dk] = sn
            o[dk & 1] = o[dk & 1] + q[dk:dk + 1, :] * sn
        ot = o[0] + o[1]
        o_ref[:, ln] = ot * lax.rsqrt(_colsum(ot * ot) * (1.0 / HD) + EPS)


def _dec_d_kernel(q_ref, k_ref, v_ref, i_ref, f_ref, c_in, n_in, m_in, c_out, n_out, m_out, o_ref, *, T):
    HD, B = HEAD_DIM, LANES
    m = m_in[...]
    nvec = n_in[...]
    for t in range(T):
        ln = slice(t * B, (t + 1) * B)
        q, k, v = q_ref[:, ln], k_ref[:, ln], v_ref[:, ln]
        ipre, logf = i_ref[:, ln], f_ref[:, ln]
        m_new = jnp.maximum(logf + m, ipre)
        fs = jnp.exp(logf + m - m_new)
        kw = k * jnp.exp(ipre - m_new)
        src = c_in if t == 0 else c_out
        num = [jnp.zeros((HD, B), F32), jnp.zeros((HD, B), F32)]
        for dk in range(HD):
            cn = fs * src[dk] + kw[dk:dk + 1, :] * v
            c_out[dk] = cn
            num[dk & 1] = num[dk & 1] + q[dk:dk + 1, :] * cn
        nvec = fs * nvec + kw
        den = _colsum(q * nvec)
        hh = (num[0] + num[1]) / jnp.maximum(jnp.abs(den), jnp.exp(-m_new))
        o_ref[:, ln] = hh * lax.rsqrt(_colsum(hh * hh) * (1.0 / HD) + EPS)
        m = m_new
    n_out[...] = nvec
    m_out[...] = m


def _sample_post_kernel(oa_ref, ob_ref, oc_ref, od_ref, u_ref, nag_ref, nbg_ref, ncg_ref, ndg_ref, y_ref):
    gw = GROUP_WIDTH
    y_ref[:, 0:gw] = oa_ref[...].T * nag_ref[...] * _silu(u_ref[:, OFF_A_Z:OFF_A_Z + gw])
    ob = ob_ref[...].T * _silu(u_ref[:, OFF_B_Z:OFF_B_Z + gw])
    grp = gw // G_B
    for g in range(G_B):
        sl = slice(g * grp, (g + 1) * grp)
        y_ref[:, gw + g * grp:gw + (g + 1) * grp] = _rms(ob[:, sl]) * nbg_ref[:, sl]
    y_ref[:, 2 * gw:3 * gw] = oc_ref[...].T * ncg_ref[...] * _sigmoid(u_ref[:, OFF_C_G:OFF_C_G + gw])
    y_ref[:, 3 * gw:4 * gw] = od_ref[...].T * ndg_ref[...] * _sigmoid(u_ref[:, OFF_D_O:OFF_D_O + gw])


def _sample_mixers(u, states_t, params, *, layer, T):
    B, H, HD, GW = LANES, N_HEADS, HEAD_DIM, GROUP_WIDTH
    ca_t, sa_t, cb_t, sb_t, sc_t, dc_t, dn_t, dm_t = states_t
    (caw, cab, cbw, cbb, gbias, galog, nag, nbg, ncg, ndg, dskip, lblog) = params
    cp1 = pltpu.CompilerParams(dimension_semantics=("arbitrary",), vmem_limit_bytes=VMEM_LIMIT)
    n = T * B

    def whole(x):
        nd = x.ndim
        return pl.BlockSpec(x.shape, lambda i: (0,) * nd)

    def lane_blk(rows):
        return pl.BlockSpec((rows, B), lambda t: (0, t))

    a_t, b_t, c_t, d_t, g_t, ca_new, cb_new = pl.pallas_call(
        functools.partial(_sample_pre_kernel, layer=layer, T=T),
        grid=(T,),
        in_specs=[pl.BlockSpec((B, U_COLS), lambda t: (t, 0)),
                  pl.BlockSpec((None, CONV_TAIL, B, CONV_A_CH), lambda t: (layer, 0, 0, 0)),
                  pl.BlockSpec((None, CONV_TAIL, B, CONV_B_CH), lambda t: (layer, 0, 0, 0))]
                 + [whole(p) for p in (caw, cab, cbw, cbb, gbias, galog, lblog)],
        out_specs=[lane_blk(CONV_A_CH), lane_blk(CONV_B_CH), lane_blk(3 * GW), lane_blk(3 * GW),
                   lane_blk(DEC_GATE_ROWS),
                   pl.BlockSpec((CONV_TAIL, B, CONV_A_CH), lambda t: (0, 0, 0)),
                   pl.BlockSpec((CONV_TAIL, B, CONV_B_CH), lambda t: (0, 0, 0))],
        out_shape=[jax.ShapeDtypeStruct((CONV_A_CH, n), F32), jax.ShapeDtypeStruct((CONV_B_CH, n), F32),
                   jax.ShapeDtypeStruct((3 * GW, n), F32), jax.ShapeDtypeStruct((3 * GW, n), F32),
                   jax.ShapeDtypeStruct((DEC_GATE_ROWS, n), F32),
                   jax.ShapeDtypeStruct((CONV_TAIL, B, CONV_A_CH), F32),
                   jax.ShapeDtypeStruct((CONV_TAIL, B, CONV_B_CH), F32)],
        scratch_shapes=[pltpu.VMEM((CONV_TAIL + T, B, CONV_A_CH + CONV_B_CH), F32)],
        compiler_params=cp1, name="sample_pre_l%d" % layer,
    )(u, ca_t, cb_t, caw, cab, cbw, cbb, gbias, galog, lblog)

    g3 = g_t.reshape(DEC_GATE_ROWS, 1, n)

    def head_rows(sec):
        return pl.BlockSpec((HD, n), lambda h: (sec * H + h, 0))

    def gate_row(base):
        return pl.BlockSpec((None, 1, n), lambda h: (base + h, 0, 0))

    def state_in(x):
        nd = x.ndim - 2
        return pl.BlockSpec((None, None) + x.shape[2:], lambda h: (layer, h) + (0,) * nd)

    def state_out(shape):
        nd = len(shape)
        return pl.BlockSpec((None,) + tuple(shape), lambda h: (h,) + (0,) * nd)

    st_shape = (HD, HD, B)
    o_spec = pl.BlockSpec((HD, n), lambda h: (h, 0))
    o_shape = jax.ShapeDtypeStruct((GW, n), F32)
    st_out = jax.ShapeDtypeStruct((H,) + st_shape, F32)

    sa_new, oa = pl.pallas_call(
        functools.partial(_dec_a_kernel, T=T), grid=(H,),
        in_specs=[head_rows(0), head_rows(1), head_rows(2), gate_row(GL_DECAY_A), gate_row(GL_BETA), state_in(sa_t)],
        out_specs=[state_out(st_shape), o_spec], out_shape=[st_out, o_shape],
        compiler_params=cp1, name="dec_a_l%d" % layer,
    )(a_t, a_t, a_t, g3, g3, sa_t)

    hpg = H // G_B
    sb_new, ob = pl.pallas_call(
        functools.partial(_dec_b_kernel, T=T), grid=(H,),
        in_specs=[head_rows(0),
                  pl.BlockSpec((HD, n), lambda h: (H + h // hpg, 0)),
                  pl.BlockSpec((HD, n), lambda h: (H + G_B + h // hpg, 0)),
                  gate_row(GL_DT_B), gate_row(GL_DT_B2),
                  pl.BlockSpec((None, 1, B), lambda h: (h, 0, 0)), state_in(sb_t)],
        out_specs=[state_out(st_shape), o_spec], out_shape=[st_out, o_shape],
        compiler_params=cp1, name="dec_b_l%d" % layer,
    )(b_t, b_t, b_t, g3, g3, dskip, sb_t)

    sc_new, oc = pl.pallas_call(
        functools.partial(_dec_c_kernel, T=T), grid=(H,),
        in_specs=[head_rows(0), head_rows(1), head_rows(2), state_in(sc_t)],
        out_specs=[state_out(st_shape), o_spec], out_shape=[st_out, o_shape],
        compiler_params=cp1, name="dec_c_l%d" % layer,
    )(c_t, c_t, c_t, sc_t)

    dm3 = dm_t.reshape(dm_t.shape[0], H, 1, B)
    dc_new, dn_new, dm_new, od = pl.pallas_call(
        functools.partial(_dec_d_kernel, T=T), grid=(H,),
        in_specs=[head_rows(0), head_rows(1), head_rows(2), gate_row(GL_I_D), gate_row(GL_F_D),
                  state_in(dc_t), state_in(dn_t), state_in(dm3)],
        out_specs=[state_out(st_shape), state_out((HD, B)), state_out((1, B)), o_spec],
        out_shape=[st_out, jax.ShapeDtypeStruct((H, HD, B), F32), jax.ShapeDtypeStruct((H, 1, B), F32), o_shape],
        compiler_params=cp1, name="dec_d_l%d" % layer,
    )(d_t, d_t, d_t, g3, g3, dc_t, dn_t, dm3)

    y = pl.pallas_call(
        _sample_post_kernel, grid=(T,),
        in_specs=[lane_blk(GW)] * 4 + [pl.BlockSpec((B, U_COLS), lambda t: (t, 0))]
                 + [whole(p) for p in (nag, nbg, ncg, ndg)],
        out_specs=pl.BlockSpec((B, D_MODEL), lambda t: (t, 0)),
        out_shape=jax.ShapeDtypeStruct((n, D_MODEL), F32),
        compiler_params=cp1, name="sample_post_l%d" % layer,
    )(oa, ob, oc, od, u, nag, nbg, ncg, ndg)
    return y, (ca_new, sa_new, cb_new, sb_new, sc_new, dc_new, dn_new, dm_new.reshape(H, B))


def _prep_w_in_kernel(w_ref, o_ref):
    w = w_ref[...]
    o_ref[:, 0:1024] = w[:, 0:1024].astype(BF16)
    o_ref[:, 1024:1792] = w[:, 1032:1800].astype(BF16)
    o_ref[:, 1792:OFF_GATES] = w[:, 1804:3852].astype(BF16)
    gates = jnp.concatenate([w[:, 1024:1032], w[:, 1800:1804], w[:, 3852:3860], w[:, 1800:1804],
                             jnp.zeros((w.shape[0], U_COLS - OFF_GATES - N_GATE_COLS), F32)], axis=1)
    o_ref[:, OFF_GATES:U_COLS] = gates.astype(BF16)


def _prep_w_in(w_in):
    depth, d, cols = w_in.shape
    rows = 256
    return pl.pallas_call(
        _prep_w_in_kernel,
        grid=(depth, d // rows),
        in_specs=[pl.BlockSpec((None, rows, cols), lambda l, i: (l, i, 0))],
        out_specs=pl.BlockSpec((None, rows, U_COLS), lambda l, i: (l, i, 0)),
        out_shape=jax.ShapeDtypeStruct((depth, d, U_COLS), BF16),
        compiler_params=pltpu.CompilerParams(dimension_semantics=("arbitrary", "arbitrary"),
                                             vmem_limit_bytes=VMEM_LIMIT),
        name="prep_w_in",
    )(w_in)


def _gate_row(parts):
    row = jnp.zeros((LANES,), F32)
    for off, val in parts:
        row = lax.dynamic_update_slice(row, val.astype(F32), (off,))
    return row.reshape(1, LANES)


def _prompt_decoder(x, states, emb_g, emb_b, lb_logits, w_in_p, layer_w):
    bsz, t, _ = x.shape
    n = bsz * t
    h = x.reshape(n, D_MODEL)
    new_states = []
    for l in range(DEPTH):
        (conv_a_w, conv_a_b, a_log_a, dt_bias_a, norm_a_g, conv_b_w, conv_b_b, a_log_b, dt_bias_b,
         d_skip_b, norm_b_g, norm_c_g, i_bias_d, f_bias_d, norm_d_g, w_out, ln1_g, ln1_b, w_up, w_down,
         ln2_g, ln2_b) = [w[l] for w in layer_w]
        if l == 0:
            h, u = _inproj(h, w_in_p, l, emb_g.reshape(1, -1), emb_b.reshape(1, -1), True)
        else:
            (u,) = _inproj(h, w_in_p, l, emb_g.reshape(1, -1), emb_b.reshape(1, -1), False)
        params = (conv_a_w, conv_a_b.reshape(1, -1), conv_b_w, conv_b_b.reshape(1, -1),
                  _gate_row([(GL_DECAY_A, dt_bias_a), (GL_DT_B, dt_bias_b), (GL_I_D, i_bias_d), (GL_F_D, f_bias_d),
                             (GL_DT_B2, dt_bias_b)]),
                  _gate_row([(GL_DECAY_A, a_log_a), (GL_DT_B, a_log_b)]),
                  norm_a_g.reshape(1, -1), norm_b_g.reshape(1, -1), norm_c_g.reshape(1, -1),
                  norm_d_g.reshape(1, -1), jnp.repeat(d_skip_b, HEAD_DIM).reshape(1, -1), lb_logits)
        y, st = _mixers(u.reshape(bsz, t, U_COLS), states, params, layer=l, P=SEQS_PER_STEP_PROMPT)
        new_states.append(st)
        h = _outffn(y.reshape(n, D_MODEL), h, w_out.astype(BF16), ln1_g.reshape(1, -1), ln1_b.reshape(1, -1),
                    w_up.astype(BF16), w_down.astype(BF16), ln2_g.reshape(1, -1), ln2_b.reshape(1, -1))
    return h.reshape(bsz, t, D_MODEL), tuple(jnp.stack(z) for z in zip(*new_states))


def _sample_decoder(x, states, emb_g, emb_b, lb_logits, w_in_p, layer_w):
    bsz, t, _ = x.shape
    n = bsz * t
    h = jnp.transpose(x, (1, 0, 2)).reshape(n, D_MODEL)
    ca, sa, cb, sb, sc, dc, dn, dm = states
    states_t = (jnp.transpose(ca, (0, 2, 1, 3)), jnp.transpose(sa, (0, 2, 3, 4, 1)),
                jnp.transpose(cb, (0, 2, 1, 3)), jnp.transpose(sb, (0, 2, 3, 4, 1)),
                jnp.transpose(sc, (0, 2, 3, 4, 1)), jnp.transpose(dc, (0, 2, 3, 4, 1)),
                jnp.transpose(dn, (0, 2, 3, 1)), jnp.transpose(dm, (0, 2, 1)))
    new_states = []
    for l in range(DEPTH):
        (conv_a_w, conv_a_b, a_log_a, dt_bias_a, norm_a_g, conv_b_w, conv_b_b, a_log_b, dt_bias_b,
         d_skip_b, norm_b_g, norm_c_g, i_bias_d, f_bias_d, norm_d_g, w_out, ln1_g, ln1_b, w_up, w_down,
         ln2_g, ln2_b) = [w[l] for w in layer_w]
        if l == 0:
            h, u = _inproj(h, w_in_p, l, emb_g.reshape(1, -1), emb_b.reshape(1, -1), True)
        else:
            (u,) = _inproj(h, w_in_p, l, emb_g.reshape(1, -1), emb_b.reshape(1, -1), False)
        params = (conv_a_w, conv_a_b.reshape(1, -1), conv_b_w, conv_b_b.reshape(1, -1),
                  _gate_row([(GL_DECAY_A, dt_bias_a), (GL_DT_B, dt_bias_b), (GL_I_D, i_bias_d), (GL_F_D, f_bias_d),
                             (GL_DT_B2, dt_bias_b)]),
                  _gate_row([(GL_DECAY_A, a_log_a), (GL_DT_B, a_log_b)]),
                  norm_a_g.reshape(1, -1), norm_b_g.reshape(1, -1), norm_c_g.reshape(1, -1),
                  norm_d_g.reshape(1, -1), jnp.broadcast_to(d_skip_b[:, None, None], (N_HEADS, 1, LANES)),
                  lb_logits)
        y, st = _sample_mixers(u, states_t, params, layer=l, T=t)
        new_states.append(st)
        h = _outffn(y, h, w_out.astype(BF16), ln1_g.reshape(1, -1), ln1_b.reshape(1, -1),
                    w_up.astype(BF16), w_down.astype(BF16), ln2_g.reshape(1, -1), ln2_b.reshape(1, -1))
    ca_n, sa_n, cb_n, sb_n, sc_n, dc_n, dn_n, dm_n = (jnp.stack(z) for z in zip(*new_states))
    out_states = (jnp.transpose(ca_n, (0, 2, 1, 3)), jnp.transpose(sa_n, (0, 4, 1, 2, 3)),
                  jnp.transpose(cb_n, (0, 2, 1, 3)), jnp.transpose(sb_n, (0, 4, 1, 2, 3)),
                  jnp.transpose(sc_n, (0, 4, 1, 2, 3)), jnp.transpose(dc_n, (0, 4, 1, 2, 3)),
                  jnp.transpose(dn_n, (0, 3, 1, 2)), jnp.transpose(dm_n, (0, 2, 1)))
    return jnp.transpose(h.reshape(t, bsz, D_MODEL), (1, 0, 2)), out_states


def kernel(x_prompt, x_sample, state_a_conv, state_a_ssm, state_b_conv, state_b_ssm, state_c_ssm, state_d_cmem, state_d_nvec, state_d_mstab, emb_ln_g, emb_ln_b, lb_logits_c, w_in, conv_a_w, conv_a_b, a_log_a, dt_bias_a, norm_a_g, conv_b_w, conv_b_b, a_log_b, dt_bias_b, d_skip_b, norm_b_g, norm_c_g, i_bias_d, f_bias_d, norm_d_g, w_out, ln1_g, ln1_b, w_up, w_down, ln2_g, ln2_b):
    layer_w = (conv_a_w, conv_a_b, a_log_a, dt_bias_a, norm_a_g, conv_b_w, conv_b_b, a_log_b, dt_bias_b,
               d_skip_b, norm_b_g, norm_c_g, i_bias_d, f_bias_d, norm_d_g, w_out, ln1_g, ln1_b, w_up, w_down,
               ln2_g, ln2_b)
    sample_states = (state_a_conv, state_a_ssm, state_b_conv, state_b_ssm,
                     state_c_ssm, state_d_cmem, state_d_nvec, state_d_mstab)
    prompt_states = tuple(jnp.zeros((1, x_prompt.shape[0]) + s.shape[2:], F32) for s in sample_states)
    w_in_p = _prep_w_in(w_in)
    y_p, ps = _prompt_decoder(x_prompt, prompt_states, emb_ln_g, emb_ln_b, lb_logits_c, w_in_p, layer_w)
    y_s, ss = _sample_decoder(x_sample, sample_states, emb_ln_g, emb_ln_b, lb_logits_c, w_in_p, layer_w)
    return (y_p, y_s) + ps + ss
```

```python
import functools

import jax
import jax.numpy as jnp
from jax import lax
from jax.experimental import pallas as pl
from jax.experimental.pallas import tpu as pltpu

F32 = jnp.float32
BF16 = jnp.bfloat16

D_MODEL = 1024
DEPTH = 2
N_HEADS = 4
HEAD_DIM = 64
GROUP_WIDTH = N_HEADS * HEAD_DIM
G_B = 2
CONV_K = 4
CHUNK = 64
D_FF = 4 * D_MODEL
EPS = 1e-6
NEG = -1e30
ALPHA = (2 * DEPTH) ** 0.25
LOG2E = 1.4426950408889634

CONV_A_CH = 3 * GROUP_WIDTH
CONV_B_CH = GROUP_WIDTH + 2 * G_B * HEAD_DIM

OFF_A_QKV = 0
OFF_A_Z = 768
OFF_B_Z = 1024
OFF_B_XBC = 1280
OFF_C_Q = 1792
OFF_C_F = 2048
OFF_C_I = 2304
OFF_C_G = 2560
OFF_D_Q = 2816
OFF_D_K = 3072
OFF_D_V = 3328
OFF_D_O = 3584
OFF_GATES = 3840
U_COLS = 3968
LANES = 128
GL_BETA = 0
GL_DECAY_A = 4
GL_DT_B = 8
GL_I_D = 12
GL_F_D = 16
GL_DT_B2 = 20
N_GATE_COLS = 24

CONV_TAIL = CONV_K - 1
ROW0 = 8

VMEM_LIMIT = 56 * 1024 * 1024
TOKEN_TILE = 512
SUB_TILE = 256
SEQS_PER_STEP_PROMPT = 4
START_C = 2
START_D = 7
START_B = 10
SEQ_STAGGER = 0


def _layernorm(x, g, b):
    mu = jnp.mean(x, axis=-1, keepdims=True)
    xc = x - mu
    var = jnp.mean(xc * xc, axis=-1, keepdims=True)
    return xc * lax.rsqrt(var + EPS) * g + b


def _sigmoid(x):
    return 1.0 / (1.0 + jnp.exp(-x))


def _silu(x):
    return x * _sigmoid(x)


def _dot(a, b):
    return jnp.dot(a, b, preferred_element_type=F32)


def _mm(a, b):
    return _dot(a.astype(BF16), b.astype(BF16))


def _nt(a, b):
    return lax.dot_general(a.astype(BF16), b.astype(BF16), (((1,), (1,)), ((), ())),
                           preferred_element_type=F32)


def _split3(x):
    hi = x.astype(BF16)
    r1 = x - hi.astype(F32)
    mid = r1.astype(BF16)
    r2 = r1 - mid.astype(F32)
    return hi, mid, r2.astype(BF16)


def _dot01(m01, parts):
    acc = _dot(m01, parts[0])
    for p in parts[1:]:
        acc = acc + _dot(m01, p)
    return acc


def _rms(x):
    return x * lax.rsqrt(jnp.mean(x * x, axis=-1, keepdims=True) + EPS)


def _levels(L):
    out, b = [], 1
    while 2 * b <= L:
        out.append(b)
        b *= 2
    return out


def _interleaved(chains):
    pending = [ch if isinstance(ch, tuple) else (0, ch) for ch in chains]
    rnd = 0
    while pending:
        alive = []
        for start, ch in pending:
            if start > rnd:
                alive.append((start, ch))
                continue
            try:
                next(ch)
                alive.append((start, ch))
            except StopIteration:
                pass
        pending = alive
        rnd += 1
        yield


def _run_interleaved(chains):
    for _ in _interleaved(chains):
        pass


def _row_parts(n_rows):
    step = min(SUB_TILE, n_rows)
    return [slice(r, r + step) for r in range(0, n_rows, step)]


def _inproj_kernel(x_ref, w_ref, g_ref, b_ref, *out_refs, apply_ln):
    def part(rows):
        x = x_ref[rows, :]
        if apply_ln:
            x = _layernorm(x, g_ref[...], b_ref[...])
            out_refs[0][rows, :] = x
        yield
        out_refs[-1][rows, :] = _dot(x.astype(BF16), w_ref[...])

    _run_interleaved([part(rows) for rows in _row_parts(x_ref.shape[0])])


def _inproj(x, w_in_p, layer, g, b, apply_ln):
    n = x.shape[0]
    tm = min(TOKEN_TILE, n)
    grid = (n // tm,)
    const = lambda i: (0, 0)
    out_shape = [jax.ShapeDtypeStruct((n, U_COLS), F32)]
    out_specs = [pl.BlockSpec((tm, U_COLS), lambda i: (i, 0))]
    if apply_ln:
        out_shape = [jax.ShapeDtypeStruct((n, D_MODEL), F32)] + out_shape
        out_specs = [pl.BlockSpec((tm, D_MODEL), lambda i: (i, 0))] + out_specs
    return pl.pallas_call(
        functools.partial(_inproj_kernel, apply_ln=apply_ln),
        grid=grid,
        in_specs=[pl.BlockSpec((tm, D_MODEL), lambda i: (i, 0)),
                  pl.BlockSpec((None, D_MODEL, U_COLS), lambda i: (layer, 0, 0), pipeline_mode=pl.Buffered(1)),
                  pl.BlockSpec((1, D_MODEL), const),
                  pl.BlockSpec((1, D_MODEL), const)],
        out_specs=out_specs,
        out_shape=out_shape,
        compiler_params=pltpu.CompilerParams(dimension_semantics=("arbitrary",),
                                             vmem_limit_bytes=VMEM_LIMIT),
        name="inproj_ln" if apply_ln else "inproj",
    )(x, w_in_p, g, b)


def _outffn_kernel(mix_ref, x_ref, wo_ref, g1_ref, b1_ref, wu_ref, wd_ref, g2_ref, b2_ref, o_ref):
    def part(rows):
        m = _dot(mix_ref[rows, :].astype(BF16), wo_ref[...])
        yield
        h1 = _layernorm(ALPHA * x_ref[rows, :] + m, g1_ref[...], b1_ref[...])
        up = _dot(h1.astype(BF16), wu_ref[...])
        yield
        act = jnp.square(jnp.maximum(up, 0.0))
        ff = _dot(act.astype(BF16), wd_ref[...])
        yield
        o_ref[rows, :] = _layernorm(ALPHA * h1 + ff, g2_ref[...], b2_ref[...])

    _run_interleaved([part(rows) for rows in _row_parts(mix_ref.shape[0])])


def _outffn(mix, x, wo, g1, b1, wu, wd, g2, b2):
    n = x.shape[0]
    tm = min(TOKEN_TILE, n)
    const = lambda i: (0, 0)
    tile = pl.BlockSpec((tm, D_MODEL), lambda i: (i, 0))
    row = pl.BlockSpec((1, D_MODEL), const)
    return pl.pallas_call(
        _outffn_kernel,
        grid=(n // tm,),
        in_specs=[tile, tile,
                  pl.BlockSpec((D_MODEL, D_MODEL), const, pipeline_mode=pl.Buffered(1)),
                  row, row,
                  pl.BlockSpec((D_MODEL, D_FF), const, pipeline_mode=pl.Buffered(1)),
                  pl.BlockSpec((D_FF, D_MODEL), const, pipeline_mode=pl.Buffered(1)),
                  row, row],
        out_specs=tile,
        out_shape=jax.ShapeDtypeStruct((n, D_MODEL), F32),
        compiler_params=pltpu.CompilerParams(dimension_semantics=("arbitrary",),
                                             vmem_limit_bytes=VMEM_LIMIT),
        name="outffn",
    )(mix, x, wo, g1, b1, wu, wd, g2, b2)


def _mixer_pair_kernel(u_ref, ca_in, sa_in, cb_in, sb_in, sc_in, dc_in, dn_in, dm_in,
                       caw_ref, cab_ref, cbw_ref, cbb_ref, gbias_ref, galog_ref,
                       nag_ref, nbg_ref, ncg_ref, ndg_ref, dskip_ref, lblog_ref,
                       y_ref, ca_out, sa_out, cb_out, sb_out, sc_out, dc_out, dn_out, dm_out,
                       cbuf, sa, sb, sc, dc, dn, dm, exp01_s,
                       *, NC, P, layer):
    c = pl.program_id(1)
    H, HD, L, W = N_HEADS, HEAD_DIM, CHUNK, LANES
    NP = H // 2
    R0, R1 = ROW0, ROW0 + L
    CB_A, CB_B = 0, CONV_A_CH
    f32 = F32

    def iota(shape, d):
        return lax.broadcasted_iota(jnp.int32, shape, d)

    def one_bf16(mask):
        return jnp.where(mask, 1.0, 0.0).astype(BF16)

    r_ll, c_ll = iota((L, L), 0), iota((L, L), 1)
    tril01 = one_bf16(r_ll >= c_ll)
    r_lw, c_lw = iota((L, W), 0), iota((L, W), 1)
    r_lg = iota((L, GROUP_WIDTH), 0)
    j_lw = c_lw & (HD - 1)
    incl_p = r_lw >= j_lw
    incl01 = jnp.where(incl_p, 1.0, 0.0)
    strict01 = jnp.where(r_lw > j_lw, 1.0, 0.0)
    levels = _levels(L)
    lm01 = {}
    for b in levels:
        sh = b.bit_length() - 1
        rb, cb = r_lw >> sh, j_lw >> sh
        lm01[b] = jnp.where((rb - cb == 1) & ((cb & 1) == 0), 1.0, 0.0)
    r_ww, c_ww = iota((W, W), 0), iota((W, W), 1)
    bdiag = (r_ww >> 6) == (c_ww >> 6)
    bd01 = jnp.where(bdiag, 1.0, 0.0)
    ones_bd = bd01.astype(BF16)
    ones_ww = jnp.ones((W, W), BF16)
    lane = iota((1, W), 1)
    lo_half = lane < HD
    is_dt = (lane >= GL_DECAY_A) & (lane < GL_I_D)
    is_i = (lane >= GL_I_D) & (lane < GL_F_D)
    is_f = (lane >= GL_F_D) & (lane < GL_F_D + H)
    is_cum = is_dt | is_f
    cum_scale = jnp.where(is_dt, LOG2E, 1.0)

    @pl.when((pl.program_id(0) == 0) & (c == 0))
    def _():
        exp_bases = (GL_DECAY_A, GL_BETA, GL_DT_B, GL_DT_B2, GL_F_D, GL_I_D)
        for i, (base, p) in enumerate([(base, p) for base in exp_bases for p in range(NP)]):
            exp01_s[:, i * W:(i + 1) * W] = one_bf16(r_ww == (base + 2 * p + (c_ww >> 6)))

    exp01 = exp01_s[...]

    def bd(x):
        xb = x.astype(BF16)
        return jnp.concatenate([xb, xb], axis=0) * ones_bd

    def half_sums(x):
        return _dot(x.astype(BF16), ones_bd)

    def transpose_bf16(x):
        return x.T.astype(BF16)

    pl_sm = lblog_ref[...]
    pl_sm = jnp.exp(pl_sm - jnp.max(pl_sm, axis=0, keepdims=True))
    pl_sm = pl_sm / jnp.sum(pl_sm, axis=0, keepdims=True)
    lb = pl_sm[0:1, :]
    for i in range(1, layer + 1):
        lb = lb + pl_sm[i:i + 1, :]
    lb = lb - pl_sm[0:1, :]

    @pl.when(c == 0)
    def _():
        cbuf[:, R0 - CONV_TAIL:R0, CB_A:CB_A + CONV_A_CH] = ca_in[...]
        cbuf[:, R0 - CONV_TAIL:R0, CB_B:CB_B + CONV_B_CH] = cb_in[...]
        zero = jnp.zeros((W, W), f32)
        for s in range(P):
            m_row = dm_in[s]
            for p in range(NP):
                h0, h1 = 2 * p, 2 * p + 1
                for ref, src in ((sa, sa_in), (sc, sc_in), (dc, dc_in)):
                    ref[s, p] = zero
                    ref[s, p, 0:HD, 0:HD] = src[s, h0]
                    ref[s, p, HD:W, HD:W] = src[s, h1]
                sb[s, p] = zero
                sb[s, p, p * HD:(p + 1) * HD, 0:HD] = sb_in[s, h0]
                sb[s, p, p * HD:(p + 1) * HD, HD:W] = sb_in[s, h1]
                dn[s, p] = jnp.concatenate([dn_in[s, h0:h0 + 1, :], dn_in[s, h1:h1 + 1, :]], axis=1)
                dm[s, p] = jnp.where(lo_half, m_row[:, h0:h0 + 1], m_row[:, h1:h1 + 1])

    cbuf[:, R0:R1, CB_A:CB_A + CONV_A_CH] = u_ref[:, :, OFF_A_QKV:OFF_A_QKV + CONV_A_CH]
    cbuf[:, R0:R1, CB_B:CB_B + CONV_B_CH] = u_ref[:, :, OFF_B_XBC:OFF_B_XBC + CONV_B_CH]

    def seq_chains(s):
        def seg(off, width):
            return u_ref[s, :, off:off + width]

        def conv(off, width, w_ref, b_ref):
            acc = b_ref[...] + cbuf[s, R0:R1, off:off + width] * w_ref[CONV_K - 1:CONV_K, :]
            for j in range(CONV_K - 1):
                lo = R0 - CONV_TAIL + j
                acc = acc + cbuf[s, lo:lo + L, off:off + width] * w_ref[j:j + 1, :]
            return acc

        pre = seg(OFF_GATES, W) + gbias_ref[...]
        e = jnp.exp(-jnp.abs(pre))
        l1p = jnp.log1p(e)
        softplus = jnp.maximum(pre, 0.0) + l1p
        logsig = jnp.minimum(pre, 0.0) - l1p
        logdec = jnp.where(is_dt, -jnp.exp(galog_ref[...]) * softplus, jnp.where(is_f, logsig, 0.0))
        gval = jnp.where(lane < GL_DECAY_A, _sigmoid(pre), jnp.where(is_i, pre, softplus))
        gcum = _dot01(tril01, _split3(logdec))
        fgate = lb + (1.0 - lb) * _sigmoid(seg(OFF_C_F, GROUP_WIDTH))
        gc = _dot01(tril01, _split3(jnp.log(fgate))) * LOG2E
        qkv = _silu(conv(CB_A, CONV_A_CH, caw_ref, cab_ref))
        qk_raw = qkv[:, 0:2 * GROUP_WIDTH]
        ss_qk = jnp.concatenate([half_sums(qk_raw[:, i * W:(i + 1) * W] * qk_raw[:, i * W:(i + 1) * W])
                                 for i in range(2 * NP)], axis=1)
        yield
        gsrc = jnp.where(is_cum, gcum * cum_scale, gval)
        src = _split3(gsrc)
        expd = _dot(src[0], exp01) + _dot(src[1], exp01) + _dot(src[2], exp01)
        gsrc_t = gsrc.T

        def expanded(k, p):
            o = (k * NP + p) * W
            return expd[:, o:o + W]

        def row_pair(base, p):
            r = base + 2 * p
            return jnp.concatenate([gsrc_t[r:r + 1, :], gsrc_t[r + 1:r + 2, :]], axis=1)

        def chain_a(p):
            ps = slice(p * W, (p + 1) * W)
            q = qkv[:, p * W:(p + 1) * W]
            k = qkv[:, GROUP_WIDTH + p * W:GROUP_WIDTH + (p + 1) * W]
            v = qkv[:, 2 * GROUP_WIDTH + p * W:2 * GROUP_WIDTH + (p + 1) * W]
            ssq = ss_qk[:, p * W:(p + 1) * W]
            ssk = ss_qk[:, GROUP_WIDTH + p * W:GROUP_WIDTH + (p + 1) * W]
            g = expanded(0, p)
            beta = expanded(1, p)
            grow = row_pair(GL_DECAY_A, p)
            q = q * lax.rsqrt(ssq + EPS) * (HD ** -0.5)
            k = k * lax.rsqrt(ssk + EPS)
            eg = jnp.exp2(g)
            g_last = g[L - 1:L, :]
            kbd = bd(k)
            kk = _nt(k, kbd)
            qk = _nt(q, kbd)
            kw_t = transpose_bf16(k * jnp.exp2(g_last - g))
            dmat = jnp.exp2(jnp.minimum(g - grow, 0.0)) * incl01
            yield
            nmat = (beta * kk) * (dmat * strict01)
            low = -(nmat * lm01[1])
            for b in levels[1:]:
                off = nmat * lm01[b]
                x = off + _dot(off.astype(BF16), bd(low))
                yield
                low = low - x - _dot(low.astype(BF16), bd(x))
                yield
            rv = beta * v
            rk = (beta * eg) * k
            rhs_bd = jnp.concatenate([bd(rv), bd(rk)], axis=1)
            sol = jnp.concatenate([rv, rk], axis=1) + _dot(low.astype(BF16), rhs_bd)
            s0 = sa[s, p]
            qs = _mm(q, s0)
            yield
            unew = sol[:, :W] - _mm(sol[:, W:], s0)
            yield
            o = eg * qs + _mm(qk * dmat, bd(unew))
            sa[s, p] = jnp.exp2(g_last) * s0 + _mm(kw_t, unew) * bd01
            ms = half_sums(o * o) * (1.0 / HD)
            yield
            a_z = seg(OFF_A_Z + p * W, W)
            y_ref[s, :, ps] = o * lax.rsqrt(ms + EPS) * nag_ref[:, ps] * _silu(a_z)

        xbc = _silu(conv(CB_B, CONV_B_CH, cbw_ref, cbb_ref))
        b_all = xbc[:, GROUP_WIDTH:GROUP_WIDTH + W]
        c_all = xbc[:, GROUP_WIDTH + W:GROUP_WIDTH + 2 * W]
        b_t = transpose_bf16(b_all)

        def chain_b(p):
            ps = slice(p * W, (p + 1) * W)
            b_grp = b_all * jnp.where((lane >> 6) == p, 1.0, 0.0)
            cb = _nt(c_all, jnp.concatenate([b_grp, b_grp], axis=0))
            g = expanded(2, p)
            dt = expanded(3, p)
            grow = row_pair(GL_DT_B, p)
            yield
            g_last = g[L - 1:L, :]
            xh = xbc[:, ps]
            v = xh * dt
            s0 = sb[s, p]
            dmat = jnp.exp2(jnp.minimum(g - grow, 0.0)) * incl01
            o = jnp.exp2(g) * _mm(c_all, s0) + _mm(cb * dmat, bd(v))
            upd = _dot(b_t, (v * jnp.exp2(g_last - g)).astype(BF16))
            sb[s, p] = jnp.exp2(g_last) * s0 + upd * jnp.where((r_ww >> 6) == p, 1.0, 0.0)
            yield
            ob = (o + dskip_ref[:, ps] * xh) * _silu(seg(OFF_B_Z + p * W, W))
            ms = _dot((ob * ob).astype(BF16), ones_ww) * (1.0 / W)
            yield
            y_ref[s, :, GROUP_WIDTH + p * W:GROUP_WIDTH + (p + 1) * W] = ob * lax.rsqrt(ms + EPS) * nbg_ref[:, ps]

        kc = 1.0 - fgate
        pref = {}
        for b in levels:
            blk = 2 * b
            if blk >= 8:
                pref[b] = jnp.concatenate(
                    [jnp.broadcast_to(gc[i * blk + b - 1:i * blk + b, :], (blk, GROUP_WIDTH)) for i in range(L // blk)],
                    axis=0)
            else:
                acc = gc
                for d in range(blk):
                    sh = d - (b - 1)
                    if sh != 0:
                        acc = jnp.where((r_lg & (blk - 1)) == d, pltpu.roll(gc, sh % L, axis=0), acc)
                pref[b] = acc

        def chain_c(p):
            ps = slice(p * W, (p + 1) * W)
            q = seg(OFF_C_Q + p * W, W)
            v = seg(OFF_C_I + p * W, W)
            k, g = kc[:, ps], gc[:, ps]
            g_last = g[L - 1:L, :]
            kw_t = transpose_bf16(k * jnp.exp2(g_last - g))
            e_col = jnp.exp2(jnp.broadcast_to(g_last, (8, W)).T[:, 0:1])
            qk_diag = half_sums(q * k)
            yield
            att = None
            for b in levels:
                pb = pref[b][:, ps]
                kd = bd(k * jnp.exp2(jnp.minimum(pb - g, 0.0)))
                part = _nt(q * jnp.exp2(jnp.minimum(g - pb, 0.0)), kd) * lm01[b]
                att = part if att is None else att + part
                yield
            s0 = sc[s, p]
            o = _mm(q * jnp.exp2(g), s0) + _mm(att, bd(v)) + qk_diag * v
            sc[s, p] = e_col * s0 + _mm(kw_t, v) * bd01
            ms = half_sums(o * o) * (1.0 / HD)
            yield
            c_g = seg(OFF_C_G + p * W, W)
            y_ref[s, :, 2 * GROUP_WIDTH + p * W:2 * GROUP_WIDTH + (p + 1) * W] = (
                o * lax.rsqrt(ms + EPS) * ncg_ref[:, ps] * _sigmoid(c_g))

        def chain_d(p):
            ps = slice(p * W, (p + 1) * W)
            q = seg(OFF_D_Q + p * W, W)
            k = seg(OFF_D_K + p * W, W) * (HD ** -0.5)
            v = seg(OFF_D_V + p * W, W)
            bcum = expanded(4, p)
            ipre = expanded(5, p)
            m0 = dm[s, p]
            a = ipre - bcum
            arow = row_pair(GL_I_D, p) - row_pair(GL_F_D, p)
            qk = _nt(q, bd(k).astype(BF16))
            cmem = dc[s, p]
            qc = _mm(q, cmem)
            nvec = dn[s, p]
            qn = half_sums(q * nvec)
            cm = a
            sh = 1
            while sh < L:
                cm = jnp.maximum(cm, jnp.where(r_lw >= sh, pltpu.roll(cm, sh, axis=0), NEG))
                sh *= 2
            yield
            m_r = bcum + jnp.maximum(m0, cm)
            pmat = jnp.where(incl_p, jnp.exp(bcum + arow - m_r), 0.0) * qk
            s_init = jnp.exp(bcum + m0 - m_r)
            num = s_init * qc + _mm(pmat, bd(v))
            den = s_init * qn + half_sums(pmat)
            m_last = m_r[L - 1:L, :]
            b_last = bcum[L - 1:L, :]
            scale = jnp.exp(b_last + m0 - m_last)
            kw = k * jnp.exp(b_last - bcum + ipre - m_last)
            kw_t = transpose_bf16(kw)
            yield
            dc[s, p] = scale * cmem + _mm(kw_t, v) * bd01
            dn[s, p] = scale * nvec + jnp.sum(kw, axis=0, keepdims=True)
            dm[s, p] = m_last
            hh = num / jnp.maximum(jnp.abs(den), jnp.exp(-m_r))
            ms = half_sums(hh * hh) * (1.0 / HD)
            yield
            d_o = seg(OFF_D_O + p * W, W)
            y_ref[s, :, 3 * GROUP_WIDTH + p * W:3 * GROUP_WIDTH + (p + 1) * W] = (
                hh * lax.rsqrt(ms + EPS) * ndg_ref[:, ps] * _sigmoid(d_o))

        return ([chain_a(p) for p in range(NP)] + [(START_C, chain_c(p)) for p in range(NP)]
                + [(START_D, chain_d(p)) for p in range(NP)] + [(START_B, chain_b(p)) for p in range(NP)])

    def seq_driver(s):
        chains = yield from seq_chains(s)
        yield
        yield from _interleaved(chains)

    _run_interleaved([(s * SEQ_STAGGER, seq_driver(s)) for s in range(P)])

    tail_a = cbuf[:, R1 - CONV_TAIL:R1, CB_A:CB_A + CONV_A_CH]
    tail_b = cbuf[:, R1 - CONV_TAIL:R1, CB_B:CB_B + CONV_B_CH]
    cbuf[:, R0 - CONV_TAIL:R0, CB_A:CB_A + CONV_A_CH] = tail_a
    cbuf[:, R0 - CONV_TAIL:R0, CB_B:CB_B + CONV_B_CH] = tail_b

    @pl.when(c == NC - 1)
    def _():
        ca_out[...] = tail_a
        cb_out[...] = tail_b
        for s in range(P):
            m_row = jnp.zeros((1, W), f32)
            for p in range(NP):
                h0, h1 = 2 * p, 2 * p + 1
                for ref, dst in ((sa, sa_out), (sc, sc_out), (dc, dc_out)):
                    dst[s, h0] = ref[s, p, 0:HD, 0:HD]
                    dst[s, h1] = ref[s, p, HD:W, HD:W]
                sb_out[s, h0] = sb[s, p, p * HD:(p + 1) * HD, 0:HD]
                sb_out[s, h1] = sb[s, p, p * HD:(p + 1) * HD, HD:W]
                nrow = dn[s, p]
                dn_out[s, h0:h0 + 1, :] = nrow[:, 0:HD]
                dn_out[s, h1:h1 + 1, :] = nrow[:, HD:W]
                mp = dm[s, p]
                m_row = jnp.where(lane == h0, mp[:, 0:1], jnp.where(lane == h1, mp[:, HD:HD + 1], m_row))
            dm_out[s] = m_row


def _mixers(u, states, params, *, layer, P):
    bsz, ttot, _ = u.shape
    T = L = CHUNK
    nc = ttot // T
    H, HD = N_HEADS, HEAD_DIM
    ca, s_a, cb, s_b, s_c, d_c, d_n, d_m = states
    d_m = jnp.pad(d_m, ((0, 0), (0, 0), (0, LANES - H))).reshape(d_m.shape[0], bsz, 1, LANES)

    def per_seq(shape):
        nd = len(shape)
        return pl.BlockSpec((P,) + tuple(shape), lambda b, c: (b,) + (0,) * nd)

    def per_seq_in(x, shape):
        nd = len(shape)
        lidx = layer if x.shape[0] > 1 else 0
        return pl.BlockSpec((None, P) + tuple(shape), lambda b, c: (lidx, b) + (0,) * nd)

    def whole(x):
        nd = x.ndim
        return pl.BlockSpec(x.shape, lambda b, c: (0,) * nd)

    state_dims = [(CONV_TAIL, CONV_A_CH), (H, HD, HD), (CONV_TAIL, CONV_B_CH), (H, HD, HD), (H, HD, HD),
                  (H, HD, HD), (H, HD), (1, LANES)]
    state_specs = [per_seq(d) for d in state_dims]
    state_shapes = [jax.ShapeDtypeStruct((bsz,) + d, F32) for d in state_dims]
    pair_state = (P, H // 2, LANES, LANES)
    scratch = ([pltpu.VMEM((P, ROW0 + L, CONV_A_CH + CONV_B_CH), F32)]
               + [pltpu.VMEM(pair_state, F32) for _ in range(4)]
               + [pltpu.VMEM((P, H // 2, 1, LANES), F32) for _ in range(2)]
               + [pltpu.VMEM((LANES, 6 * (H // 2) * LANES), BF16)])
    outs = pl.pallas_call(
        functools.partial(_mixer_pair_kernel, NC=nc, P=P, layer=layer),
        grid=(bsz // P, nc),
        in_specs=[pl.BlockSpec((P, T, U_COLS), lambda b, c: (b, c, 0))]
                 + [per_seq_in(x, d) for x, d in zip((ca, s_a, cb, s_b, s_c, d_c, d_n, d_m), state_dims)]
                 + [whole(p) for p in params],
        out_specs=[pl.BlockSpec((P, T, D_MODEL), lambda b, c: (b, c, 0))] + state_specs,
        out_shape=[jax.ShapeDtypeStruct((bsz, ttot, D_MODEL), F32)] + state_shapes,
        scratch_shapes=scratch,
        compiler_params=pltpu.CompilerParams(dimension_semantics=("arbitrary", "arbitrary"),
                                             vmem_limit_bytes=VMEM_LIMIT),
        name="mixers_l%d" % layer,
    )(u, ca, s_a, cb, s_b, s_c, d_c, d_n, d_m, *params)
    y, new = outs[0], list(outs[1:])
    new[7] = new[7].reshape(bsz, LANES)[:, :H]
    return y, tuple(new)


DEC_GATE_ROWS = 128


def _sample_pre_kernel(u_ref, ca_ref, cb_ref, caw_ref, cab_ref, cbw_ref, cbb_ref, gbias_ref, galog_ref, lblog_ref,
                       at_ref, bt_ref, ct_ref, dt_ref, gt_ref, ca_out, cb_out, hist, *, layer, T):
    t = pl.program_id(0)
    na = CONV_A_CH

    @pl.when(t == 0)
    def _():
        hist[0:CONV_TAIL, :, 0:na] = ca_ref[...]
        hist[0:CONV_TAIL, :, na:] = cb_ref[...]

    hist[CONV_TAIL + t, :, 0:na] = u_ref[:, OFF_A_QKV:OFF_A_QKV + CONV_A_CH]
    hist[CONV_TAIL + t, :, na:] = u_ref[:, OFF_B_XBC:OFF_B_XBC + CONV_B_CH]
    w_all = jnp.concatenate([caw_ref[...], cbw_ref[...]], axis=1)
    acc = jnp.concatenate([cab_ref[...], cbb_ref[...]], axis=1)
    for j in range(CONV_K):
        acc = acc + hist[t + j] * w_all[j:j + 1, :]
    act = _silu(acc)
    at_ref[...] = act[:, 0:na].T
    bt_ref[...] = act[:, na:].T

    lane = lax.broadcasted_iota(jnp.int32, (1, LANES), 1)
    pre = u_ref[:, OFF_GATES:OFF_GATES + LANES] + gbias_ref[...]
    e = jnp.exp(-jnp.abs(pre))
    l1p = jnp.log1p(e)
    softplus = jnp.maximum(pre, 0.0) + l1p
    logsig = jnp.minimum(pre, 0.0) - l1p
    is_dt = (lane >= GL_DECAY_A) & (lane < GL_I_D)
    is_i = (lane >= GL_I_D) & (lane < GL_F_D)
    is_f = (lane >= GL_F_D) & (lane < GL_F_D + N_HEADS)
    decay = jnp.exp(-jnp.exp(galog_ref[...]) * softplus)
    gates = jnp.where(lane < GL_DECAY_A, _sigmoid(pre),
                      jnp.where(is_dt, decay, jnp.where(is_i, pre, jnp.where(is_f, logsig, softplus))))
    gt_ref[...] = gates.T

    pl_sm = lblog_ref[...]
    pl_sm = jnp.exp(pl_sm - jnp.max(pl_sm, axis=0, keepdims=True))
    pl_sm = pl_sm / jnp.sum(pl_sm, axis=0, keepdims=True)
    lb = pl_sm[0:1, :]
    for i in range(1, layer + 1):
        lb = lb + pl_sm[i:i + 1, :]
    lb = lb - pl_sm[0:1, :]
    fgate = lb + (1.0 - lb) * _sigmoid(u_ref[:, OFF_C_F:OFF_C_F + GROUP_WIDTH])
    gw = GROUP_WIDTH
    ct_ref[0:gw, :] = u_ref[:, OFF_C_Q:OFF_C_Q + gw].T
    ct_ref[gw:2 * gw, :] = fgate.T
    ct_ref[2 * gw:3 * gw, :] = u_ref[:, OFF_C_I:OFF_C_I + gw].T
    dt_ref[0:gw, :] = u_ref[:, OFF_D_Q:OFF_D_Q + gw].T
    dt_ref[gw:2 * gw, :] = (u_ref[:, OFF_D_K:OFF_D_K + gw] * (HEAD_DIM ** -0.5)).T
    dt_ref[2 * gw:3 * gw, :] = u_ref[:, OFF_D_V:OFF_D_V + gw].T

    @pl.when(t == T - 1)
    def _():
        ca_out[...] = hist[T:T + CONV_TAIL, :, 0:na]
        cb_out[...] = hist[T:T + CONV_TAIL, :, na:]


def _colsum(x):
    return jnp.sum(x, axis=0, keepdims=True)


def _dec_a_kernel(q_ref, k_ref, v_ref, a_ref, beta_ref, s_in, s_out, o_ref, *, T):
    HD, B = HEAD_DIM, LANES
    for t in range(T):
        ln = slice(t * B, (t + 1) * B)
        q, k, v = q_ref[:, ln], k_ref[:, ln], v_ref[:, ln]
        q = q * (lax.rsqrt(_colsum(q * q) + EPS) * (HD ** -0.5))
        k = k * lax.rsqrt(_colsum(k * k) + EPS)
        a, beta = a_ref[:, ln], beta_ref[:, ln]
        src = s_in if t == 0 else s_out
        ks = [jnp.zeros((HD, B), F32), jnp.zeros((HD, B), F32)]
        for dk in range(HD):
            ks[dk & 1] = ks[dk & 1] + k[dk:dk + 1, :] * src[dk]
        unew = beta * (v - a * (ks[0] + ks[1]))
        o = [jnp.zeros((HD, B), F32), jnp.zeros((HD, B), F32)]
        for dk in range(HD):
            sn = a * src[dk] + k[dk:dk + 1, :] * unew
            s_out[dk] = sn
            o[dk & 1] = o[dk & 1] + q[dk:dk + 1, :] * sn
        ot = o[0] + o[1]
        o_ref[:, ln] = ot * lax.rsqrt(_colsum(ot * ot) * (1.0 / HD) + EPS)


def _dec_b_kernel(x_ref, b_ref, c_ref, a_ref, dt_ref, skip_ref, s_in, s_out, o_ref, *, T):
    HD, B = HEAD_DIM, LANES
    for t in range(T):
        ln = slice(t * B, (t + 1) * B)
        x, bm, cm = x_ref[:, ln], b_ref[:, ln], c_ref[:, ln]
        a = a_ref[:, ln]
        xdt = x * dt_ref[:, ln]
        src = s_in if t == 0 else s_out
        o = [jnp.zeros((HD, B), F32), jnp.zeros((HD, B), F32)]
        for n in range(HD):
            sn = a * src[n] + bm[n:n + 1, :] * xdt
            s_out[n] = sn
            o[n & 1] = o[n & 1] + cm[n:n + 1, :] * sn
        o_ref[:, ln] = o[0] + o[1] + skip_ref[...] * x


def _dec_c_kernel(q_ref, f_ref, v_ref, s_in, s_out, o_ref, *, T):
    HD, B = HEAD_DIM, LANES
    for t in range(T):
        ln = slice(t * B, (t + 1) * B)
        q, f, v = q_ref[:, ln], f_ref[:, ln], v_ref[:, ln]
        k = 1.0 - f
        src = s_in if t == 0 else s_out
        o = [jnp.zeros((HD, B), F32), jnp.zeros((HD, B), F32)]
        for dk in range(HD):
            sn = f[dk:dk + 1, :] * src[dk] + k[dk:dk + 1, :] * v
            s_out[dk] = sn
            o[dk & 1] = o[dk & 1] + q[dk:dk + 1, :] * sn
        ot = o[0] + o[1]
        o_ref[:, ln] = ot * lax.rsqrt(_colsum(ot * ot) * (1.0 / HD) + EPS)


def _dec_d_kernel(q_ref, k_ref, v_ref, i_ref, f_ref, c_in, n_in, m_in, c_out, n_out, m_out, o_ref, *, T):
    HD, B = HEAD_DIM, LANES
    m = m_in[...]
    nvec = n_in[...]
    for t in range(T):
        ln = slice(t * B, (t + 1) * B)
        q, k, v = q_ref[:, ln], k_ref[:, ln], v_ref[:, ln]
        ipre, logf = i_ref[:, ln], f_ref[:, ln]
        m_new = jnp.maximum(logf + m, ipre)
        fs = jnp.exp(logf + m - m_new)
        kw = k * jnp.exp(ipre - m_new)
        src = c_in if t == 0 else c_out
        num = [jnp.zeros((HD, B), F32), jnp.zeros((HD, B), F32)]
        for dk in range(HD):
            cn = fs * src[dk] + kw[dk:dk + 1, :] * v
            c_out[dk] = cn
            num[dk & 1] = num[dk & 1] + q[dk:dk + 1, :] * cn
        nvec = fs * nvec + kw
        den = _colsum(q * nvec)
        hh = (num[0] + num[1]) / jnp.maximum(jnp.abs(den), jnp.exp(-m_new))
        o_ref[:, ln] = hh * lax.rsqrt(_colsum(hh * hh) * (1.0 / HD) + EPS)
        m = m_new
    n_out[...] = nvec
    m_out[...] = m


def _sample_post_kernel(oa_ref, ob_ref, oc_ref, od_ref, u_ref, nag_ref, nbg_ref, ncg_ref, ndg_ref, y_ref):
    gw = GROUP_WIDTH
    y_ref[:, 0:gw] = oa_ref[...].T * nag_ref[...] * _silu(u_ref[:, OFF_A_Z:OFF_A_Z + gw])
    ob = ob_ref[...].T * _silu(u_ref[:, OFF_B_Z:OFF_B_Z + gw])
    grp = gw // G_B
    for g in range(G_B):
        sl = slice(g * grp, (g + 1) * grp)
        y_ref[:, gw + g * grp:gw + (g + 1) * grp] = _rms(ob[:, sl]) * nbg_ref[:, sl]
    y_ref[:, 2 * gw:3 * gw] = oc_ref[...].T * ncg_ref[...] * _sigmoid(u_ref[:, OFF_C_G:OFF_C_G + gw])
    y_ref[:, 3 * gw:4 * gw] = od_ref[...].T * ndg_ref[...] * _sigmoid(u_ref[:, OFF_D_O:OFF_D_O + gw])


def _sample_mixers(u, states_t, params, *, layer, T):
    B, H, HD, GW = LANES, N_HEADS, HEAD_DIM, GROUP_WIDTH
    ca_t, sa_t, cb_t, sb_t, sc_t, dc_t, dn_t, dm_t = states_t
    (caw, cab, cbw, cbb, gbias, galog, nag, nbg, ncg, ndg, dskip, lblog) = params
    cp1 = pltpu.CompilerParams(dimension_semantics=("arbitrary",), vmem_limit_bytes=VMEM_LIMIT)
    n = T * B

    def whole(x):
        nd = x.ndim
        return pl.BlockSpec(x.shape, lambda i: (0,) * nd)

    def lane_blk(rows):
        return pl.BlockSpec((rows, B), lambda t: (0, t))

    a_t, b_t, c_t, d_t, g_t, ca_new, cb_new = pl.pallas_call(
        functools.partial(_sample_pre_kernel, layer=layer, T=T),
        grid=(T,),
        in_specs=[pl.BlockSpec((B, U_COLS), lambda t: (t, 0)),
                  pl.BlockSpec((None, CONV_TAIL, B, CONV_A_CH), lambda t: (layer, 0, 0, 0)),
                  pl.BlockSpec((None, CONV_TAIL, B, CONV_B_CH), lambda t: (layer, 0, 0, 0))]
                 + [whole(p) for p in (caw, cab, cbw, cbb, gbias, galog, lblog)],
        out_specs=[lane_blk(CONV_A_CH), lane_blk(CONV_B_CH), lane_blk(3 * GW), lane_blk(3 * GW),
                   lane_blk(DEC_GATE_ROWS),
                   pl.BlockSpec((CONV_TAIL, B, CONV_A_CH), lambda t: (0, 0, 0)),
                   pl.BlockSpec((CONV_TAIL, B, CONV_B_CH), lambda t: (0, 0, 0))],
        out_shape=[jax.ShapeDtypeStruct((CONV_A_CH, n), F32), jax.ShapeDtypeStruct((CONV_B_CH, n), F32),
                   jax.ShapeDtypeStruct((3 * GW, n), F32), jax.ShapeDtypeStruct((3 * GW, n), F32),
                   jax.ShapeDtypeStruct((DEC_GATE_ROWS, n), F32),
                   jax.ShapeDtypeStruct((CONV_TAIL, B, CONV_A_CH), F32),
                   jax.ShapeDtypeStruct((CONV_TAIL, B, CONV_B_CH), F32)],
        scratch_shapes=[pltpu.VMEM((CONV_TAIL + T, B, CONV_A_CH + CONV_B_CH), F32)],
        compiler_params=cp1, name="sample_pre_l%d" % layer,
    )(u, ca_t, cb_t, caw, cab, cbw, cbb, gbias, galog, lblog)

    g3 = g_t.reshape(DEC_GATE_ROWS, 1, n)

    def head_rows(sec):
        return pl.BlockSpec((HD, n), lambda h: (sec * H + h, 0))

    def gate_row(base):
        return pl.BlockSpec((None, 1, n), lambda h: (base + h, 0, 0))

    def state_in(x):
        nd = x.ndim - 2
        return pl.BlockSpec((None, None) + x.shape[2:], lambda h: (layer, h) + (0,) * nd)

    def state_out(shape):
        nd = len(shape)
        return pl.BlockSpec((None,) + tuple(shape), lambda h: (h,) + (0,) * nd)

    st_shape = (HD, HD, B)
    o_spec = pl.BlockSpec((HD, n), lambda h: (h, 0))
    o_shape = jax.ShapeDtypeStruct((GW, n), F32)
    st_out = jax.ShapeDtypeStruct((H,) + st_shape, F32)

    sa_new, oa = pl.pallas_call(
        functools.partial(_dec_a_kernel, T=T), grid=(H,),
        in_specs=[head_rows(0), head_rows(1), head_rows(2), gate_row(GL_DECAY_A), gate_row(GL_BETA), state_in(sa_t)],
        out_specs=[state_out(st_shape), o_spec], out_shape=[st_out, o_shape],
        compiler_params=cp1, name="dec_a_l%d" % layer,
    )(a_t, a_t, a_t, g3, g3, sa_t)

    hpg = H // G_B
    sb_new, ob = pl.pallas_call(
        functools.partial(_dec_b_kernel, T=T), grid=(H,),
        in_specs=[head_rows(0),
                  pl.BlockSpec((HD, n), lambda h: (H + h // hpg, 0)),
                  pl.BlockSpec((HD, n), lambda h: (H + G_B + h // hpg, 0)),
                  gate_row(GL_DT_B), gate_row(GL_DT_B2),
                  pl.BlockSpec((None, 1, B), lambda h: (h, 0, 0)), state_in(sb_t)],
        out_specs=[state_out(st_shape), o_spec], out_shape=[st_out, o_shape],
        compiler_params=cp1, name="dec_b_l%d" % layer,
    )(b_t, b_t, b_t, g3, g3, dskip, sb_t)

    sc_new, oc = pl.pallas_call(
        functools.partial(_dec_c_kernel, T=T), grid=(H,),
        in_specs=[head_rows(0), head_rows(1), head_rows(2), state_in(sc_t)],
        out_specs=[state_out(st_shape), o_spec], out_shape=[st_out, o_shape],
        compiler_params=cp1, name="dec_c_l%d" % layer,
    )(c_t, c_t, c_t, sc_t)

    dm3 = dm_t.reshape(dm_t.shape[0], H, 1, B)
    dc_new, dn_new, dm_new, od = pl.pallas_call(
        functools.partial(_dec_d_kernel, T=T), grid=(H,),
        in_specs=[head_rows(0), head_rows(1), head_rows(2), gate_row(GL_I_D), gate_row(GL_F_D),
                  state_in(dc_t), state_in(dn_t), state_in(dm3)],
        out_specs=[state_out(st_shape), state_out((HD, B)), state_out((1, B)), o_spec],
        out_shape=[st_out, jax.ShapeDtypeStruct((H, HD, B), F32), jax.ShapeDtypeStruct((H, 1, B), F32), o_shape],
        compiler_params=cp1, name="dec_d_l%d" % layer,
    )(d_t, d_t, d_t, g3, g3, dc_t, dn_t, dm3)

    y = pl.pallas_call(
        _sample_post_kernel, grid=(T,),
        in_specs=[lane_blk(GW)] * 4 + [pl.BlockSpec((B, U_COLS), lambda t: (t, 0))]
                 + [whole(p) for p in (nag, nbg, ncg, ndg)],
        out_specs=pl.BlockSpec((B, D_MODEL), lambda t: (t, 0)),
        out_shape=jax.ShapeDtypeStruct((n, D_MODEL), F32),
        compiler_params=cp1, name="sample_post_l%d" % layer,
    )(oa, ob, oc, od, u, nag, nbg, ncg, ndg)
    return y, (ca_new, sa_new, cb_new, sb_new, sc_new, dc_new, dn_new, dm_new.reshape(H, B))


def _prep_w_in_kernel(w_ref, o_ref):
    w = w_ref[...]
    o_ref[:, 0:1024] = w[:, 0:1024].astype(BF16)
    o_ref[:, 1024:1792] = w[:, 1032:1800].astype(BF16)
    o_ref[:, 1792:OFF_GATES] = w[:, 1804:3852].astype(BF16)
    gates = jnp.concatenate([w[:, 1024:1032], w[:, 1800:1804], w[:, 3852:3860], w[:, 1800:1804],
                             jnp.zeros((w.shape[0], U_COLS - OFF_GATES - N_GATE_COLS), F32)], axis=1)
    o_ref[:, OFF_GATES:U_COLS] = gates.astype(BF16)


def _prep_w_in(w_in):
    depth, d, cols = w_in.shape
    rows = 256
    return pl.pallas_call(
        _prep_w_in_kernel,
        grid=(depth, d // rows),
        in_specs=[pl.BlockSpec((None, rows, cols), lambda l, i: (l, i, 0))],
        out_specs=pl.BlockSpec((None, rows, U_COLS), lambda l, i: (l, i, 0)),
        out_shape=jax.ShapeDtypeStruct((depth, d, U_COLS), BF16),
        compiler_params=pltpu.CompilerParams(dimension_semantics=("arbitrary", "arbitrary"),
                                             vmem_limit_bytes=VMEM_LIMIT),
        name="prep_w_in",
    )(w_in)


def _gate_row(parts):
    row = jnp.zeros((LANES,), F32)
    for off, val in parts:
        row = lax.dynamic_update_slice(row, val.astype(F32), (off,))
    return row.reshape(1, LANES)


def _prompt_decoder(x, states, emb_g, emb_b, lb_logits, w_in_p, layer_w):
    bsz, t, _ = x.shape
    n = bsz * t
    h = x.reshape(n, D_MODEL)
    new_states = []
    for l in range(DEPTH):
        (conv_a_w, conv_a_b, a_log_a, dt_bias_a, norm_a_g, conv_b_w, conv_b_b, a_log_b, dt_bias_b,
         d_skip_b, norm_b_g, norm_c_g, i_bias_d, f_bias_d, norm_d_g, w_out, ln1_g, ln1_b, w_up, w_down,
         ln2_g, ln2_b) = [w[l] for w in layer_w]
        if l == 0:
            h, u = _inproj(h, w_in_p, l, emb_g.reshape(1, -1), emb_b.reshape(1, -1), True)
        else:
            (u,) = _inproj(h, w_in_p, l, emb_g.reshape(1, -1), emb_b.reshape(1, -1), False)
        params = (conv_a_w, conv_a_b.reshape(1, -1), conv_b_w, conv_b_b.reshape(1, -1),
                  _gate_row([(GL_DECAY_A, dt_bias_a), (GL_DT_B, dt_bias_b), (GL_I_D, i_bias_d), (GL_F_D, f_bias_d),
                             (GL_DT_B2, dt_bias_b)]),
                  _gate_row([(GL_DECAY_A, a_log_a), (GL_DT_B, a_log_b)]),
                  norm_a_g.reshape(1, -1), norm_b_g.reshape(1, -1), norm_c_g.reshape(1, -1),
                  norm_d_g.reshape(1, -1), jnp.repeat(d_skip_b, HEAD_DIM).reshape(1, -1), lb_logits)
        y, st = _mixers(u.reshape(bsz, t, U_COLS), states, params, layer=l, P=SEQS_PER_STEP_PROMPT)
        new_states.append(st)
        h = _outffn(y.reshape(n, D_MODEL), h, w_out.astype(BF16), ln1_g.reshape(1, -1), ln1_b.reshape(1, -1),
                    w_up.astype(BF16), w_down.astype(BF16), ln2_g.reshape(1, -1), ln2_b.reshape(1, -1))
    return h.reshape(bsz, t, D_MODEL), tuple(jnp.stack(z) for z in zip(*new_states))


def _sample_decoder(x, states, emb_g, emb_b, lb_logits, w_in_p, layer_w):
    bsz, t, _ = x.shape
    n = bsz * t
    h = jnp.transpose(x, (1, 0, 2)).reshape(n, D_MODEL)
    ca, sa, cb, sb, sc, dc, dn, dm = states
    states_t = (jnp.transpose(ca, (0, 2, 1, 3)), jnp.transpose(sa, (0, 2, 3, 4, 1)),
                jnp.transpose(cb, (0, 2, 1, 3)), jnp.transpose(sb, (0, 2, 3, 4, 1)),
                jnp.transpose(sc, (0, 2, 3, 4, 1)), jnp.transpose(dc, (0, 2, 3, 4, 1)),
                jnp.transpose(dn, (0, 2, 3, 1)), jnp.transpose(dm, (0, 2, 1)))
    new_states = []
    for l in range(DEPTH):
        (conv_a_w, conv_a_b, a_log_a, dt_bias_a, norm_a_g, conv_b_w, conv_b_b, a_log_b, dt_bias_b,
         d_skip_b, norm_b_g, norm_c_g, i_bias_d, f_bias_d, norm_d_g, w_out, ln1_g, ln1_b, w_up, w_down,
         ln2_g, ln2_b) = [w[l] for w in layer_w]
        if l == 0:
            h, u = _inproj(h, w_in_p, l, emb_g.reshape(1, -1), emb_b.reshape(1, -1), True)
        else:
            (u,) = _inproj(h, w_in_p, l, emb_g.reshape(1, -1), emb_b.reshape(1, -1), False)
        params = (conv_a_w, conv_a_b.reshape(1, -1), conv_b_w, conv_b_b.reshape(1, -1),
                  _gate_row([(GL_DECAY_A, dt_bias_a), (GL_DT_B, dt_bias_b), (GL_I_D, i_bias_d), (GL_F_D, f_bias_d),
                             (GL_DT_B2, dt_bias_b)]),
                  _gate_row([(GL_DECAY_A, a_log_a), (GL_DT_B, a_log_b)]),
                  norm_a_g.reshape(1, -1), norm_b_g.reshape(1, -1), norm_c_g.reshape(1, -1),
                  norm_d_g.reshape(1, -1), jnp.broadcast_to(d_skip_b[:, None, None], (N_HEADS, 1, LANES)),
                  lb_logits)
        y, st = _sample_mixers(u, states_t, params, layer=l, T=t)
        new_states.append(st)
        h = _outffn(y, h, w_out.astype(BF16), ln1_g.reshape(1, -1), ln1_b.reshape(1, -1),
                    w_up.astype(BF16), w_down.astype(BF16), ln2_g.reshape(1, -1), ln2_b.reshape(1, -1))
    ca_n, sa_n, cb_n, sb_n, sc_n, dc_n, dn_n, dm_n = (jnp.stack(z) for z in zip(*new_states))
    out_states = (jnp.transpose(ca_n, (0, 2, 1, 3)), jnp.transpose(sa_n, (0, 4, 1, 2, 3)),
                  jnp.transpose(cb_n, (0, 2, 1, 3)), jnp.transpose(sb_n, (0, 4, 1, 2, 3)),
                  jnp.transpose(sc_n, (0, 4, 1, 2, 3)), jnp.transpose(dc_n, (0, 4, 1, 2, 3)),
                  jnp.transpose(dn_n, (0, 3, 1, 2)), jnp.transpose(dm_n, (0, 2, 1)))
    return jnp.transpose(h.reshape(t, bsz, D_MODEL), (1, 0, 2)), out_states


def kernel(x_prompt, x_sample, state_a_conv, state_a_ssm, state_b_conv, state_b_ssm, state_c_ssm, state_d_cmem, state_d_nvec, state_d_mstab, emb_ln_g, emb_ln_b, lb_logits_c, w_in, conv_a_w, conv_a_b, a_log_a, dt_bias_a, norm_a_g, conv_b_w, conv_b_b, a_log_b, dt_bias_b, d_skip_b, norm_b_g, norm_c_g, i_bias_d, f_bias_d, norm_d_g, w_out, ln1_g, ln1_b, w_up, w_down, ln2_g, ln2_b):
    layer_w = (conv_a_w, conv_a_b, a_log_a, dt_bias_a, norm_a_g, conv_b_w, conv_b_b, a_log_b, dt_bias_b,
               d_skip_b, norm_b_g, norm_c_g, i_bias_d, f_bias_d, norm_d_g, w_out, ln1_g, ln1_b, w_up, w_down,
               ln2_g, ln2_b)
    sample_states = (state_a_conv, state_a_ssm, state_b_conv, state_b_ssm,
                     state_c_ssm, state_d_cmem, state_d_nvec, state_d_mstab)
    prompt_states = tuple(jnp.zeros((1, x_prompt.shape[0]) + s.shape[2:], F32) for s in sample_states)
    w_in_p = _prep_w_in(w_in)
    y_p, ps = _prompt_decoder(x_prompt, prompt_states, emb_ln_g, emb_ln_b, lb_logits_c, w_in_p, layer_w)
    y_s, ss = _sample_decoder(x_sample, sample_states, emb_ln_g, emb_ln_b, lb_logits_c, w_in_p, layer_w)
    return (y_p, y_s) + ps + ss
```

```python
import functools

import jax
import jax.numpy as jnp
from jax import lax
from jax.experimental import pallas as pl
from jax.experimental.pallas import tpu as pltpu

F32 = jnp.float32
BF16 = jnp.bfloat16

D_MODEL = 1024
DEPTH = 2
N_HEADS = 4
HEAD_DIM = 64
GROUP_WIDTH = N_HEADS * HEAD_DIM
G_B = 2
CONV_K = 4
CHUNK = 64
D_FF = 4 * D_MODEL
EPS = 1e-6
NEG = -1e30
ALPHA = (2 * DEPTH) ** 0.25
LOG2E = 1.4426950408889634

CONV_A_CH = 3 * GROUP_WIDTH
CONV_B_CH = GROUP_WIDTH + 2 * G_B * HEAD_DIM

OFF_A_QKV = 0
OFF_A_Z = 768
OFF_B_Z = 1024
OFF_B_XBC = 1280
OFF_C_Q = 1792
OFF_C_F = 2048
OFF_C_I = 2304
OFF_C_G = 2560
OFF_D_Q = 2816
OFF_D_K = 3072
OFF_D_V = 3328
OFF_D_O = 3584
OFF_GATES = 3840
U_COLS = 3968
LANES = 128
GL_BETA = 0
GL_DECAY_A = 4
GL_DT_B = 8
GL_I_D = 12
GL_F_D = 16
GL_DT_B2 = 20
N_GATE_COLS = 24

CONV_TAIL = CONV_K - 1
ROW0 = 8

VMEM_LIMIT = 56 * 1024 * 1024
TOKEN_TILE = 512
SUB_TILE = 256
SEQS_PER_STEP_PROMPT = 8
START_C = 2
START_D = 7
START_B = 10
SEQ_STAGGER = 0


def _layernorm(x, g, b):
    mu = jnp.mean(x, axis=-1, keepdims=True)
    xc = x - mu
    var = jnp.mean(xc * xc, axis=-1, keepdims=True)
    return xc * lax.rsqrt(var + EPS) * g + b


def _sigmoid(x):
    return 1.0 / (1.0 + jnp.exp(-x))


def _silu(x):
    return x * _sigmoid(x)


def _dot(a, b):
    return jnp.dot(a, b, preferred_element_type=F32)


def _mm(a, b):
    return _dot(a.astype(BF16), b.astype(BF16))


def _nt(a, b):
    return lax.dot_general(a.astype(BF16), b.astype(BF16), (((1,), (1,)), ((), ())),
                           preferred_element_type=F32)


def _split3(x):
    hi = x.astype(BF16)
    r1 = x - hi.astype(F32)
    mid = r1.astype(BF16)
    r2 = r1 - mid.astype(F32)
    return hi, mid, r2.astype(BF16)


def _dot01(m01, parts):
    acc = _dot(m01, parts[0])
    for p in parts[1:]:
        acc = acc + _dot(m01, p)
    return acc


def _rms(x):
    return x * lax.rsqrt(jnp.mean(x * x, axis=-1, keepdims=True) + EPS)


def _levels(L):
    out, b = [], 1
    while 2 * b <= L:
        out.append(b)
        b *= 2
    return out


def _interleaved(chains):
    pending = [ch if isinstance(ch, tuple) else (0, ch) for ch in chains]
    rnd = 0
    while pending:
        alive = []
        for start, ch in pending:
            if start > rnd:
                alive.append((start, ch))
                continue
            try:
                next(ch)
                alive.append((start, ch))
            except StopIteration:
                pass
        pending = alive
        rnd += 1
        yield


def _run_interleaved(chains):
    for _ in _interleaved(chains):
        pass


def _row_parts(n_rows):
    step = min(SUB_TILE, n_rows)
    return [slice(r, r + step) for r in range(0, n_rows, step)]


def _inproj_kernel(x_ref, w_ref, g_ref, b_ref, *out_refs, apply_ln):
    def part(rows):
        x = x_ref[rows, :]
        if apply_ln:
            x = _layernorm(x, g_ref[...], b_ref[...])
            out_refs[0][rows, :] = x
        yield
        out_refs[-1][rows, :] = _dot(x.astype(BF16), w_ref[...])

    _run_interleaved([part(rows) for rows in _row_parts(x_ref.shape[0])])


def _inproj(x, w_in_p, layer, g, b, apply_ln):
    n = x.shape[0]
    tm = min(TOKEN_TILE, n)
    grid = (n // tm,)
    const = lambda i: (0, 0)
    out_shape = [jax.ShapeDtypeStruct((n, U_COLS), F32)]
    out_specs = [pl.BlockSpec((tm, U_COLS), lambda i: (i, 0))]
    if apply_ln:
        out_shape = [jax.ShapeDtypeStruct((n, D_MODEL), F32)] + out_shape
        out_specs = [pl.BlockSpec((tm, D_MODEL), lambda i: (i, 0))] + out_specs
    return pl.pallas_call(
        functools.partial(_inproj_kernel, apply_ln=apply_ln),
        grid=grid,
        in_specs=[pl.BlockSpec((tm, D_MODEL), lambda i: (i, 0)),
                  pl.BlockSpec((None, D_MODEL, U_COLS), lambda i: (layer, 0, 0), pipeline_mode=pl.Buffered(1)),
                  pl.BlockSpec((1, D_MODEL), const),
                  pl.BlockSpec((1, D_MODEL), const)],
        out_specs=out_specs,
        out_shape=out_shape,
        compiler_params=pltpu.CompilerParams(dimension_semantics=("arbitrary",),
                                             vmem_limit_bytes=VMEM_LIMIT),
        name="inproj_ln" if apply_ln else "inproj",
    )(x, w_in_p, g, b)


def _outffn_kernel(mix_ref, x_ref, wo_ref, g1_ref, b1_ref, wu_ref, wd_ref, g2_ref, b2_ref, o_ref):
    def part(rows):
        m = _dot(mix_ref[rows, :].astype(BF16), wo_ref[...])
        yield
        h1 = _layernorm(ALPHA * x_ref[rows, :] + m, g1_ref[...], b1_ref[...])
        up = _dot(h1.astype(BF16), wu_ref[...])
        yield
        act = jnp.square(jnp.maximum(up, 0.0))
        ff = _dot(act.astype(BF16), wd_ref[...])
        yield
        o_ref[rows, :] = _layernorm(ALPHA * h1 + ff, g2_ref[...], b2_ref[...])

    _run_interleaved([part(rows) for rows in _row_parts(mix_ref.shape[0])])


def _outffn(mix, x, wo, g1, b1, wu, wd, g2, b2):
    n = x.shape[0]
    tm = min(TOKEN_TILE, n)
    const = lambda i: (0, 0)
    tile = pl.BlockSpec((tm, D_MODEL), lambda i: (i, 0))
    row = pl.BlockSpec((1, D_MODEL), const)
    return pl.pallas_call(
        _outffn_kernel,
        grid=(n // tm,),
        in_specs=[tile, tile,
                  pl.BlockSpec((D_MODEL, D_MODEL), const, pipeline_mode=pl.Buffered(1)),
                  row, row,
                  pl.BlockSpec((D_MODEL, D_FF), const, pipeline_mode=pl.Buffered(1)),
                  pl.BlockSpec((D_FF, D_MODEL), const, pipeline_mode=pl.Buffered(1)),
                  row, row],
        out_specs=tile,
        out_shape=jax.ShapeDtypeStruct((n, D_MODEL), F32),
        compiler_params=pltpu.CompilerParams(dimension_semantics=("arbitrary",),
                                             vmem_limit_bytes=VMEM_LIMIT),
        name="outffn",
    )(mix, x, wo, g1, b1, wu, wd, g2, b2)


def _mixer_pair_kernel(u_ref, ca_in, sa_in, cb_in, sb_in, sc_in, dc_in, dn_in, dm_in,
                       caw_ref, cab_ref, cbw_ref, cbb_ref, gbias_ref, galog_ref,
                       nag_ref, nbg_ref, ncg_ref, ndg_ref, dskip_ref, lblog_ref,
                       y_ref, ca_out, sa_out, cb_out, sb_out, sc_out, dc_out, dn_out, dm_out,
                       cbuf, sa, sb, sc, dc, dn, dm, exp01_s,
                       *, NC, P, layer):
    c = pl.program_id(1)
    H, HD, L, W = N_HEADS, HEAD_DIM, CHUNK, LANES
    NP = H // 2
    R0, R1 = ROW0, ROW0 + L
    CB_A, CB_B = 0, CONV_A_CH
    f32 = F32

    def iota(shape, d):
        return lax.broadcasted_iota(jnp.int32, shape, d)

    def one_bf16(mask):
        return jnp.where(mask, 1.0, 0.0).astype(BF16)

    r_ll, c_ll = iota((L, L), 0), iota((L, L), 1)
    tril01 = one_bf16(r_ll >= c_ll)
    r_lw, c_lw = iota((L, W), 0), iota((L, W), 1)
    r_lg = iota((L, GROUP_WIDTH), 0)
    j_lw = c_lw & (HD - 1)
    incl_p = r_lw >= j_lw
    incl01 = jnp.where(incl_p, 1.0, 0.0)
    strict01 = jnp.where(r_lw > j_lw, 1.0, 0.0)
    levels = _levels(L)
    lm01 = {}
    for b in levels:
        sh = b.bit_length() - 1
        rb, cb = r_lw >> sh, j_lw >> sh
        lm01[b] = jnp.where((rb - cb == 1) & ((cb & 1) == 0), 1.0, 0.0)
    r_ww, c_ww = iota((W, W), 0), iota((W, W), 1)
    bdiag = (r_ww >> 6) == (c_ww >> 6)
    bd01 = jnp.where(bdiag, 1.0, 0.0)
    ones_bd = bd01.astype(BF16)
    ones_ww = jnp.ones((W, W), BF16)
    lane = iota((1, W), 1)
    lo_half = lane < HD
    is_dt = (lane >= GL_DECAY_A) & (lane < GL_I_D)
    is_i = (lane >= GL_I_D) & (lane < GL_F_D)
    is_f = (lane >= GL_F_D) & (lane < GL_F_D + H)
    is_cum = is_dt | is_f
    cum_scale = jnp.where(is_dt, LOG2E, 1.0)

    @pl.when((pl.program_id(0) == 0) & (c == 0))
    def _():
        exp_bases = (GL_DECAY_A, GL_BETA, GL_DT_B, GL_DT_B2, GL_F_D, GL_I_D)
        for i, (base, p) in enumerate([(base, p) for base in exp_bases for p in range(NP)]):
            exp01_s[:, i * W:(i + 1) * W] = one_bf16(r_ww == (base + 2 * p + (c_ww >> 6)))

    exp01 = exp01_s[...]

    def bd(x):
        xb = x.astype(BF16)
        return jnp.concatenate([xb, xb], axis=0) * ones_bd

    def half_sums(x):
        return _dot(x.astype(BF16), ones_bd)

    def transpose_bf16(x):
        return x.T.astype(BF16)

    pl_sm = lblog_ref[...]
    pl_sm = jnp.exp(pl_sm - jnp.max(pl_sm, axis=0, keepdims=True))
    pl_sm = pl_sm / jnp.sum(pl_sm, axis=0, keepdims=True)
    lb = pl_sm[0:1, :]
    for i in range(1, layer + 1):
        lb = lb + pl_sm[i:i + 1, :]
    lb = lb - pl_sm[0:1, :]

    @pl.when(c == 0)
    def _():
        cbuf[:, R0 - CONV_TAIL:R0, CB_A:CB_A + CONV_A_CH] = ca_in[...]
        cbuf[:, R0 - CONV_TAIL:R0, CB_B:CB_B + CONV_B_CH] = cb_in[...]
        zero = jnp.zeros((W, W), f32)
        for s in range(P):
            m_row = dm_in[s]
            for p in range(NP):
                h0, h1 = 2 * p, 2 * p + 1
                for ref, src in ((sa, sa_in), (sc, sc_in), (dc, dc_in)):
                    ref[s, p] = zero
                    ref[s, p, 0:HD, 0:HD] = src[s, h0]
                    ref[s, p, HD:W, HD:W] = src[s, h1]
                sb[s, p] = zero
                sb[s, p, p * HD:(p + 1) * HD, 0:HD] = sb_in[s, h0]
                sb[s, p, p * HD:(p + 1) * HD, HD:W] = sb_in[s, h1]
                dn[s, p] = jnp.concatenate([dn_in[s, h0:h0 + 1, :], dn_in[s, h1:h1 + 1, :]], axis=1)
                dm[s, p] = jnp.where(lo_half, m_row[:, h0:h0 + 1], m_row[:, h1:h1 + 1])

    cbuf[:, R0:R1, CB_A:CB_A + CONV_A_CH] = u_ref[:, :, OFF_A_QKV:OFF_A_QKV + CONV_A_CH]
    cbuf[:, R0:R1, CB_B:CB_B + CONV_B_CH] = u_ref[:, :, OFF_B_XBC:OFF_B_XBC + CONV_B_CH]

    def seq_chains(s):
        def seg(off, width):
            return u_ref[s, :, off:off + width]

        def conv(off, width, w_ref, b_ref):
            acc = b_ref[...] + cbuf[s, R0:R1, off:off + width] * w_ref[CONV_K - 1:CONV_K, :]
            for j in range(CONV_K - 1):
                lo = R0 - CONV_TAIL + j
                acc = acc + cbuf[s, lo:lo + L, off:off + width] * w_ref[j:j + 1, :]
            return acc

        pre = seg(OFF_GATES, W) + gbias_ref[...]
        e = jnp.exp(-jnp.abs(pre))
        l1p = jnp.log1p(e)
        softplus = jnp.maximum(pre, 0.0) + l1p
        logsig = jnp.minimum(pre, 0.0) - l1p
        logdec = jnp.where(is_dt, -jnp.exp(galog_ref[...]) * softplus, jnp.where(is_f, logsig, 0.0))
        gval = jnp.where(lane < GL_DECAY_A, _sigmoid(pre), jnp.where(is_i, pre, softplus))
        gcum = _dot01(tril01, _split3(logdec))
        fgate = lb + (1.0 - lb) * _sigmoid(seg(OFF_C_F, GROUP_WIDTH))
        gc = _dot01(tril01, _split3(jnp.log(fgate))) * LOG2E
        qkv = _silu(conv(CB_A, CONV_A_CH, caw_ref, cab_ref))
        qk_raw = qkv[:, 0:2 * GROUP_WIDTH]
        ss_qk = jnp.concatenate([half_sums(qk_raw[:, i * W:(i + 1) * W] * qk_raw[:, i * W:(i + 1) * W])
                                 for i in range(2 * NP)], axis=1)
        yield
        gsrc = jnp.where(is_cum, gcum * cum_scale, gval)
        src = _split3(gsrc)
        expd = _dot(src[0], exp01) + _dot(src[1], exp01) + _dot(src[2], exp01)
        gsrc_t = gsrc.T

        def expanded(k, p):
            o = (k * NP + p) * W
            return expd[:, o:o + W]

        def row_pair(base, p):
            r = base + 2 * p
            return jnp.concatenate([gsrc_t[r:r + 1, :], gsrc_t[r + 1:r + 2, :]], axis=1)

        def chain_a(p):
            ps = slice(p * W, (p + 1) * W)
            q = qkv[:, p * W:(p + 1) * W]
            k = qkv[:, GROUP_WIDTH + p * W:GROUP_WIDTH + (p + 1) * W]
            v = qkv[:, 2 * GROUP_WIDTH + p * W:2 * GROUP_WIDTH + (p + 1) * W]
            ssq = ss_qk[:, p * W:(p + 1) * W]
            ssk = ss_qk[:, GROUP_WIDTH + p * W:GROUP_WIDTH + (p + 1) * W]
            g = expanded(0, p)
            beta = expanded(1, p)
            grow = row_pair(GL_DECAY_A, p)
            q = q * lax.rsqrt(ssq + EPS) * (HD ** -0.5)
            k = k * lax.rsqrt(ssk + EPS)
            eg = jnp.exp2(g)
            g_last = g[L - 1:L, :]
            kbd = bd(k)
            kk = _nt(k, kbd)
            qk = _nt(q, kbd)
            kw_t = transpose_bf16(k * jnp.exp2(g_last - g))
            dmat = jnp.exp2(jnp.minimum(g - grow, 0.0)) * incl01
            yield
            nmat = (beta * kk) * (dmat * strict01)
            low = -(nmat * lm01[1])
            for b in levels[1:]:
                off = nmat * lm01[b]
                x = off + _dot(off.astype(BF16), bd(low))
                yield
                low = low - x - _dot(low.astype(BF16), bd(x))
                yield
            rv = beta * v
            rk = (beta * eg) * k
            rhs_bd = jnp.concatenate([bd(rv), bd(rk)], axis=1)
            sol = jnp.concatenate([rv, rk], axis=1) + _dot(low.astype(BF16), rhs_bd)
            s0 = sa[s, p]
            qs = _mm(q, s0)
            yield
            unew = sol[:, :W] - _mm(sol[:, W:], s0)
            yield
            o = eg * qs + _mm(qk * dmat, bd(unew))
            sa[s, p] = jnp.exp2(g_last) * s0 + _mm(kw_t, unew) * bd01
            ms = half_sums(o * o) * (1.0 / HD)
            yield
            a_z = seg(OFF_A_Z + p * W, W)
            y_ref[s, :, ps] = o * lax.rsqrt(ms + EPS) * nag_ref[:, ps] * _silu(a_z)

        xbc = _silu(conv(CB_B, CONV_B_CH, cbw_ref, cbb_ref))
        b_all = xbc[:, GROUP_WIDTH:GROUP_WIDTH + W]
        c_all = xbc[:, GROUP_WIDTH + W:GROUP_WIDTH + 2 * W]
        b_t = transpose_bf16(b_all)

        def chain_b(p):
            ps = slice(p * W, (p + 1) * W)
            b_grp = b_all * jnp.where((lane >> 6) == p, 1.0, 0.0)
            cb = _nt(c_all, jnp.concatenate([b_grp, b_grp], axis=0))
            g = expanded(2, p)
            dt = expanded(3, p)
            grow = row_pair(GL_DT_B, p)
            yield
            g_last = g[L - 1:L, :]
            xh = xbc[:, ps]
            v = xh * dt
            s0 = sb[s, p]
            dmat = jnp.exp2(jnp.minimum(g - grow, 0.0)) * incl01
            o = jnp.exp2(g) * _mm(c_all, s0) + _mm(cb * dmat, bd(v))
            upd = _dot(b_t, (v * jnp.exp2(g_last - g)).astype(BF16))
            sb[s, p] = jnp.exp2(g_last) * s0 + upd * jnp.where((r_ww >> 6) == p, 1.0, 0.0)
            yield
            ob = (o + dskip_ref[:, ps] * xh) * _silu(seg(OFF_B_Z + p * W, W))
            ms = _dot((ob * ob).astype(BF16), ones_ww) * (1.0 / W)
            yield
            y_ref[s, :, GROUP_WIDTH + p * W:GROUP_WIDTH + (p + 1) * W] = ob * lax.rsqrt(ms + EPS) * nbg_ref[:, ps]

        kc = 1.0 - fgate
        pref = {}
        for b in levels:
            blk = 2 * b
            if blk >= 8:
                pref[b] = jnp.concatenate(
                    [jnp.broadcast_to(gc[i * blk + b - 1:i * blk + b, :], (blk, GROUP_WIDTH)) for i in range(L // blk)],
                    axis=0)
            else:
                acc = gc
                for d in range(blk):
                    sh = d - (b - 1)
                    if sh != 0:
                        acc = jnp.where((r_lg & (blk - 1)) == d, pltpu.roll(gc, sh % L, axis=0), acc)
                pref[b] = acc

        def chain_c(p):
            ps = slice(p * W, (p + 1) * W)
            q = seg(OFF_C_Q + p * W, W)
            v = seg(OFF_C_I + p * W, W)
            k, g = kc[:, ps], gc[:, ps]
            g_last = g[L - 1:L, :]
            kw_t = transpose_bf16(k * jnp.exp2(g_last - g))
            e_col = jnp.exp2(jnp.broadcast_to(g_last, (8, W)).T[:, 0:1])
            qk_diag = half_sums(q * k)
            yield
            att = None
            for b in levels:
                pb = pref[b][:, ps]
                kd = bd(k * jnp.exp2(jnp.minimum(pb - g, 0.0)))
                part = _nt(q * jnp.exp2(jnp.minimum(g - pb, 0.0)), kd) * lm01[b]
                att = part if att is None else att + part
                yield
            s0 = sc[s, p]
            o = _mm(q * jnp.exp2(g), s0) + _mm(att, bd(v)) + qk_diag * v
            sc[s, p] = e_col * s0 + _mm(kw_t, v) * bd01
            ms = half_sums(o * o) * (1.0 / HD)
            yield
            c_g = seg(OFF_C_G + p * W, W)
            y_ref[s, :, 2 * GROUP_WIDTH + p * W:2 * GROUP_WIDTH + (p + 1) * W] = (
                o * lax.rsqrt(ms + EPS) * ncg_ref[:, ps] * _sigmoid(c_g))

        def chain_d(p):
            ps = slice(p * W, (p + 1) * W)
            q = seg(OFF_D_Q + p * W, W)
            k = seg(OFF_D_K + p * W, W) * (HD ** -0.5)
            v = seg(OFF_D_V + p * W, W)
            bcum = expanded(4, p)
            ipre = expanded(5, p)
            m0 = dm[s, p]
            a = ipre - bcum
            arow = row_pair(GL_I_D, p) - row_pair(GL_F_D, p)
            qk = _nt(q, bd(k).astype(BF16))
            cmem = dc[s, p]
            qc = _mm(q, cmem)
            nvec = dn[s, p]
            qn = half_sums(q * nvec)
            cm = a
            sh = 1
            while sh < L:
                cm = jnp.maximum(cm, jnp.where(r_lw >= sh, pltpu.roll(cm, sh, axis=0), NEG))
                sh *= 2
            yield
            m_r = bcum + jnp.maximum(m0, cm)
            pmat = jnp.where(incl_p, jnp.exp(bcum + arow - m_r), 0.0) * qk
            s_init = jnp.exp(bcum + m0 - m_r)
            num = s_init * qc + _mm(pmat, bd(v))
            den = s_init * qn + half_sums(pmat)
            m_last = m_r[L - 1:L, :]
            b_last = bcum[L - 1:L, :]
            scale = jnp.exp(b_last + m0 - m_last)
            kw = k * jnp.exp(b_last - bcum + ipre - m_last)
            kw_t = transpose_bf16(kw)
            yield
            dc[s, p] = scale * cmem + _mm(kw_t, v) * bd01
            dn[s, p] = scale * nvec + jnp.sum(kw, axis=0, keepdims=True)
            dm[s, p] = m_last
            hh = num / jnp.maximum(jnp.abs(den), jnp.exp(-m_r))
            ms = half_sums(hh * hh) * (1.0 / HD)
            yield
            d_o = seg(OFF_D_O + p * W, W)
            y_ref[s, :, 3 * GROUP_WIDTH + p * W:3 * GROUP_WIDTH + (p + 1) * W] = (
                hh * lax.rsqrt(ms + EPS) * ndg_ref[:, ps] * _sigmoid(d_o))

        return ([chain_a(p) for p in range(NP)] + [(START_C, chain_c(p)) for p in range(NP)]
                + [(START_D, chain_d(p)) for p in range(NP)] + [(START_B, chain_b(p)) for p in range(NP)])

    def seq_driver(s):
        chains = yield from seq_chains(s)
        yield
        yield from _interleaved(chains)

    _run_interleaved([(s * SEQ_STAGGER, seq_driver(s)) for s in range(P)])

    tail_a = cbuf[:, R1 - CONV_TAIL:R1, CB_A:CB_A + CONV_A_CH]
    tail_b = cbuf[:, R1 - CONV_TAIL:R1, CB_B:CB_B + CONV_B_CH]
    cbuf[:, R0 - CONV_TAIL:R0, CB_A:CB_A + CONV_A_CH] = tail_a
    cbuf[:, R0 - CONV_TAIL:R0, CB_B:CB_B + CONV_B_CH] = tail_b

    @pl.when(c == NC - 1)
    def _():
        ca_out[...] = tail_a
        cb_out[...] = tail_b
        for s in range(P):
            m_row = jnp.zeros((1, W), f32)
            for p in range(NP):
                h0, h1 = 2 * p, 2 * p + 1
                for ref, dst in ((sa, sa_out), (sc, sc_out), (dc, dc_out)):
                    dst[s, h0] = ref[s, p, 0:HD, 0:HD]
                    dst[s, h1] = ref[s, p, HD:W, HD:W]
                sb_out[s, h0] = sb[s, p, p * HD:(p + 1) * HD, 0:HD]
                sb_out[s, h1] = sb[s, p, p * HD:(p + 1) * HD, HD:W]
                nrow = dn[s, p]
                dn_out[s, h0:h0 + 1, :] = nrow[:, 0:HD]
                dn_out[s, h1:h1 + 1, :] = nrow[:, HD:W]
                mp = dm[s, p]
                m_row = jnp.where(lane == h0, mp[:, 0:1], jnp.where(lane == h1, mp[:, HD:HD + 1], m_row))
            dm_out[s] = m_row


def _mixers(u, states, params, *, layer, P):
    bsz, ttot, _ = u.shape
    T = L = CHUNK
    nc = ttot // T
    H, HD = N_HEADS, HEAD_DIM
    ca, s_a, cb, s_b, s_c, d_c, d_n, d_m = states
    d_m = jnp.pad(d_m, ((0, 0), (0, 0), (0, LANES - H))).reshape(d_m.shape[0], bsz, 1, LANES)

    def per_seq(shape):
        nd = len(shape)
        return pl.BlockSpec((P,) + tuple(shape), lambda b, c: (b,) + (0,) * nd)

    def per_seq_in(x, shape):
        nd = len(shape)
        lidx = layer if x.shape[0] > 1 else 0
        return pl.BlockSpec((None, P) + tuple(shape), lambda b, c: (lidx, b) + (0,) * nd)

    def whole(x):
        nd = x.ndim
        return pl.BlockSpec(x.shape, lambda b, c: (0,) * nd)

    state_dims = [(CONV_TAIL, CONV_A_CH), (H, HD, HD), (CONV_TAIL, CONV_B_CH), (H, HD, HD), (H, HD, HD),
                  (H, HD, HD), (H, HD), (1, LANES)]
    state_specs = [per_seq(d) for d in state_dims]
    state_shapes = [jax.ShapeDtypeStruct((bsz,) + d, F32) for d in state_dims]
    pair_state = (P, H // 2, LANES, LANES)
    scratch = ([pltpu.VMEM((P, ROW0 + L, CONV_A_CH + CONV_B_CH), F32)]
               + [pltpu.VMEM(pair_state, F32) for _ in range(4)]
               + [pltpu.VMEM((P, H // 2, 1, LANES), F32) for _ in range(2)]
               + [pltpu.VMEM((LANES, 6 * (H // 2) * LANES), BF16)])
    outs = pl.pallas_call(
        functools.partial(_mixer_pair_kernel, NC=nc, P=P, layer=layer),
        grid=(bsz // P, nc),
        in_specs=[pl.BlockSpec((P, T, U_COLS), lambda b, c: (b, c, 0))]
                 + [per_seq_in(x, d) for x, d in zip((ca, s_a, cb, s_b, s_c, d_c, d_n, d_m), state_dims)]
                 + [whole(p) for p in params],
        out_specs=[pl.BlockSpec((P, T, D_MODEL), lambda b, c: (b, c, 0))] + state_specs,
        out_shape=[jax.ShapeDtypeStruct((bsz, ttot, D_MODEL), F32)] + state_shapes,
        scratch_shapes=scratch,
        compiler_params=pltpu.CompilerParams(dimension_semantics=("arbitrary", "arbitrary"),
                                             vmem_limit_bytes=VMEM_LIMIT),
        name="mixers_l%d" % layer,
    )(u, ca, s_a, cb, s_b, s_c, d_c, d_n, d_m, *params)
    y, new = outs[0], list(outs[1:])
    new[7] = new[7].reshape(bsz, LANES)[:, :H]
    return y, tuple(new)


DEC_GATE_ROWS = 128


def _sample_pre_kernel(u_ref, ca_ref, cb_ref, caw_ref, cab_ref, cbw_ref, cbb_ref, gbias_ref, galog_ref, lblog_ref,
                       at_ref, bt_ref, ct_ref, dt_ref, gt_ref, ca_out, cb_out, hist, *, layer, T):
    t = pl.program_id(0)
    na = CONV_A_CH

    @pl.when(t == 0)
    def _():
        hist[0:CONV_TAIL, :, 0:na] = ca_ref[...]
        hist[0:CONV_TAIL, :, na:] = cb_ref[...]

    hist[CONV_TAIL + t, :, 0:na] = u_ref[:, OFF_A_QKV:OFF_A_QKV + CONV_A_CH]
    hist[CONV_TAIL + t, :, na:] = u_ref[:, OFF_B_XBC:OFF_B_XBC + CONV_B_CH]
    w_all = jnp.concatenate([caw_ref[...], cbw_ref[...]], axis=1)
    acc = jnp.concatenate([cab_ref[...], cbb_ref[...]], axis=1)
    for j in range(CONV_K):
        acc = acc + hist[t + j] * w_all[j:j + 1, :]
    act = _silu(acc)
    at_ref[...] = act[:, 0:na].T
    bt_ref[...] = act[:, na:].T

    lane = lax.broadcasted_iota(jnp.int32, (1, LANES), 1)
    pre = u_ref[:, OFF_GATES:OFF_GATES + LANES] + gbias_ref[...]
    e = jnp.exp(-jnp.abs(pre))
    l1p = jnp.log1p(e)
    softplus = jnp.maximum(pre, 0.0) + l1p
    logsig = jnp.minimum(pre, 0.0) - l1p
    is_dt = (lane >= GL_DECAY_A) & (lane < GL_I_D)
    is_i = (lane >= GL_I_D) & (lane < GL_F_D)
    is_f = (lane >= GL_F_D) & (lane < GL_F_D + N_HEADS)
    decay = jnp.exp(-jnp.exp(galog_ref[...]) * softplus)
    gates = jnp.where(lane < GL_DECAY_A, _sigmoid(pre),
                      jnp.where(is_dt, decay, jnp.where(is_i, pre, jnp.where(is_f, logsig, softplus))))
    gt_ref[...] = gates.T

    pl_sm = lblog_ref[...]
    pl_sm = jnp.exp(pl_sm - jnp.max(pl_sm, axis=0, keepdims=True))
    pl_sm = pl_sm / jnp.sum(pl_sm, axis=0, keepdims=True)
    lb = pl_sm[0:1, :]
    for i in range(1, layer + 1):
        lb = lb + pl_sm[i:i + 1, :]
    lb = lb - pl_sm[0:1, :]
    fgate = lb + (1.0 - lb) * _sigmoid(u_ref[:, OFF_C_F:OFF_C_F + GROUP_WIDTH])
    gw = GROUP_WIDTH
    ct_ref[0:gw, :] = u_ref[:, OFF_C_Q:OFF_C_Q + gw].T
    ct_ref[gw:2 * gw, :] = fgate.T
    ct_ref[2 * gw:3 * gw, :] = u_ref[:, OFF_C_I:OFF_C_I + gw].T
    dt_ref[0:gw, :] = u_ref[:, OFF_D_Q:OFF_D_Q + gw].T
    dt_ref[gw:2 * gw, :] = (u_ref[:, OFF_D_K:OFF_D_K + gw] * (HEAD_DIM ** -0.5)).T
    dt_ref[2 * gw:3 * gw, :] = u_ref[:, OFF_D_V:OFF_D_V + gw].T

    @pl.when(t == T - 1)
    def _():
        ca_out[...] = hist[T:T + CONV_TAIL, :, 0:na]
        cb_out[...] = hist[T:T + CONV_TAIL, :, na:]


def _colsum(x):
    return jnp.sum(x, axis=0, keepdims=True)


def _dec_a_kernel(q_ref, k_ref, v_ref, a_ref, beta_ref, s_in, s_out, o_ref, *, T):
    HD, B = HEAD_DIM, LANES
    for t in range(T):
        ln = slice(t * B, (t + 1) * B)
        q, k, v = q_ref[:, ln], k_ref[:, ln], v_ref[:, ln]
        q = q * (lax.rsqrt(_colsum(q * q) + EPS) * (HD ** -0.5))
        k = k * lax.rsqrt(_colsum(k * k) + EPS)
        a, beta = a_ref[:, ln], beta_ref[:, ln]
        src = s_in if t == 0 else s_out
        ks = [jnp.zeros((HD, B), F32), jnp.zeros((HD, B), F32)]
        for dk in range(HD):
            ks[dk & 1] = ks[dk & 1] + k[dk:dk + 1, :] * src[dk]
        unew = beta * (v - a * (ks[0] + ks[1]))
        o = [jnp.zeros((HD, B), F32), jnp.zeros((HD, B), F32)]
        for dk in range(HD):
            sn = a * src[dk] + k[dk:dk + 1, :] * unew
            s_out[dk] = sn
            o[dk & 1] = o[dk & 1] + q[dk:dk + 1, :] * sn
        ot = o[0] + o[1]
        o_ref[:, ln] = ot * lax.rsqrt(_colsum(ot * ot) * (1.0 / HD) + EPS)


def _dec_b_kernel(x_ref, b_ref, c_ref, a_ref, dt_ref, skip_ref, s_in, s_out, o_ref, *, T):
    HD, B = HEAD_DIM, LANES
    for t in range(T):
        ln = slice(t * B, (t + 1) * B)
        x, bm, cm = x_ref[:, ln], b_ref[:, ln], c_ref[:, ln]
        a = a_ref[:, ln]
        xdt = x * dt_ref[:, ln]
        src = s_in if t == 0 else s_out
        o = [jnp.zeros((HD, B), F32), jnp.zeros((HD, B), F32)]
        for n in range(HD):
            sn = a * src[n] + bm[n:n + 1, :] * xdt
            s_out[n] = sn
            o[n & 1] = o[n & 1] + cm[n:n + 1, :] * sn
        o_ref[:, ln] = o[0] + o[1] + skip_ref[...] * x


def _dec_c_kernel(q_ref, f_ref, v_ref, s_in, s_out, o_ref, *, T):
    HD, B = HEAD_DIM, LANES
    for t in range(T):
        ln = slice(t * B, (t + 1) * B)
        q, f, v = q_ref[:, ln], f_ref[:, ln], v_ref[:, ln]
        k = 1.0 - f
        src = s_in if t == 0 else s_out
        o = [jnp.zeros((HD, B), F32), jnp.zeros((HD, B), F32)]
        for dk in range(HD):
            sn = f[dk:dk + 1, :] * src[dk] + k[dk:dk + 1, :] * v
            s_out[dk] = sn
            o[dk & 1] = o[dk & 1] + q[dk:dk + 1, :] * sn
        ot = o[0] + o[1]
        o_ref[:, ln] = ot * lax.rsqrt(_colsum(ot * ot) * (1.0 / HD) + EPS)


def _dec_d_kernel(q_ref, k_ref, v_ref, i_ref, f_ref, c_in, n_in, m_in, c_out, n_out, m_out, o_ref, *, T):
    HD, B = HEAD_DIM, LANES
    m = m_in[...]
    nvec = n_in[...]
    for t in range(T):
        ln = slice(t * B, (t + 1) * B)
        q, k, v = q_ref[:, ln], k_ref[:, ln], v_ref[:, ln]
        ipre, logf = i_ref[:, ln], f_ref[:, ln]
        m_new = jnp.maximum(logf + m, ipre)
        fs = jnp.exp(logf + m - m_new)
        kw = k * jnp.exp(ipre - m_new)
        src = c_in if t == 0 else c_out
        num = [jnp.zeros((HD, B), F32), jnp.zeros((HD, B), F32)]
        for dk in range(HD):
            cn = fs * src[dk] + kw[dk:dk + 1, :] * v
            c_out[dk] = cn
            num[dk & 1] = num[dk & 1] + q[dk:dk + 1, :] * cn
        nvec = fs * nvec + kw
        den = _colsum(q * nvec)
        hh = (num[0] + num[1]) / jnp.maximum(jnp.abs(den), jnp.exp(-m_new))
        o_ref[:, ln] = hh * lax.rsqrt(_colsum(hh * hh) * (1.0 / HD) + EPS)
        m = m_new
    n_out[...] = nvec
    m_out[...] = m


def _sample_post_kernel(oa_ref, ob_ref, oc_ref, od_ref, u_ref, nag_ref, nbg_ref, ncg_ref, ndg_ref, y_ref):
    gw = GROUP_WIDTH
    y_ref[:, 0:gw] = oa_ref[...].T * nag_ref[...] * _silu(u_ref[:, OFF_A_Z:OFF_A_Z + gw])
    ob = ob_ref[...].T * _silu(u_ref[:, OFF_B_Z:OFF_B_Z + gw])
    grp = gw // G_B
    for g in range(G_B):
        sl = slice(g * grp, (g + 1) * grp)
        y_ref[:, gw + g * grp:gw + (g + 1) * grp] = _rms(ob[:, sl]) * nbg_ref[:, sl]
    y_ref[:, 2 * gw:3 * gw] = oc_ref[...].T * ncg_ref[...] * _sigmoid(u_ref[:, OFF_C_G:OFF_C_G + gw])
    y_ref[:, 3 * gw:4 * gw] = od_ref[...].T * ndg_ref[...] * _sigmoid(u_ref[:, OFF_D_O:OFF_D_O + gw])


def _sample_mixers(u, states_t, params, *, layer, T):
    B, H, HD, GW = LANES, N_HEADS, HEAD_DIM, GROUP_WIDTH
    ca_t, sa_t, cb_t, sb_t, sc_t, dc_t, dn_t, dm_t = states_t
    (caw, cab, cbw, cbb, gbias, galog, nag, nbg, ncg, ndg, dskip, lblog) = params
    cp1 = pltpu.CompilerParams(dimension_semantics=("arbitrary",), vmem_limit_bytes=VMEM_LIMIT)
    n = T * B

    def whole(x):
        nd = x.ndim
        return pl.BlockSpec(x.shape, lambda i: (0,) * nd)

    def lane_blk(rows):
        return pl.BlockSpec((rows, B), lambda t: (0, t))

    a_t, b_t, c_t, d_t, g_t, ca_new, cb_new = pl.pallas_call(
        functools.partial(_sample_pre_kernel, layer=layer, T=T),
        grid=(T,),
        in_specs=[pl.BlockSpec((B, U_COLS), lambda t: (t, 0)),
                  pl.BlockSpec((None, CONV_TAIL, B, CONV_A_CH), lambda t: (layer, 0, 0, 0)),
                  pl.BlockSpec((None, CONV_TAIL, B, CONV_B_CH), lambda t: (layer, 0, 0, 0))]
                 + [whole(p) for p in (caw, cab, cbw, cbb, gbias, galog, lblog)],
        out_specs=[lane_blk(CONV_A_CH), lane_blk(CONV_B_CH), lane_blk(3 * GW), lane_blk(3 * GW),
                   lane_blk(DEC_GATE_ROWS),
                   pl.BlockSpec((CONV_TAIL, B, CONV_A_CH), lambda t: (0, 0, 0)),
                   pl.BlockSpec((CONV_TAIL, B, CONV_B_CH), lambda t: (0, 0, 0))],
        out_shape=[jax.ShapeDtypeStruct((CONV_A_CH, n), F32), jax.ShapeDtypeStruct((CONV_B_CH, n), F32),
                   jax.ShapeDtypeStruct((3 * GW, n), F32), jax.ShapeDtypeStruct((3 * GW, n), F32),
                   jax.ShapeDtypeStruct((DEC_GATE_ROWS, n), F32),
                   jax.ShapeDtypeStruct((CONV_TAIL, B, CONV_A_CH), F32),
                   jax.ShapeDtypeStruct((CONV_TAIL, B, CONV_B_CH), F32)],
        scratch_shapes=[pltpu.VMEM((CONV_TAIL + T, B, CONV_A_CH + CONV_B_CH), F32)],
        compiler_params=cp1, name="sample_pre_l%d" % layer,
    )(u, ca_t, cb_t, caw, cab, cbw, cbb, gbias, galog, lblog)

    g3 = g_t.reshape(DEC_GATE_ROWS, 1, n)

    def head_rows(sec):
        return pl.BlockSpec((HD, n), lambda h: (sec * H + h, 0))

    def gate_row(base):
        return pl.BlockSpec((None, 1, n), lambda h: (base + h, 0, 0))

    def state_in(x):
        nd = x.ndim - 2
        return pl.BlockSpec((None, None) + x.shape[2:], lambda h: (layer, h) + (0,) * nd)

    def state_out(shape):
        nd = len(shape)
        return pl.BlockSpec((None,) + tuple(shape), lambda h: (h,) + (0,) * nd)

    st_shape = (HD, HD, B)
    o_spec = pl.BlockSpec((HD, n), lambda h: (h, 0))
    o_shape = jax.ShapeDtypeStruct((GW, n), F32)
    st_out = jax.ShapeDtypeStruct((H,) + st_shape, F32)

    sa_new, oa = pl.pallas_call(
        functools.partial(_dec_a_kernel, T=T), grid=(H,),
        in_specs=[head_rows(0), head_rows(1), head_rows(2), gate_row(GL_DECAY_A), gate_row(GL_BETA), state_in(sa_t)],
        out_specs=[state_out(st_shape), o_spec], out_shape=[st_out, o_shape],
        compiler_params=cp1, name="dec_a_l%d" % layer,
    )(a_t, a_t, a_t, g3, g3, sa_t)

    hpg = H // G_B
    sb_new, ob = pl.pallas_call(
        functools.partial(_dec_b_kernel, T=T), grid=(H,),
        in_specs=[head_rows(0),
                  pl.BlockSpec((HD, n), lambda h: (H + h // hpg, 0)),
                  pl.BlockSpec((HD, n), lambda h: (H + G_B + h // hpg, 0)),
                  gate_row(GL_DT_B), gate_row(GL_DT_B2),
                  pl.BlockSpec((None, 1, B), lambda h: (h, 0, 0)), state_in(sb_t)],
        out_specs=[state_out(st_shape), o_spec], out_shape=[st_out, o_shape],
        compiler_params=cp1, name="dec_b_l%d" % layer,
    )(b_t, b_t, b_t, g3, g3, dskip, sb_t)

    sc_new, oc = pl.pallas_call(
        functools.partial(_dec_c_kernel, T=T), grid=(H,),
        in_specs=[head_rows(0), head_rows(1), head_rows(2), state_in(sc_t)],
        out_specs=[state_out(st_shape), o_spec], out_shape=[st_out, o_shape],
        compiler_params=cp1, name="dec_c_l%d" % layer,
    )(c_t, c_t, c_t, sc_t)

    dm3 = dm_t.reshape(dm_t.shape[0], H, 1, B)
    dc_new, dn_new, dm_new, od = pl.pallas_call(
        functools.partial(_dec_d_kernel, T=T), grid=(H,),
        in_specs=[head_rows(0), head_rows(1), head_rows(2), gate_row(GL_I_D), gate_row(GL_F_D),
                  state_in(dc_t), state_in(dn_t), state_in(dm3)],
        out_specs=[state_out(st_shape), state_out((HD, B)), state_out((1, B)), o_spec],
        out_shape=[st_out, jax.ShapeDtypeStruct((H, HD, B), F32), jax.ShapeDtypeStruct((H, 1, B), F32), o_shape],
        compiler_params=cp1, name="dec_d_l%d" % layer,
    )(d_t, d_t, d_t, g3, g3, dc_t, dn_t, dm3)

    y = pl.pallas_call(
        _sample_post_kernel, grid=(T,),
        in_specs=[lane_blk(GW)] * 4 + [pl.BlockSpec((B, U_COLS), lambda t: (t, 0))]
                 + [whole(p) for p in (nag, nbg, ncg, ndg)],
        out_specs=pl.BlockSpec((B, D_MODEL), lambda t: (t, 0)),
        out_shape=jax.ShapeDtypeStruct((n, D_MODEL), F32),
        compiler_params=cp1, name="sample_post_l%d" % layer,
    )(oa, ob, oc, od, u, nag, nbg, ncg, ndg)
    return y, (ca_new, sa_new, cb_new, sb_new, sc_new, dc_new, dn_new, dm_new.reshape(H, B))


def _prep_w_in_kernel(w_ref, o_ref):
    w = w_ref[...]
    o_ref[:, 0:1024] = w[:, 0:1024].astype(BF16)
    o_ref[:, 1024:1792] = w[:, 1032:1800].astype(BF16)
    o_ref[:, 1792:OFF_GATES] = w[:, 1804:3852].astype(BF16)
    gates = jnp.concatenate([w[:, 1024:1032], w[:, 1800:1804], w[:, 3852:3860], w[:, 1800:1804],
                             jnp.zeros((w.shape[0], U_COLS - OFF_GATES - N_GATE_COLS), F32)], axis=1)
    o_ref[:, OFF_GATES:U_COLS] = gates.astype(BF16)


def _prep_w_in(w_in):
    depth, d, cols = w_in.shape
    rows = 256
    return pl.pallas_call(
        _prep_w_in_kernel,
        grid=(depth, d // rows),
        in_specs=[pl.BlockSpec((None, rows, cols), lambda l, i: (l, i, 0))],
        out_specs=pl.BlockSpec((None, rows, U_COLS), lambda l, i: (l, i, 0)),
        out_shape=jax.ShapeDtypeStruct((depth, d, U_COLS), BF16),
        compiler_params=pltpu.CompilerParams(dimension_semantics=("arbitrary", "arbitrary"),
                                             vmem_limit_bytes=VMEM_LIMIT),
        name="prep_w_in",
    )(w_in)


def _gate_row(parts):
    row = jnp.zeros((LANES,), F32)
    for off, val in parts:
        row = lax.dynamic_update_slice(row, val.astype(F32), (off,))
    return row.reshape(1, LANES)


def _prompt_decoder(x, states, emb_g, emb_b, lb_logits, w_in_p, layer_w):
    bsz, t, _ = x.shape
    n = bsz * t
    h = x.reshape(n, D_MODEL)
    new_states = []
    for l in range(DEPTH):
        (conv_a_w, conv_a_b, a_log_a, dt_bias_a, norm_a_g, conv_b_w, conv_b_b, a_log_b, dt_bias_b,
         d_skip_b, norm_b_g, norm_c_g, i_bias_d, f_bias_d, norm_d_g, w_out, ln1_g, ln1_b, w_up, w_down,
         ln2_g, ln2_b) = [w[l] for w in layer_w]
        if l == 0:
            h, u = _inproj(h, w_in_p, l, emb_g.reshape(1, -1), emb_b.reshape(1, -1), True)
        else:
            (u,) = _inproj(h, w_in_p, l, emb_g.reshape(1, -1), emb_b.reshape(1, -1), False)
        params = (conv_a_w, conv_a_b.reshape(1, -1), conv_b_w, conv_b_b.reshape(1, -1),
                  _gate_row([(GL_DECAY_A, dt_bias_a), (GL_DT_B, dt_bias_b), (GL_I_D, i_bias_d), (GL_F_D, f_bias_d),
                             (GL_DT_B2, dt_bias_b)]),
                  _gate_row([(GL_DECAY_A, a_log_a), (GL_DT_B, a_log_b)]),
                  norm_a_g.reshape(1, -1), norm_b_g.reshape(1, -1), norm_c_g.reshape(1, -1),
                  norm_d_g.reshape(1, -1), jnp.repeat(d_skip_b, HEAD_DIM).reshape(1, -1), lb_logits)
        y, st = _mixers(u.reshape(bsz, t, U_COLS), states, params, layer=l, P=SEQS_PER_STEP_PROMPT)
        new_states.append(st)
        h = _outffn(y.reshape(n, D_MODEL), h, w_out.astype(BF16), ln1_g.reshape(1, -1), ln1_b.reshape(1, -1),
                    w_up.astype(BF16), w_down.astype(BF16), ln2_g.reshape(1, -1), ln2_b.reshape(1, -1))
    return h.reshape(bsz, t, D_MODEL), tuple(jnp.stack(z) for z in zip(*new_states))


def _sample_decoder(x, states, emb_g, emb_b, lb_logits, w_in_p, layer_w):
    bsz, t, _ = x.shape
    n = bsz * t
    h = jnp.transpose(x, (1, 0, 2)).reshape(n, D_MODEL)
    ca, sa, cb, sb, sc, dc, dn, dm = states
    states_t = (jnp.transpose(ca, (0, 2, 1, 3)), jnp.transpose(sa, (0, 2, 3, 4, 1)),
                jnp.transpose(cb, (0, 2, 1, 3)), jnp.transpose(sb, (0, 2, 3, 4, 1)),
                jnp.transpose(sc, (0, 2, 3, 4, 1)), jnp.transpose(dc, (0, 2, 3, 4, 1)),
                jnp.transpose(dn, (0, 2, 3, 1)), jnp.transpose(dm, (0, 2, 1)))
    new_states = []
    for l in range(DEPTH):
        (conv_a_w, conv_a_b, a_log_a, dt_bias_a, norm_a_g, conv_b_w, conv_b_b, a_log_b, dt_bias_b,
         d_skip_b, norm_b_g, norm_c_g, i_bias_d, f_bias_d, norm_d_g, w_out, ln1_g, ln1_b, w_up, w_down,
         ln2_g, ln2_b) = [w[l] for w in layer_w]
        if l == 0:
            h, u = _inproj(h, w_in_p, l, emb_g.reshape(1, -1), emb_b.reshape(1, -1), True)
        else:
            (u,) = _inproj(h, w_in_p, l, emb_g.reshape(1, -1), emb_b.reshape(1, -1), False)
        params = (conv_a_w, conv_a_b.reshape(1, -1), conv_b_w, conv_b_b.reshape(1, -1),
                  _gate_row([(GL_DECAY_A, dt_bias_a), (GL_DT_B, dt_bias_b), (GL_I_D, i_bias_d), (GL_F_D, f_bias_d),
                             (GL_DT_B2, dt_bias_b)]),
                  _gate_row([(GL_DECAY_A, a_log_a), (GL_DT_B, a_log_b)]),
                  norm_a_g.reshape(1, -1), norm_b_g.reshape(1, -1), norm_c_g.reshape(1, -1),
                  norm_d_g.reshape(1, -1), jnp.broadcast_to(d_skip_b[:, None, None], (N_HEADS, 1, LANES)),
                  lb_logits)
        y, st = _sample_mixers(u, states_t, params, layer=l, T=t)
        new_states.append(st)
        h = _outffn(y, h, w_out.astype(BF16), ln1_g.reshape(1, -1), ln1_b.reshape(1, -1),
                    w_up.astype(BF16), w_down.astype(BF16), ln2_g.reshape(1, -1), ln2_b.reshape(1, -1))
    ca_n, sa_n, cb_n, sb_n, sc_n, dc_n, dn_n, dm_n = (jnp.stack(z) for z in zip(*new_states))
    out_states = (jnp.transpose(ca_n, (0, 2, 1, 3)), jnp.transpose(sa_n, (0, 4, 1, 2, 3)),
                  jnp.transpose(cb_n, (0, 2, 1, 3)), jnp.transpose(sb_n, (0, 4, 1, 2, 3)),
                  jnp.transpose(sc_n, (0, 4, 1, 2, 3)), jnp.transpose(dc_n, (0, 4, 1, 2, 3)),
                  jnp.transpose(dn_n, (0, 3, 1, 2)), jnp.transpose(dm_n, (0, 2, 1)))
    return jnp.transpose(h.reshape(t, bsz, D_MODEL), (1, 0, 2)), out_states


def kernel(x_prompt, x_sample, state_a_conv, state_a_ssm, state_b_conv, state_b_ssm, state_c_ssm, state_d_cmem, state_d_nvec, state_d_mstab, emb_ln_g, emb_ln_b, lb_logits_c, w_in, conv_a_w, conv_a_b, a_log_a, dt_bias_a, norm_a_g, conv_b_w, conv_b_b, a_log_b, dt_bias_b, d_skip_b, norm_b_g, norm_c_g, i_bias_d, f_bias_d, norm_d_g, w_out, ln1_g, ln1_b, w_up, w_down, ln2_g, ln2_b):
    layer_w = (conv_a_w, conv_a_b, a_log_a, dt_bias_a, norm_a_g, conv_b_w, conv_b_b, a_log_b, dt_bias_b,
               d_skip_b, norm_b_g, norm_c_g, i_bias_d, f_bias_d, norm_d_g, w_out, ln1_g, ln1_b, w_up, w_down,
               ln2_g, ln2_b)
    sample_states = (state_a_conv, state_a_ssm, state_b_conv, state_b_ssm,
                     state_c_ssm, state_d_cmem, state_d_nvec, state_d_mstab)
    prompt_states = tuple(jnp.zeros((1, x_prompt.shape[0]) + s.shape[2:], F32) for s in sample_states)
    w_in_p = _prep_w_in(w_in)
    y_p, ps = _prompt_decoder(x_prompt, prompt_states, emb_ln_g, emb_ln_b, lb_logits_c, w_in_p, layer_w)
    y_s, ss = _sample_decoder(x_sample, sample_states, emb_ln_g, emb_ln_b, lb_logits_c, w_in_p, layer_w)
    return (y_p, y_s) + ps + ss
```

```python
import functools

import jax
import jax.numpy as jnp
from jax import lax
from jax.experimental import pallas as pl
from jax.experimental.pallas import tpu as pltpu

F32 = jnp.float32
BF16 = jnp.bfloat16

D_MODEL = 1024
DEPTH = 2
N_HEADS = 4
HEAD_DIM = 64
GROUP_WIDTH = N_HEADS * HEAD_DIM
G_B = 2
CONV_K = 4
CHUNK = 64
D_FF = 4 * D_MODEL
EPS = 1e-6
NEG = -1e30
ALPHA = (2 * DEPTH) ** 0.25
LOG2E = 1.4426950408889634

CONV_A_CH = 3 * GROUP_WIDTH
CONV_B_CH = GROUP_WIDTH + 2 * G_B * HEAD_DIM

OFF_A_QKV = 0
OFF_A_Z = 768
OFF_B_Z = 1024
OFF_B_XBC = 1280
OFF_C_Q = 1792
OFF_C_F = 2048
OFF_C_I = 2304
OFF_C_G = 2560
OFF_D_Q = 2816
OFF_D_K = 3072
OFF_D_V = 3328
OFF_D_O = 3584
OFF_GATES = 3840
U_COLS = 3968
LANES = 128
GL_BETA = 0
GL_DECAY_A = 4
GL_DT_B = 8
GL_I_D = 12
GL_F_D = 16
GL_DT_B2 = 20
N_GATE_COLS = 24

CONV_TAIL = CONV_K - 1
ROW0 = 8

VMEM_LIMIT = 56 * 1024 * 1024
TOKEN_TILE = 512
SUB_TILE = 256
SEQS_PER_STEP_PROMPT = 8
START_C = 2
START_D = 7
START_B = 10
SEQ_STAGGER = 0


def _layernorm(x, g, b):
    mu = jnp.mean(x, axis=-1, keepdims=True)
    xc = x - mu
    var = jnp.mean(xc * xc, axis=-1, keepdims=True)
    return xc * lax.rsqrt(var + EPS) * g + b


def _sigmoid(x):
    return 1.0 / (1.0 + jnp.exp(-x))


def _silu(x):
    return x * _sigmoid(x)


def _dot(a, b):
    return jnp.dot(a, b, preferred_element_type=F32)


def _mm(a, b):
    return _dot(a.astype(BF16), b.astype(BF16))


def _nt(a, b):
    return lax.dot_general(a.astype(BF16), b.astype(BF16), (((1,), (1,)), ((), ())),
                           preferred_element_type=F32)


def _split3(x):
    hi = x.astype(BF16)
    r1 = x - hi.astype(F32)
    mid = r1.astype(BF16)
    r2 = r1 - mid.astype(F32)
    return hi, mid, r2.astype(BF16)


def _dot01(m01, parts):
    acc = _dot(m01, parts[0])
    for p in parts[1:]:
        acc = acc + _dot(m01, p)
    return acc


def _rms(x):
    return x * lax.rsqrt(jnp.mean(x * x, axis=-1, keepdims=True) + EPS)


def _levels(L):
    out, b = [], 1
    while 2 * b <= L:
        out.append(b)
        b *= 2
    return out


def _interleaved(chains):
    pending = [ch if isinstance(ch, tuple) else (0, ch) for ch in chains]
    rnd = 0
    while pending:
        alive = []
        for start, ch in pending:
            if start > rnd:
                alive.append((start, ch))
                continue
            try:
                next(ch)
                alive.append((start, ch))
            except StopIteration:
                pass
        pending = alive
        rnd += 1
        yield


def _run_interleaved(chains):
    for _ in _interleaved(chains):
        pass


def _row_parts(n_rows):
    step = min(SUB_TILE, n_rows)
    return [slice(r, r + step) for r in range(0, n_rows, step)]


def _inproj_kernel(x_ref, w_ref, g_ref, b_ref, *out_refs, apply_ln):
    def part(rows):
        x = x_ref[rows, :]
        if apply_ln:
            x = _layernorm(x, g_ref[...], b_ref[...])
            out_refs[0][rows, :] = x
        yield
        out_refs[-1][rows, :] = _nt(x, w_ref[...])

    _run_interleaved([part(rows) for rows in _row_parts(x_ref.shape[0])])


def _inproj(x, w_in_p, layer, g, b, apply_ln):
    n = x.shape[0]
    tm = min(TOKEN_TILE, n)
    grid = (n // tm,)
    const = lambda i: (0, 0)
    out_shape = [jax.ShapeDtypeStruct((n, U_COLS), F32)]
    out_specs = [pl.BlockSpec((tm, U_COLS), lambda i: (i, 0))]
    if apply_ln:
        out_shape = [jax.ShapeDtypeStruct((n, D_MODEL), F32)] + out_shape
        out_specs = [pl.BlockSpec((tm, D_MODEL), lambda i: (i, 0))] + out_specs
    return pl.pallas_call(
        functools.partial(_inproj_kernel, apply_ln=apply_ln),
        grid=grid,
        in_specs=[pl.BlockSpec((tm, D_MODEL), lambda i: (i, 0)),
                  pl.BlockSpec((None, U_COLS, D_MODEL), lambda i: (layer, 0, 0), pipeline_mode=pl.Buffered(1)),
                  pl.BlockSpec((1, D_MODEL), const),
                  pl.BlockSpec((1, D_MODEL), const)],
        out_specs=out_specs,
        out_shape=out_shape,
        compiler_params=pltpu.CompilerParams(dimension_semantics=("arbitrary",),
                                             vmem_limit_bytes=VMEM_LIMIT),
        name="inproj_ln" if apply_ln else "inproj",
    )(x, w_in_p, g, b)


def _outffn_kernel(mix_ref, x_ref, wo_ref, g1_ref, b1_ref, wu_ref, wd_ref, g2_ref, b2_ref, o_ref):
    def part(rows):
        m = _dot(mix_ref[rows, :].astype(BF16), wo_ref[...])
        yield
        h1 = _layernorm(ALPHA * x_ref[rows, :] + m, g1_ref[...], b1_ref[...])
        up = _dot(h1.astype(BF16), wu_ref[...])
        yield
        act = jnp.square(jnp.maximum(up, 0.0))
        ff = _dot(act.astype(BF16), wd_ref[...])
        yield
        o_ref[rows, :] = _layernorm(ALPHA * h1 + ff, g2_ref[...], b2_ref[...])

    _run_interleaved([part(rows) for rows in _row_parts(mix_ref.shape[0])])


def _outffn(mix, x, wo, g1, b1, wu, wd, g2, b2):
    n = x.shape[0]
    tm = min(TOKEN_TILE, n)
    const = lambda i: (0, 0)
    tile = pl.BlockSpec((tm, D_MODEL), lambda i: (i, 0))
    row = pl.BlockSpec((1, D_MODEL), const)
    return pl.pallas_call(
        _outffn_kernel,
        grid=(n // tm,),
        in_specs=[tile, tile,
                  pl.BlockSpec((D_MODEL, D_MODEL), const, pipeline_mode=pl.Buffered(1)),
                  row, row,
                  pl.BlockSpec((D_MODEL, D_FF), const, pipeline_mode=pl.Buffered(1)),
                  pl.BlockSpec((D_FF, D_MODEL), const, pipeline_mode=pl.Buffered(1)),
                  row, row],
        out_specs=tile,
        out_shape=jax.ShapeDtypeStruct((n, D_MODEL), F32),
        compiler_params=pltpu.CompilerParams(dimension_semantics=("arbitrary",),
                                             vmem_limit_bytes=VMEM_LIMIT),
        name="outffn",
    )(mix, x, wo, g1, b1, wu, wd, g2, b2)


def _mixer_pair_kernel(u_ref, ca_in, sa_in, cb_in, sb_in, sc_in, dc_in, dn_in, dm_in,
                       caw_ref, cab_ref, cbw_ref, cbb_ref, gbias_ref, galog_ref,
                       nag_ref, nbg_ref, ncg_ref, ndg_ref, dskip_ref, lblog_ref,
                       y_ref, ca_out, sa_out, cb_out, sb_out, sc_out, dc_out, dn_out, dm_out,
                       cbuf, sa, sb, sc, dc, dn, dm, exp01_s,
                       *, NC, P, layer):
    c = pl.program_id(1)
    H, HD, L, W = N_HEADS, HEAD_DIM, CHUNK, LANES
    NP = H // 2
    R0, R1 = ROW0, ROW0 + L
    CB_A, CB_B = 0, CONV_A_CH
    f32 = F32

    def iota(shape, d):
        return lax.broadcasted_iota(jnp.int32, shape, d)

    def one_bf16(mask):
        return jnp.where(mask, 1.0, 0.0).astype(BF16)

    r_ll, c_ll = iota((L, L), 0), iota((L, L), 1)
    tril01 = one_bf16(r_ll >= c_ll)
    r_lw, c_lw = iota((L, W), 0), iota((L, W), 1)
    r_lg = iota((L, GROUP_WIDTH), 0)
    j_lw = c_lw & (HD - 1)
    incl_p = r_lw >= j_lw
    incl01 = jnp.where(incl_p, 1.0, 0.0)
    strict01 = jnp.where(r_lw > j_lw, 1.0, 0.0)
    levels = _levels(L)
    lm01 = {}
    for b in levels:
        sh = b.bit_length() - 1
        rb, cb = r_lw >> sh, j_lw >> sh
        lm01[b] = jnp.where((rb - cb == 1) & ((cb & 1) == 0), 1.0, 0.0)
    r_ww, c_ww = iota((W, W), 0), iota((W, W), 1)
    bdiag = (r_ww >> 6) == (c_ww >> 6)
    bd01 = jnp.where(bdiag, 1.0, 0.0)
    ones_bd = bd01.astype(BF16)
    ones_ww = jnp.ones((W, W), BF16)
    lane = iota((1, W), 1)
    lo_half = lane < HD
    is_dt = (lane >= GL_DECAY_A) & (lane < GL_I_D)
    is_i = (lane >= GL_I_D) & (lane < GL_F_D)
    is_f = (lane >= GL_F_D) & (lane < GL_F_D + H)
    is_cum = is_dt | is_f
    cum_scale = jnp.where(is_dt, LOG2E, 1.0)

    @pl.when((pl.program_id(0) == 0) & (c == 0))
    def _():
        exp_bases = (GL_DECAY_A, GL_BETA, GL_DT_B, GL_DT_B2, GL_F_D, GL_I_D)
        for i, (base, p) in enumerate([(base, p) for base in exp_bases for p in range(NP)]):
            exp01_s[:, i * W:(i + 1) * W] = one_bf16(r_ww == (base + 2 * p + (c_ww >> 6)))

    exp01 = exp01_s[...]

    def bd(x):
        xb = x.astype(BF16)
        return jnp.concatenate([xb, xb], axis=0) * ones_bd

    def half_sums(x):
        return _dot(x.astype(BF16), ones_bd)

    def transpose_bf16(x):
        return x.T.astype(BF16)

    pl_sm = lblog_ref[...]
    pl_sm = jnp.exp(pl_sm - jnp.max(pl_sm, axis=0, keepdims=True))
    pl_sm = pl_sm / jnp.sum(pl_sm, axis=0, keepdims=True)
    lb = pl_sm[0:1, :]
    for i in range(1, layer + 1):
        lb = lb + pl_sm[i:i + 1, :]
    lb = lb - pl_sm[0:1, :]

    @pl.when(c == 0)
    def _():
        cbuf[:, R0 - CONV_TAIL:R0, CB_A:CB_A + CONV_A_CH] = ca_in[...]
        cbuf[:, R0 - CONV_TAIL:R0, CB_B:CB_B + CONV_B_CH] = cb_in[...]
        zero = jnp.zeros((W, W), f32)
        for s in range(P):
            m_row = dm_in[s]
            for p in range(NP):
                h0, h1 = 2 * p, 2 * p + 1
                for ref, src in ((sa, sa_in), (sc, sc_in), (dc, dc_in)):
                    ref[s, p] = zero
                    ref[s, p, 0:HD, 0:HD] = src[s, h0]
                    ref[s, p, HD:W, HD:W] = src[s, h1]
                sb[s, p] = zero
                sb[s, p, p * HD:(p + 1) * HD, 0:HD] = sb_in[s, h0]
                sb[s, p, p * HD:(p + 1) * HD, HD:W] = sb_in[s, h1]
                dn[s, p] = jnp.concatenate([dn_in[s, h0:h0 + 1, :], dn_in[s, h1:h1 + 1, :]], axis=1)
                dm[s, p] = jnp.where(lo_half, m_row[:, h0:h0 + 1], m_row[:, h1:h1 + 1])

    cbuf[:, R0:R1, CB_A:CB_A + CONV_A_CH] = u_ref[:, :, OFF_A_QKV:OFF_A_QKV + CONV_A_CH]
    cbuf[:, R0:R1, CB_B:CB_B + CONV_B_CH] = u_ref[:, :, OFF_B_XBC:OFF_B_XBC + CONV_B_CH]

    def seq_chains(s):
        def seg(off, width):
            return u_ref[s, :, off:off + width]

        def conv(off, width, w_ref, b_ref):
            acc = b_ref[...] + cbuf[s, R0:R1, off:off + width] * w_ref[CONV_K - 1:CONV_K, :]
            for j in range(CONV_K - 1):
                lo = R0 - CONV_TAIL + j
                acc = acc + cbuf[s, lo:lo + L, off:off + width] * w_ref[j:j + 1, :]
            return acc

        pre = seg(OFF_GATES, W) + gbias_ref[...]
        e = jnp.exp(-jnp.abs(pre))
        l1p = jnp.log1p(e)
        softplus = jnp.maximum(pre, 0.0) + l1p
        logsig = jnp.minimum(pre, 0.0) - l1p
        logdec = jnp.where(is_dt, -jnp.exp(galog_ref[...]) * softplus, jnp.where(is_f, logsig, 0.0))
        gval = jnp.where(lane < GL_DECAY_A, _sigmoid(pre), jnp.where(is_i, pre, softplus))
        gcum = _dot01(tril01, _split3(logdec))
        fgate = lb + (1.0 - lb) * _sigmoid(seg(OFF_C_F, GROUP_WIDTH))
        gc = _dot01(tril01, _split3(jnp.log(fgate))) * LOG2E
        qkv = _silu(conv(CB_A, CONV_A_CH, caw_ref, cab_ref))
        qk_raw = qkv[:, 0:2 * GROUP_WIDTH]
        ss_qk = jnp.concatenate([half_sums(qk_raw[:, i * W:(i + 1) * W] * qk_raw[:, i * W:(i + 1) * W])
                                 for i in range(2 * NP)], axis=1)
        yield
        gsrc = jnp.where(is_cum, gcum * cum_scale, gval)
        src = _split3(gsrc)
        expd = _dot(src[0], exp01) + _dot(src[1], exp01) + _dot(src[2], exp01)
        gsrc_t = gsrc.T

        def expanded(k, p):
            o = (k * NP + p) * W
            return expd[:, o:o + W]

        def row_pair(base, p):
            r = base + 2 * p
            return jnp.concatenate([gsrc_t[r:r + 1, :], gsrc_t[r + 1:r + 2, :]], axis=1)

        def chain_a(p):
            ps = slice(p * W, (p + 1) * W)
            q = qkv[:, p * W:(p + 1) * W]
            k = qkv[:, GROUP_WIDTH + p * W:GROUP_WIDTH + (p + 1) * W]
            v = qkv[:, 2 * GROUP_WIDTH + p * W:2 * GROUP_WIDTH + (p + 1) * W]
            ssq = ss_qk[:, p * W:(p + 1) * W]
            ssk = ss_qk[:, GROUP_WIDTH + p * W:GROUP_WIDTH + (p + 1) * W]
            g = expanded(0, p)
            beta = expanded(1, p)
            grow = row_pair(GL_DECAY_A, p)
            q = q * lax.rsqrt(ssq + EPS) * (HD ** -0.5)
            k = k * lax.rsqrt(ssk + EPS)
            eg = jnp.exp2(g)
            g_last = g[L - 1:L, :]
            kbd = bd(k)
            kk = _nt(k, kbd)
            qk = _nt(q, kbd)
            kw_t = transpose_bf16(k * jnp.exp2(g_last - g))
            dmat = jnp.exp2(jnp.minimum(g - grow, 0.0)) * incl01
            yield
            nmat = (beta * kk) * (dmat * strict01)
            low = -(nmat * lm01[1])
            for b in levels[1:]:
                off = nmat * lm01[b]
                x = off + _dot(off.astype(BF16), bd(low))
                yield
                low = low - x - _dot(low.astype(BF16), bd(x))
                yield
            rv = beta * v
            rk = (beta * eg) * k
            rhs_bd = jnp.concatenate([bd(rv), bd(rk)], axis=1)
            sol = jnp.concatenate([rv, rk], axis=1) + _dot(low.astype(BF16), rhs_bd)
            s0 = sa[s, p]
            qs = _mm(q, s0)
            yield
            unew = sol[:, :W] - _mm(sol[:, W:], s0)
            yield
            o = eg * qs + _mm(qk * dmat, bd(unew))
            sa[s, p] = jnp.exp2(g_last) * s0 + _mm(kw_t, unew) * bd01
            ms = half_sums(o * o) * (1.0 / HD)
            yield
            a_z = seg(OFF_A_Z + p * W, W)
            y_ref[s, :, ps] = o * lax.rsqrt(ms + EPS) * nag_ref[:, ps] * _silu(a_z)

        xbc = _silu(conv(CB_B, CONV_B_CH, cbw_ref, cbb_ref))
        b_all = xbc[:, GROUP_WIDTH:GROUP_WIDTH + W]
        c_all = xbc[:, GROUP_WIDTH + W:GROUP_WIDTH + 2 * W]
        b_t = transpose_bf16(b_all)

        def chain_b(p):
            ps = slice(p * W, (p + 1) * W)
            b_grp = b_all * jnp.where((lane >> 6) == p, 1.0, 0.0)
            cb = _nt(c_all, jnp.concatenate([b_grp, b_grp], axis=0))
            g = expanded(2, p)
            dt = expanded(3, p)
            grow = row_pair(GL_DT_B, p)
            yield
            g_last = g[L - 1:L, :]
            xh = xbc[:, ps]
            v = xh * dt
            s0 = sb[s, p]
            dmat = jnp.exp2(jnp.minimum(g - grow, 0.0)) * incl01
            o = jnp.exp2(g) * _mm(c_all, s0) + _mm(cb * dmat, bd(v))
            upd = _dot(b_t, (v * jnp.exp2(g_last - g)).astype(BF16))
            sb[s, p] = jnp.exp2(g_last) * s0 + upd * jnp.where((r_ww >> 6) == p, 1.0, 0.0)
            yield
            ob = (o + dskip_ref[:, ps] * xh) * _silu(seg(OFF_B_Z + p * W, W))
            ms = _dot((ob * ob).astype(BF16), ones_ww) * (1.0 / W)
            yield
            y_ref[s, :, GROUP_WIDTH + p * W:GROUP_WIDTH + (p + 1) * W] = ob * lax.rsqrt(ms + EPS) * nbg_ref[:, ps]

        kc = 1.0 - fgate
        pref = {}
        for b in levels:
            blk = 2 * b
            if blk >= 8:
                pref[b] = jnp.concatenate(
                    [jnp.broadcast_to(gc[i * blk + b - 1:i * blk + b, :], (blk, GROUP_WIDTH)) for i in range(L // blk)],
                    axis=0)
            else:
                acc = gc
                for d in range(blk):
                    sh = d - (b - 1)
                    if sh != 0:
                        acc = jnp.where((r_lg & (blk - 1)) == d, pltpu.roll(gc, sh % L, axis=0), acc)
                pref[b] = acc

        def chain_c(p):
            ps = slice(p * W, (p + 1) * W)
            q = seg(OFF_C_Q + p * W, W)
            v = seg(OFF_C_I + p * W, W)
            k, g = kc[:, ps], gc[:, ps]
            g_last = g[L - 1:L, :]
            kw_t = transpose_bf16(k * jnp.exp2(g_last - g))
            e_col = jnp.exp2(jnp.broadcast_to(g_last, (8, W)).T[:, 0:1])
            qk_diag = half_sums(q * k)
            yield
            att = None
            for b in levels:
                pb = pref[b][:, ps]
                kd = bd(k * jnp.exp2(jnp.minimum(pb - g, 0.0)))
                part = _nt(q * jnp.exp2(jnp.minimum(g - pb, 0.0)), kd) * lm01[b]
                att = part if att is None else att + part
                yield
            s0 = sc[s, p]
            o = _mm(q * jnp.exp2(g), s0) + _mm(att, bd(v)) + qk_diag * v
            sc[s, p] = e_col * s0 + _mm(kw_t, v) * bd01
            ms = half_sums(o * o) * (1.0 / HD)
            yield
            c_g = seg(OFF_C_G + p * W, W)
            y_ref[s, :, 2 * GROUP_WIDTH + p * W:2 * GROUP_WIDTH + (p + 1) * W] = (
                o * lax.rsqrt(ms + EPS) * ncg_ref[:, ps] * _sigmoid(c_g))

        def chain_d(p):
            ps = slice(p * W, (p + 1) * W)
            q = seg(OFF_D_Q + p * W, W)
            k = seg(OFF_D_K + p * W, W) * (HD ** -0.5)
            v = seg(OFF_D_V + p * W, W)
            bcum = expanded(4, p)
            ipre = expanded(5, p)
            m0 = dm[s, p]
            a = ipre - bcum
            arow = row_pair(GL_I_D, p) - row_pair(GL_F_D, p)
            qk = _nt(q, bd(k).astype(BF16))
            cmem = dc[s, p]
            qc = _mm(q, cmem)
            nvec = dn[s, p]
            qn = half_sums(q * nvec)
            cm = a
            sh = 1
            while sh < L:
                cm = jnp.maximum(cm, jnp.where(r_lw >= sh, pltpu.roll(cm, sh, axis=0), NEG))
                sh *= 2
            yield
            m_r = bcum + jnp.maximum(m0, cm)
            pmat = jnp.where(incl_p, jnp.exp(bcum + arow - m_r), 0.0) * qk
            s_init = jnp.exp(bcum + m0 - m_r)
            num = s_init * qc + _mm(pmat, bd(v))
            den = s_init * qn + half_sums(pmat)
            m_last = m_r[L - 1:L, :]
            b_last = bcum[L - 1:L, :]
            scale = jnp.exp(b_last + m0 - m_last)
            kw = k * jnp.exp(b_last - bcum + ipre - m_last)
            kw_t = transpose_bf16(kw)
            yield
            dc[s, p] = scale * cmem + _mm(kw_t, v) * bd01
            dn[s, p] = scale * nvec + jnp.sum(kw, axis=0, keepdims=True)
            dm[s, p] = m_last
            hh = num / jnp.maximum(jnp.abs(den), jnp.exp(-m_r))
            ms = half_sums(hh * hh) * (1.0 / HD)
            yield
            d_o = seg(OFF_D_O + p * W, W)
            y_ref[s, :, 3 * GROUP_WIDTH + p * W:3 * GROUP_WIDTH + (p + 1) * W] = (
                hh * lax.rsqrt(ms + EPS) * ndg_ref[:, ps] * _sigmoid(d_o))

        return ([chain_a(p) for p in range(NP)] + [(START_C, chain_c(p)) for p in range(NP)]
                + [(START_D, chain_d(p)) for p in range(NP)] + [(START_B, chain_b(p)) for p in range(NP)])

    def seq_driver(s):
        chains = yield from seq_chains(s)
        yield
        yield from _interleaved(chains)

    _run_interleaved([(s * SEQ_STAGGER, seq_driver(s)) for s in range(P)])

    tail_a = cbuf[:, R1 - CONV_TAIL:R1, CB_A:CB_A + CONV_A_CH]
    tail_b = cbuf[:, R1 - CONV_TAIL:R1, CB_B:CB_B + CONV_B_CH]
    cbuf[:, R0 - CONV_TAIL:R0, CB_A:CB_A + CONV_A_CH] = tail_a
    cbuf[:, R0 - CONV_TAIL:R0, CB_B:CB_B + CONV_B_CH] = tail_b

    @pl.when(c == NC - 1)
    def _():
        ca_out[...] = tail_a
        cb_out[...] = tail_b
        for s in range(P):
            m_row = jnp.zeros((1, W), f32)
            for p in range(NP):
                h0, h1 = 2 * p, 2 * p + 1
                for ref, dst in ((sa, sa_out), (sc, sc_out), (dc, dc_out)):
                    dst[s, h0] = ref[s, p, 0:HD, 0:HD]
                    dst[s, h1] = ref[s, p, HD:W, HD:W]
                sb_out[s, h0] = sb[s, p, p * HD:(p + 1) * HD, 0:HD]
                sb_out[s, h1] = sb[s, p, p * HD:(p + 1) * HD, HD:W]
                nrow = dn[s, p]
                dn_out[s, h0:h0 + 1, :] = nrow[:, 0:HD]
                dn_out[s, h1:h1 + 1, :] = nrow[:, HD:W]
                mp = dm[s, p]
                m_row = jnp.where(lane == h0, mp[:, 0:1], jnp.where(lane == h1, mp[:, HD:HD + 1], m_row))
            dm_out[s] = m_row


def _mixers(u, states, params, *, layer, P):
    bsz, ttot, _ = u.shape
    T = L = CHUNK
    nc = ttot // T
    H, HD = N_HEADS, HEAD_DIM
    ca, s_a, cb, s_b, s_c, d_c, d_n, d_m = states
    d_m = jnp.pad(d_m, ((0, 0), (0, 0), (0, LANES - H))).reshape(d_m.shape[0], bsz, 1, LANES)

    def per_seq(shape):
        nd = len(shape)
        return pl.BlockSpec((P,) + tuple(shape), lambda b, c: (b,) + (0,) * nd)

    def per_seq_in(x, shape):
        nd = len(shape)
        lidx = layer if x.shape[0] > 1 else 0
        return pl.BlockSpec((None, P) + tuple(shape), lambda b, c: (lidx, b) + (0,) * nd)

    def whole(x):
        nd = x.ndim
        return pl.BlockSpec(x.shape, lambda b, c: (0,) * nd)

    state_dims = [(CONV_TAIL, CONV_A_CH), (H, HD, HD), (CONV_TAIL, CONV_B_CH), (H, HD, HD), (H, HD, HD),
                  (H, HD, HD), (H, HD), (1, LANES)]
    state_specs = [per_seq(d) for d in state_dims]
    state_shapes = [jax.ShapeDtypeStruct((bsz,) + d, F32) for d in state_dims]
    pair_state = (P, H // 2, LANES, LANES)
    scratch = ([pltpu.VMEM((P, ROW0 + L, CONV_A_CH + CONV_B_CH), F32)]
               + [pltpu.VMEM(pair_state, F32) for _ in range(4)]
               + [pltpu.VMEM((P, H // 2, 1, LANES), F32) for _ in range(2)]
               + [pltpu.VMEM((LANES, 6 * (H // 2) * LANES), BF16)])
    outs = pl.pallas_call(
        functools.partial(_mixer_pair_kernel, NC=nc, P=P, layer=layer),
        grid=(bsz // P, nc),
        in_specs=[pl.BlockSpec((P, T, U_COLS), lambda b, c: (b, c, 0))]
                 + [per_seq_in(x, d) for x, d in zip((ca, s_a, cb, s_b, s_c, d_c, d_n, d_m), state_dims)]
                 + [whole(p) for p in params],
        out_specs=[pl.BlockSpec((P, T, D_MODEL), lambda b, c: (b, c, 0))] + state_specs,
        out_shape=[jax.ShapeDtypeStruct((bsz, ttot, D_MODEL), F32)] + state_shapes,
        scratch_shapes=scratch,
        compiler_params=pltpu.CompilerParams(dimension_semantics=("arbitrary", "arbitrary"),
                                             vmem_limit_bytes=VMEM_LIMIT),
        name="mixers_l%d" % layer,
    )(u, ca, s_a, cb, s_b, s_c, d_c, d_n, d_m, *params)
    y, new = outs[0], list(outs[1:])
    new[7] = new[7].reshape(bsz, LANES)[:, :H]
    return y, tuple(new)


DEC_GATE_ROWS = 128


def _sample_pre_kernel(u_ref, ca_ref, cb_ref, caw_ref, cab_ref, cbw_ref, cbb_ref, gbias_ref, galog_ref, lblog_ref,
                       at_ref, bt_ref, ct_ref, dt_ref, gt_ref, ca_out, cb_out, hist, *, layer, T):
    t = pl.program_id(0)
    na = CONV_A_CH

    @pl.when(t == 0)
    def _():
        hist[0:CONV_TAIL, :, 0:na] = ca_ref[...]
        hist[0:CONV_TAIL, :, na:] = cb_ref[...]

    hist[CONV_TAIL + t, :, 0:na] = u_ref[:, OFF_A_QKV:OFF_A_QKV + CONV_A_CH]
    hist[CONV_TAIL + t, :, na:] = u_ref[:, OFF_B_XBC:OFF_B_XBC + CONV_B_CH]
    w_all = jnp.concatenate([caw_ref[...], cbw_ref[...]], axis=1)
    acc = jnp.concatenate([cab_ref[...], cbb_ref[...]], axis=1)
    for j in range(CONV_K):
        acc = acc + hist[t + j] * w_all[j:j + 1, :]
    act = _silu(acc)
    at_ref[...] = act[:, 0:na].T
    bt_ref[...] = act[:, na:].T

    lane = lax.broadcasted_iota(jnp.int32, (1, LANES), 1)
    pre = u_ref[:, OFF_GATES:OFF_GATES + LANES] + gbias_ref[...]
    e = jnp.exp(-jnp.abs(pre))
    l1p = jnp.log1p(e)
    softplus = jnp.maximum(pre, 0.0) + l1p
    logsig = jnp.minimum(pre, 0.0) - l1p
    is_dt = (lane >= GL_DECAY_A) & (lane < GL_I_D)
    is_i = (lane >= GL_I_D) & (lane < GL_F_D)
    is_f = (lane >= GL_F_D) & (lane < GL_F_D + N_HEADS)
    decay = jnp.exp(-jnp.exp(galog_ref[...]) * softplus)
    gates = jnp.where(lane < GL_DECAY_A, _sigmoid(pre),
                      jnp.where(is_dt, decay, jnp.where(is_i, pre, jnp.where(is_f, logsig, softplus))))
    gt_ref[...] = gates.T

    pl_sm = lblog_ref[...]
    pl_sm = jnp.exp(pl_sm - jnp.max(pl_sm, axis=0, keepdims=True))
    pl_sm = pl_sm / jnp.sum(pl_sm, axis=0, keepdims=True)
    lb = pl_sm[0:1, :]
    for i in range(1, layer + 1):
        lb = lb + pl_sm[i:i + 1, :]
    lb = lb - pl_sm[0:1, :]
    fgate = lb + (1.0 - lb) * _sigmoid(u_ref[:, OFF_C_F:OFF_C_F + GROUP_WIDTH])
    gw = GROUP_WIDTH
    ct_ref[0:gw, :] = u_ref[:, OFF_C_Q:OFF_C_Q + gw].T
    ct_ref[gw:2 * gw, :] = fgate.T
    ct_ref[2 * gw:3 * gw, :] = u_ref[:, OFF_C_I:OFF_C_I + gw].T
    dt_ref[0:gw, :] = u_ref[:, OFF_D_Q:OFF_D_Q + gw].T
    dt_ref[gw:2 * gw, :] = (u_ref[:, OFF_D_K:OFF_D_K + gw] * (HEAD_DIM ** -0.5)).T
    dt_ref[2 * gw:3 * gw, :] = u_ref[:, OFF_D_V:OFF_D_V + gw].T

    @pl.when(t == T - 1)
    def _():
        ca_out[...] = hist[T:T + CONV_TAIL, :, 0:na]
        cb_out[...] = hist[T:T + CONV_TAIL, :, na:]


def _colsum(x):
    return jnp.sum(x, axis=0, keepdims=True)


def _dec_a_kernel(q_ref, k_ref, v_ref, a_ref, beta_ref, s_in, s_out, o_ref, *, T):
    HD, B = HEAD_DIM, LANES
    for t in range(T):
        ln = slice(t * B, (t + 1) * B)
        q, k, v = q_ref[:, ln], k_ref[:, ln], v_ref[:, ln]
        q = q * (lax.rsqrt(_colsum(q * q) + EPS) * (HD ** -0.5))
        k = k * lax.rsqrt(_colsum(k * k) + EPS)
        a, beta = a_ref[:, ln], beta_ref[:, ln]
        src = s_in if t == 0 else s_out
        ks = [jnp.zeros((HD, B), F32), jnp.zeros((HD, B), F32)]
        for dk in range(HD):
            ks[dk & 1] = ks[dk & 1] + k[dk:dk + 1, :] * src[dk]
        unew = beta * (v - a * (ks[0] + ks[1]))
        o = [jnp.zeros((HD, B), F32), jnp.zeros((HD, B), F32)]
        for dk in range(HD):
            sn = a * src[dk] + k[dk:dk + 1, :] * unew
            s_out[dk] = sn
            o[dk & 1] = o[dk & 1] + q[dk:dk + 1, :] * sn
        ot = o[0] + o[1]
        o_ref[:, ln] = ot * lax.rsqrt(_colsum(ot * ot) * (1.0 / HD) + EPS)


def _dec_b_kernel(x_ref, b_ref, c_ref, a_ref, dt_ref, skip_ref, s_in, s_out, o_ref, *, T):
    HD, B = HEAD_DIM, LANES
    for t in range(T):
        ln = slice(t * B, (t + 1) * B)
        x, bm, cm = x_ref[:, ln], b_ref[:, ln], c_ref[:, ln]
        a = a_ref[:, ln]
        xdt = x * dt_ref[:, ln]
        src = s_in if t == 0 else s_out
        o = [jnp.zeros((HD, B), F32), jnp.zeros((HD, B), F32)]
        for n in range(HD):
            sn = a * src[n] + bm[n:n + 1, :] * xdt
            s_out[n] = sn
            o[n & 1] = o[n & 1] + cm[n:n + 1, :] * sn
        o_ref[:, ln] = o[0] + o[1] + skip_ref[...] * x


def _dec_c_kernel(q_ref, f_ref, v_ref, s_in, s_out, o_ref, *, T):
    HD, B = HEAD_DIM, LANES
    for t in range(T):
        ln = slice(t * B, (t + 1) * B)
        q, f, v = q_ref[:, ln], f_ref[:, ln], v_ref[:, ln]
        k = 1.0 - f
        src = s_in if t == 0 else s_out
        o = [jnp.zeros((HD, B), F32), jnp.zeros((HD, B), F32)]
        for dk in range(HD):
            sn = f[dk:dk + 1, :] * src[dk] + k[dk:dk + 1, :] * v
            s_out[dk] = sn
            o[dk & 1] = o[dk & 1] + q[dk:dk + 1, :] * sn
        ot = o[0] + o[1]
        o_ref[:, ln] = ot * lax.rsqrt(_colsum(ot * ot) * (1.0 / HD) + EPS)


def _dec_d_kernel(q_ref, k_ref, v_ref, i_ref, f_ref, c_in, n_in, m_in, c_out, n_out, m_out, o_ref, *, T):
    HD, B = HEAD_DIM, LANES
    m = m_in[...]
    nvec = n_in[...]
    for t in range(T):
        ln = slice(t * B, (t + 1) * B)
        q, k, v = q_ref[:, ln], k_ref[:, ln], v_ref[:, ln]
        ipre, logf = i_ref[:, ln], f_ref[:, ln]
        m_new = jnp.maximum(logf + m, ipre)
        fs = jnp.exp(logf + m - m_new)
        kw = k * jnp.exp(ipre - m_new)
        src = c_in if t == 0 else c_out
        num = [jnp.zeros((HD, B), F32), jnp.zeros((HD, B), F32)]
        for dk in range(HD):
            cn = fs * src[dk] + kw[dk:dk + 1, :] * v
            c_out[dk] = cn
            num[dk & 1] = num[dk & 1] + q[dk:dk + 1, :] * cn
        nvec = fs * nvec + kw
        den = _colsum(q * nvec)
        hh = (num[0] + num[1]) / jnp.maximum(jnp.abs(den), jnp.exp(-m_new))
        o_ref[:, ln] = hh * lax.rsqrt(_colsum(hh * hh) * (1.0 / HD) + EPS)
        m = m_new
    n_out[...] = nvec
    m_out[...] = m


def _sample_post_kernel(oa_ref, ob_ref, oc_ref, od_ref, u_ref, nag_ref, nbg_ref, ncg_ref, ndg_ref, y_ref):
    gw = GROUP_WIDTH
    y_ref[:, 0:gw] = oa_ref[...].T * nag_ref[...] * _silu(u_ref[:, OFF_A_Z:OFF_A_Z + gw])
    ob = ob_ref[...].T * _silu(u_ref[:, OFF_B_Z:OFF_B_Z + gw])
    grp = gw // G_B
    for g in range(G_B):
        sl = slice(g * grp, (g + 1) * grp)
        y_ref[:, gw + g * grp:gw + (g + 1) * grp] = _rms(ob[:, sl]) * nbg_ref[:, sl]
    y_ref[:, 2 * gw:3 * gw] = oc_ref[...].T * ncg_ref[...] * _sigmoid(u_ref[:, OFF_C_G:OFF_C_G + gw])
    y_ref[:, 3 * gw:4 * gw] = od_ref[...].T * ndg_ref[...] * _sigmoid(u_ref[:, OFF_D_O:OFF_D_O + gw])


def _sample_mixers(u, states_t, params, *, layer, T):
    B, H, HD, GW = LANES, N_HEADS, HEAD_DIM, GROUP_WIDTH
    ca_t, sa_t, cb_t, sb_t, sc_t, dc_t, dn_t, dm_t = states_t
    (caw, cab, cbw, cbb, gbias, galog, nag, nbg, ncg, ndg, dskip, lblog) = params
    cp1 = pltpu.CompilerParams(dimension_semantics=("arbitrary",), vmem_limit_bytes=VMEM_LIMIT)
    n = T * B

    def whole(x):
        nd = x.ndim
        return pl.BlockSpec(x.shape, lambda i: (0,) * nd)

    def lane_blk(rows):
        return pl.BlockSpec((rows, B), lambda t: (0, t))

    a_t, b_t, c_t, d_t, g_t, ca_new, cb_new = pl.pallas_call(
        functools.partial(_sample_pre_kernel, layer=layer, T=T),
        grid=(T,),
        in_specs=[pl.BlockSpec((B, U_COLS), lambda t: (t, 0)),
                  pl.BlockSpec((None, CONV_TAIL, B, CONV_A_CH), lambda t: (layer, 0, 0, 0)),
                  pl.BlockSpec((None, CONV_TAIL, B, CONV_B_CH), lambda t: (layer, 0, 0, 0))]
                 + [whole(p) for p in (caw, cab, cbw, cbb, gbias, galog, lblog)],
        out_specs=[lane_blk(CONV_A_CH), lane_blk(CONV_B_CH), lane_blk(3 * GW), lane_blk(3 * GW),
                   lane_blk(DEC_GATE_ROWS),
                   pl.BlockSpec((CONV_TAIL, B, CONV_A_CH), lambda t: (0, 0, 0)),
                   pl.BlockSpec((CONV_TAIL, B, CONV_B_CH), lambda t: (0, 0, 0))],
        out_shape=[jax.ShapeDtypeStruct((CONV_A_CH, n), F32), jax.ShapeDtypeStruct((CONV_B_CH, n), F32),
                   jax.ShapeDtypeStruct((3 * GW, n), F32), jax.ShapeDtypeStruct((3 * GW, n), F32),
                   jax.ShapeDtypeStruct((DEC_GATE_ROWS, n), F32),
                   jax.ShapeDtypeStruct((CONV_TAIL, B, CONV_A_CH), F32),
                   jax.ShapeDtypeStruct((CONV_TAIL, B, CONV_B_CH), F32)],
        scratch_shapes=[pltpu.VMEM((CONV_TAIL + T, B, CONV_A_CH + CONV_B_CH), F32)],
        compiler_params=cp1, name="sample_pre_l%d" % layer,
    )(u, ca_t, cb_t, caw, cab, cbw, cbb, gbias, galog, lblog)

    g3 = g_t.reshape(DEC_GATE_ROWS, 1, n)

    def head_rows(sec):
        return pl.BlockSpec((HD, n), lambda h: (sec * H + h, 0))

    def gate_row(base):
        return pl.BlockSpec((None, 1, n), lambda h: (base + h, 0, 0))

    def state_in(x):
        nd = x.ndim - 2
        return pl.BlockSpec((None, None) + x.shape[2:], lambda h: (layer, h) + (0,) * nd)

    def state_out(shape):
        nd = len(shape)
        return pl.BlockSpec((None,) + tuple(shape), lambda h: (h,) + (0,) * nd)

    st_shape = (HD, HD, B)
    o_spec = pl.BlockSpec((HD, n), lambda h: (h, 0))
    o_shape = jax.ShapeDtypeStruct((GW, n), F32)
    st_out = jax.ShapeDtypeStruct((H,) + st_shape, F32)

    sa_new, oa = pl.pallas_call(
        functools.partial(_dec_a_kernel, T=T), grid=(H,),
        in_specs=[head_rows(0), head_rows(1), head_rows(2), gate_row(GL_DECAY_A), gate_row(GL_BETA), state_in(sa_t)],
        out_specs=[state_out(st_shape), o_spec], out_shape=[st_out, o_shape],
        compiler_params=cp1, name="dec_a_l%d" % layer,
    )(a_t, a_t, a_t, g3, g3, sa_t)

    hpg = H // G_B
    sb_new, ob = pl.pallas_call(
        functools.partial(_dec_b_kernel, T=T), grid=(H,),
        in_specs=[head_rows(0),
                  pl.BlockSpec((HD, n), lambda h: (H + h // hpg, 0)),
                  pl.BlockSpec((HD, n), lambda h: (H + G_B + h // hpg, 0)),
                  gate_row(GL_DT_B), gate_row(GL_DT_B2),
                  pl.BlockSpec((None, 1, B), lambda h: (h, 0, 0)), state_in(sb_t)],
        out_specs=[state_out(st_shape), o_spec], out_shape=[st_out, o_shape],
        compiler_params=cp1, name="dec_b_l%d" % layer,
    )(b_t, b_t, b_t, g3, g3, dskip, sb_t)

    sc_new, oc = pl.pallas_call(
        functools.partial(_dec_c_kernel, T=T), grid=(H,),
        in_specs=[head_rows(0), head_rows(1), head_rows(2), state_in(sc_t)],
        out_specs=[state_out(st_shape), o_spec], out_shape=[st_out, o_shape],
        compiler_params=cp1, name="dec_c_l%d" % layer,
    )(c_t, c_t, c_t, sc_t)

    dm3 = dm_t.reshape(dm_t.shape[0], H, 1, B)
    dc_new, dn_new, dm_new, od = pl.pallas_call(
        functools.partial(_dec_d_kernel, T=T), grid=(H,),
        in_specs=[head_rows(0), head_rows(1), head_rows(2), gate_row(GL_I_D), gate_row(GL_F_D),
                  state_in(dc_t), state_in(dn_t), state_in(dm3)],
        out_specs=[state_out(st_shape), state_out((HD, B)), state_out((1, B)), o_spec],
        out_shape=[st_out, jax.ShapeDtypeStruct((H, HD, B), F32), jax.ShapeDtypeStruct((H, 1, B), F32), o_shape],
        compiler_params=cp1, name="dec_d_l%d" % layer,
    )(d_t, d_t, d_t, g3, g3, dc_t, dn_t, dm3)

    y = pl.pallas_call(
        _sample_post_kernel, grid=(T,),
        in_specs=[lane_blk(GW)] * 4 + [pl.BlockSpec((B, U_COLS), lambda t: (t, 0))]
                 + [whole(p) for p in (nag, nbg, ncg, ndg)],
        out_specs=pl.BlockSpec((B, D_MODEL), lambda t: (t, 0)),
        out_shape=jax.ShapeDtypeStruct((n, D_MODEL), F32),
        compiler_params=cp1, name="sample_post_l%d" % layer,
    )(oa, ob, oc, od, u, nag, nbg, ncg, ndg)
    return y, (ca_new, sa_new, cb_new, sb_new, sc_new, dc_new, dn_new, dm_new.reshape(H, B))


def _prep_w_in_kernel(w_ref, o_ref):
    for l in range(w_ref.shape[1]):
        o_ref[l, 0:1024, :] = w_ref[0:1024, l, :].astype(BF16)
        o_ref[l, 1024:1792, :] = w_ref[1032:1800, l, :].astype(BF16)
        o_ref[l, 1792:OFF_GATES, :] = w_ref[1804:3852, l, :].astype(BF16)
        gates = jnp.concatenate([w_ref[1024:1032, l, :], w_ref[1800:1804, l, :], w_ref[3852:3860, l, :],
                                 w_ref[1800:1804, l, :],
                                 jnp.zeros((U_COLS - OFF_GATES - N_GATE_COLS, w_ref.shape[2]), F32)], axis=0)
        o_ref[l, OFF_GATES:U_COLS, :] = gates.astype(BF16)


def _prep_w_in(w_in):
    depth, d, cols = w_in.shape
    blk = 256
    return pl.pallas_call(
        _prep_w_in_kernel,
        grid=(d // blk,),
        in_specs=[pl.BlockSpec((cols, depth, blk), lambda i: (0, 0, i))],
        out_specs=pl.BlockSpec((depth, U_COLS, blk), lambda i: (0, 0, i)),
        out_shape=jax.ShapeDtypeStruct((depth, U_COLS, d), BF16),
        compiler_params=pltpu.CompilerParams(dimension_semantics=("arbitrary",),
                                             vmem_limit_bytes=VMEM_LIMIT),
        name="prep_w_in",
    )(jnp.transpose(w_in, (2, 0, 1)))


def _gate_row(parts):
    row = jnp.zeros((LANES,), F32)
    for off, val in parts:
        row = lax.dynamic_update_slice(row, val.astype(F32), (off,))
    return row.reshape(1, LANES)


def _prompt_decoder(x, states, emb_g, emb_b, lb_logits, w_in_p, layer_w):
    bsz, t, _ = x.shape
    n = bsz * t
    h = x.reshape(n, D_MODEL)
    new_states = []
    for l in range(DEPTH):
        (conv_a_w, conv_a_b, a_log_a, dt_bias_a, norm_a_g, conv_b_w, conv_b_b, a_log_b, dt_bias_b,
         d_skip_b, norm_b_g, norm_c_g, i_bias_d, f_bias_d, norm_d_g, w_out, ln1_g, ln1_b, w_up, w_down,
         ln2_g, ln2_b) = [w[l] for w in layer_w]
        if l == 0:
            h, u = _inproj(h, w_in_p, l, emb_g.reshape(1, -1), emb_b.reshape(1, -1), True)
        else:
            (u,) = _inproj(h, w_in_p, l, emb_g.reshape(1, -1), emb_b.reshape(1, -1), False)
        params = (conv_a_w, conv_a_b.reshape(1, -1), conv_b_w, conv_b_b.reshape(1, -1),
                  _gate_row([(GL_DECAY_A, dt_bias_a), (GL_DT_B, dt_bias_b), (GL_I_D, i_bias_d), (GL_F_D, f_bias_d),
                             (GL_DT_B2, dt_bias_b)]),
                  _gate_row([(GL_DECAY_A, a_log_a), (GL_DT_B, a_log_b)]),
                  norm_a_g.reshape(1, -1), norm_b_g.reshape(1, -1), norm_c_g.reshape(1, -1),
                  norm_d_g.reshape(1, -1), jnp.repeat(d_skip_b, HEAD_DIM).reshape(1, -1), lb_logits)
        y, st = _mixers(u.reshape(bsz, t, U_COLS), states, params, layer=l, P=SEQS_PER_STEP_PROMPT)
        new_states.append(st)
        h = _outffn(y.reshape(n, D_MODEL), h, w_out.astype(BF16), ln1_g.reshape(1, -1), ln1_b.reshape(1, -1),
                    w_up.astype(BF16), w_down.astype(BF16), ln2_g.reshape(1, -1), ln2_b.reshape(1, -1))
    return h.reshape(bsz, t, D_MODEL), tuple(jnp.stack(z) for z in zip(*new_states))


def _sample_decoder(x, states, emb_g, emb_b, lb_logits, w_in_p, layer_w):
    bsz, t, _ = x.shape
    n = bsz * t
    h = jnp.transpose(x, (1, 0, 2)).reshape(n, D_MODEL)
    ca, sa, cb, sb, sc, dc, dn, dm = states
    states_t = (jnp.transpose(ca, (0, 2, 1, 3)), jnp.transpose(sa, (0, 2, 3, 4, 1)),
                jnp.transpose(cb, (0, 2, 1, 3)), jnp.transpose(sb, (0, 2, 3, 4, 1)),
                jnp.transpose(sc, (0, 2, 3, 4, 1)), jnp.transpose(dc, (0, 2, 3, 4, 1)),
                jnp.transpose(dn, (0, 2, 3, 1)), jnp.transpose(dm, (0, 2, 1)))
    new_states = []
    for l in range(DEPTH):
        (conv_a_w, conv_a_b, a_log_a, dt_bias_a, norm_a_g, conv_b_w, conv_b_b, a_log_b, dt_bias_b,
         d_skip_b, norm_b_g, norm_c_g, i_bias_d, f_bias_d, norm_d_g, w_out, ln1_g, ln1_b, w_up, w_down,
         ln2_g, ln2_b) = [w[l] for w in layer_w]
        if l == 0:
            h, u = _inproj(h, w_in_p, l, emb_g.reshape(1, -1), emb_b.reshape(1, -1), True)
        else:
            (u,) = _inproj(h, w_in_p, l, emb_g.reshape(1, -1), emb_b.reshape(1, -1), False)
        params = (conv_a_w, conv_a_b.reshape(1, -1), conv_b_w, conv_b_b.reshape(1, -1),
                  _gate_row([(GL_DECAY_A, dt_bias_a), (GL_DT_B, dt_bias_b), (GL_I_D, i_bias_d), (GL_F_D, f_bias_d),
                             (GL_DT_B2, dt_bias_b)]),
                  _gate_row([(GL_DECAY_A, a_log_a), (GL_DT_B, a_log_b)]),
                  norm_a_g.reshape(1, -1), norm_b_g.reshape(1, -1), norm_c_g.reshape(1, -1),
                  norm_d_g.reshape(1, -1), jnp.broadcast_to(d_skip_b[:, None, None], (N_HEADS, 1, LANES)),
                  lb_logits)
        y, st = _sample_mixers(u, states_t, params, layer=l, T=t)
        new_states.append(st)
        h = _outffn(y, h, w_out.astype(BF16), ln1_g.reshape(1, -1), ln1_b.reshape(1, -1),
                    w_up.astype(BF16), w_down.astype(BF16), ln2_g.reshape(1, -1), ln2_b.reshape(1, -1))
    ca_n, sa_n, cb_n, sb_n, sc_n, dc_n, dn_n, dm_n = (jnp.stack(z) for z in zip(*new_states))
    out_states = (jnp.transpose(ca_n, (0, 2, 1, 3)), jnp.transpose(sa_n, (0, 4, 1, 2, 3)),
                  jnp.transpose(cb_n, (0, 2, 1, 3)), jnp.transpose(sb_n, (0, 4, 1, 2, 3)),
                  jnp.transpose(sc_n, (0, 4, 1, 2, 3)), jnp.transpose(dc_n, (0, 4, 1, 2, 3)),
                  jnp.transpose(dn_n, (0, 3, 1, 2)), jnp.transpose(dm_n, (0, 2, 1)))
    return jnp.transpose(h.reshape(t, bsz, D_MODEL), (1, 0, 2)), out_states


def kernel(x_prompt, x_sample, state_a_conv, state_a_ssm, state_b_conv, state_b_ssm, state_c_ssm, state_d_cmem, state_d_nvec, state_d_mstab, emb_ln_g, emb_ln_b, lb_logits_c, w_in, conv_a_w, conv_a_b, a_log_a, dt_bias_a, norm_a_g, conv_b_w, conv_b_b, a_log_b, dt_bias_b, d_skip_b, norm_b_g, norm_c_g, i_bias_d, f_bias_d, norm_d_g, w_out, ln1_g, ln1_b, w_up, w_down, ln2_g, ln2_b):
    layer_w = (conv_a_w, conv_a_b, a_log_a, dt_bias_a, norm_a_g, conv_b_w, conv_b_b, a_log_b, dt_bias_b,
               d_skip_b, norm_b_g, norm_c_g, i_bias_d, f_bias_d, norm_d_g, w_out, ln1_g, ln1_b, w_up, w_down,
               ln2_g, ln2_b)
    sample_states = (state_a_conv, state_a_ssm, state_b_conv, state_b_ssm,
                     state_c_ssm, state_d_cmem, state_d_nvec, state_d_mstab)
    prompt_states = tuple(jnp.zeros((1, x_prompt.shape[0]) + s.shape[2:], F32) for s in sample_states)
    w_in_p = _prep_w_in(w_in)
    y_p, ps = _prompt_decoder(x_prompt, prompt_states, emb_ln_g, emb_ln_b, lb_logits_c, w_in_p, layer_w)
    y_s, ss = _sample_decoder(x_sample, sample_states, emb_ln_g, emb_ln_b, lb_logits_c, w_in_p, layer_w)
    return (y_p, y_s) + ps + ss
```
